```python
import math
import jax
import jax.numpy as jnp
from jax import lax
import numpy as np

D_MODEL = 1024
BATCH = 8
SEQ = 4096
DEPTH = 2

GRID_W = 64
CTX_LEN = 256
HEAD_DIM = 64
ATTN_SCALE = HEAD_DIM ** -0.5
ROPE_THETA = 10000.0
Q_BLOCK = 128
EPS = 1e-6

DIFF_HEADS = D_MODEL // 256
DIFF_QK = DIFF_HEADS * 2 * HEAD_DIM
DIFF_V = DIFF_HEADS * 2 * HEAD_DIM
DIFF_IN = 2 * DIFF_QK + DIFF_V
HGRN_HEADS = D_MODEL // 256
HGRN_HEAD_DIM = 128
HGRN_WIDTH = HGRN_HEADS * HGRN_HEAD_DIM
HGRN_IN = 5 * HGRN_WIDTH
HGRN_CHUNK = 64
EVEN_IN = DIFF_IN + HGRN_IN
EVEN_OUT = DIFF_V + HGRN_WIDTH
GQA_HEADS = D_MODEL // 128
GQA_KV_HEADS = GQA_HEADS // 4
GQA_GROUP = GQA_HEADS // GQA_KV_HEADS
GQA_IN = (GQA_HEADS + 2 * GQA_KV_HEADS) * HEAD_DIM
NA_HEADS = D_MODEL // 128
NA_ROWS = 8
NA_COLS = 16
NA_IN = 3 * NA_HEADS * HEAD_DIM
ODD_IN = GQA_IN + NA_IN
ODD_OUT = (GQA_HEADS + NA_HEADS) * HEAD_DIM
N_EXPERTS = 64
N_GROUPS = 8
EXPERTS_PER_GROUP = N_EXPERTS // N_GROUPS
TOPK_GROUPS = 4
TOP_K = 8
EXPERT_DIM = D_MODEL // 4
SHARED_DIM = D_MODEL // 4
ROUTED_SCALE = 2.5
MOE_BLOCK = 256

N_EVEN = (DEPTH + 1) // 2
N_ODD = DEPTH // 2

kernel_name = 'hybrid_flow_backbone'


def rms_norm(x, gain):
    xf = x.astype(jnp.float32)
    y = xf * lax.rsqrt(jnp.mean(xf * xf, axis=-1, keepdims=True) + EPS)
    return (y * gain.astype(jnp.float32)).astype(x.dtype)


def modulate(h, shift, scale):
    return h * (1 + scale[:, None]) + shift[:, None]


def swiglu(x, w_gate, w_up, w_down):
    return (jax.nn.silu(x @ w_gate) * (x @ w_up)) @ w_down


def axial_rope_angles(n, head_dim):
    pos = jnp.arange(n, dtype=jnp.int32)
    row = (pos // GRID_W).astype(jnp.float32)
    col = (pos % GRID_W).astype(jnp.float32)
    axis_dim = head_dim // 2
    inv_freq = ROPE_THETA ** (-jnp.arange(0, axis_dim, 2, dtype=jnp.float32) / axis_dim)
    return row[:, None] * inv_freq, col[:, None] * inv_freq


def apply_axial_rope(x, ang_row, ang_col):
    axis_dim = x.shape[-1] // 2

    def rotate(seg, ang):
        s1, s2 = jnp.split(seg, 2, axis=-1)
        cos = jnp.cos(ang).astype(seg.dtype)
        sin = jnp.sin(ang).astype(seg.dtype)
        return jnp.concatenate([s1 * cos - s2 * sin, s2 * cos + s1 * sin], axis=-1)

    return jnp.concatenate([rotate(x[..., :axis_dim], ang_row), rotate(x[..., axis_dim:], ang_col)], axis=-1)


def sweep_query_blocks(fn, q, seq_axis):
    n = q.shape[seq_axis]
    nb = n // Q_BLOCK
    qm = jnp.moveaxis(q, seq_axis, 0)
    qm = qm.reshape((nb, Q_BLOCK) + qm.shape[1:])
    blocks = jnp.moveaxis(qm, 1, seq_axis + 1)
    out = lax.map(fn, blocks)
    out = jnp.moveaxis(out, 0, -3)
    return out.reshape(out.shape[:-3] + (n, out.shape[-1]))


def diff_attend(q, k, v, lam):
    s = jnp.einsum('bhcqd,bhckd->bhcqk', q, k).astype(jnp.float32) * ATTN_SCALE
    p = jax.nn.softmax(s, axis=-1)
    a = p[:, :, 0] - lam * p[:, :, 1]
    return jnp.einsum('bhqk,bhkd->bhqd', a.astype(v.dtype), v)


def differential_attention(px, pc, qk_gain, lam_params, out_gain, layer_idx, ang_row, ang_col, need_ctx):
    def heads(p):
        b, n, _ = p.shape
        q = p[..., :DIFF_QK].reshape(b, n, DIFF_HEADS, 2, HEAD_DIM).transpose(0, 2, 3, 1, 4)
        k = p[..., DIFF_QK:2 * DIFF_QK].reshape(b, n, DIFF_HEADS, 2, HEAD_DIM).transpose(0, 2, 3, 1, 4)
        v = p[..., 2 * DIFF_QK:].reshape(b, n, DIFF_HEADS, 2 * HEAD_DIM).transpose(0, 2, 1, 3)
        return rms_norm(q, qk_gain[0]), rms_norm(k, qk_gain[1]), v

    qx, kx, vx = heads(px)
    qc, kc, vc = heads(pc)
    qx = apply_axial_rope(qx, ang_row, ang_col)
    kx = apply_axial_rope(kx, ang_row, ang_col)
    lam_init = 0.8 - 0.6 * math.exp(-0.3 * layer_idx)
    lp = lam_params.astype(jnp.float32)
    lam = jnp.exp(jnp.sum(lp[0] * lp[1])) - jnp.exp(jnp.sum(lp[2] * lp[3])) + lam_init
    k_all = jnp.concatenate([kc, kx], axis=3)
    v_all = jnp.concatenate([vc, vx], axis=2)

    def finish(o):
        b, h, n, dv = o.shape
        o = rms_norm(o, out_gain) * (1.0 - lam_init)
        return o.transpose(0, 2, 1, 3).reshape(b, n, h * dv)

    o_x = sweep_query_blocks(lambda qb: diff_attend(qb, k_all, v_all, lam), qx, 3)
    o_c = finish(diff_attend(qc, kc, vc, lam)) if need_ctx else None
    return finish(o_x), o_c


def gla_chunkwise(q, k, v, log_f, s0):
    b, h, n, dk = q.shape
    dv = v.shape[-1]
    nc = n // HGRN_CHUNK

    def chunks(t):
        return jnp.moveaxis(t.reshape(b, h, nc, HGRN_CHUNK, t.shape[-1]), 2, 0)

    tri = jnp.tril(jnp.ones((HGRN_CHUNK, HGRN_CHUNK), dtype=bool))[:, :, None]

    def step(state, blk):
        qc, kc, vc, gc = blk
        cum = jnp.cumsum(gc, axis=2)
        tot = cum[:, :, -1:, :]
        decay_ts = jnp.exp(jnp.where(tri, cum[:, :, :, None, :] - cum[:, :, None, :, :], -jnp.inf))
        scores = jnp.einsum('bhtk,bhsk,bhtsk->bhts', qc, kc, decay_ts)
        out = jnp.einsum('bhts,bhsv->bhtv', scores, vc) + jnp.einsum('bhtk,bhkv->bhtv', qc * jnp.exp(cum), state)
        new_state = jnp.exp(tot)[:, :, 0, :, None] * state + jnp.einsum('bhsk,bhsv->bhkv', kc * jnp.exp(tot - cum), vc)
        return new_state, out

    s_fin, outs = lax.scan(step, s0, (chunks(q), chunks(k), chunks(v), chunks(log_f)))
    return s_fin, jnp.moveaxis(outs, 0, 2).reshape(b, h, n, dv)


def hgrn2_bidirectional(px, pc, lb_fwd, lb_bwd, out_gain, need_ctx):
    def prep(p):
        b, n, _ = p.shape
        q, f_fwd, f_bwd, i, g = jnp.split(p, [HGRN_WIDTH, 2 * HGRN_WIDTH, 3 * HGRN_WIDTH, 4 * HGRN_WIDTH], axis=-1)

        def to_heads(t):
            return t.reshape(b, n, HGRN_HEADS, HGRN_HEAD_DIM).transpose(0, 2, 1, 3).astype(jnp.float32)

        def forget(raw, lb):
            lb = lb.reshape(HGRN_HEADS, 1, HGRN_HEAD_DIM)
            f = lb + (1.0 - lb) * jax.nn.sigmoid(to_heads(raw))
            return 1.0 - f, jnp.log(f)

        return jax.nn.silu(to_heads(q)), forget(f_fwd, lb_fwd), forget(f_bwd, lb_bwd), to_heads(i), g

    qx, (kx_f, lfx_f), (kx_b, lfx_b), vx, gx = prep(px)
    qc, (kc_f, lfc_f), (kc_b, lfc_b), vc, gc = prep(pc)
    bsz = px.shape[0]
    s0 = jnp.zeros((bsz, HGRN_HEADS, HGRN_HEAD_DIM, HGRN_HEAD_DIM), jnp.float32)

    def flip(t):
        return jnp.flip(t, axis=2)

    s_cf, oc_f = gla_chunkwise(qc, kc_f, vc, lfc_f, s0)
    _, ox_f = gla_chunkwise(qx, kx_f, vx, lfx_f, s_cf)
    s_cb, oc_b = gla_chunkwise(flip(qc), flip(kc_b), flip(vc), flip(lfc_b), s0)
    _, ox_b = gla_chunkwise(flip(qx), flip(kx_b), flip(vx), flip(lfx_b), s_cb)

    def finish(o, g):
        b, h, n, dv = o.shape
        o = rms_norm(o, out_gain).transpose(0, 2, 1, 3).reshape(b, n, h * dv)
        return o.astype(g.dtype) * jax.nn.silu(g)

    o_c = finish(oc_f + flip(oc_b), gc) if need_ctx else None
    return finish(ox_f + flip(ox_b), gx), o_c


def even_layer_mixer(hx, hc, w_in, w_out, qk_gain, lam_params, diff_gain, lb_fwd, lb_bwd, hgrn_gain, layer_idx, ang_row, ang_col, need_ctx):
    px = hx @ w_in
    pc = hc @ w_in
    a_x, a_c = differential_attention(px[..., :DIFF_IN], pc[..., :DIFF_IN], qk_gain, lam_params, diff_gain, layer_idx, ang_row, ang_col, need_ctx)
    b_x, b_c = hgrn2_bidirectional(px[..., DIFF_IN:], pc[..., DIFF_IN:], lb_fwd, lb_bwd, hgrn_gain, need_ctx)
    yx = jnp.concatenate([a_x, b_x], axis=-1) @ w_out
    yc = jnp.concatenate([a_c, b_c], axis=-1) @ w_out if need_ctx else None
    return yx, yc


def gqa_attend(q, k, v):
    s = jnp.einsum('bkgqd,bkmd->bkgqm', q, k).astype(jnp.float32) * ATTN_SCALE
    p = jax.nn.softmax(s, axis=-1).astype(v.dtype)
    return jnp.einsum('bkgqm,bkmd->bkgqd', p, v)


def gqa_attention(px, pc, qk_gain, ang_row, ang_col, need_ctx):
    def heads(p):
        b, n, _ = p.shape
        q, k, v = jnp.split(p, [GQA_HEADS * HEAD_DIM, (GQA_HEADS + GQA_KV_HEADS) * HEAD_DIM], axis=-1)
        q = rms_norm(q.reshape(b, n, GQA_KV_HEADS, GQA_GROUP, HEAD_DIM).transpose(0, 2, 3, 1, 4), qk_gain[0])
        k = rms_norm(k.reshape(b, n, GQA_KV_HEADS, HEAD_DIM).transpose(0, 2, 1, 3), qk_gain[1])
        v = v.reshape(b, n, GQA_KV_HEADS, HEAD_DIM).transpose(0, 2, 1, 3)
        return q, k, v

    qx, kx, vx = heads(px)
    qc, kc, vc = heads(pc)
    qx = apply_axial_rope(qx, ang_row, ang_col)
    kx = apply_axial_rope(kx, ang_row, ang_col)
    k_all = jnp.concatenate([kc, kx], axis=2)
    v_all = jnp.concatenate([vc, vx], axis=2)

    def finish(o):
        b, kv, g, n, d = o.shape
        return o.transpose(0, 3, 1, 2, 4).reshape(b, n, kv * g * d)

    o_x = sweep_query_blocks(lambda qb: gqa_attend(qb, k_all, v_all), qx, 3)
    o_c = finish(gqa_attend(qc, kc, vc)) if need_ctx else None
    return finish(o_x), o_c


def neighbourhood_attention(px, pc, qk_gain, rpb, need_ctx):
    def heads(p):
        b, n, _ = p.shape
        q, k, v = jnp.split(p, 3, axis=-1)
        sh = lambda t: t.reshape(b, n, NA_HEADS, HEAD_DIM).transpose(0, 2, 1, 3)
        return rms_norm(sh(q), qk_gain[0]), rms_norm(sh(k), qk_gain[1]), sh(v)

    qx, kx, vx = heads(px)
    qc, kc, vc = heads(pc)
    b, h, n, d = qx.shape
    n_ctx = kc.shape[2]
    rows = n // GRID_W
    win_rows = min(NA_ROWS, rows)
    col = jnp.arange(GRID_W)
    col_start = jnp.clip(col - NA_COLS // 2, 0, GRID_W - NA_COLS)
    col_idx = col_start[:, None] + jnp.arange(NA_COLS)
    col_bias_idx = col_idx - col[:, None] + NA_COLS - 1
    qg = qx.reshape(b, h, rows, GRID_W, d)
    kg = kx.reshape(b, h, rows, GRID_W, d)
    vg = vx.reshape(b, h, rows, GRID_W, d)

    def row_block(r):
        r_start = jnp.clip(r - win_rows // 2, 0, rows - win_rows)
        q_row = lax.dynamic_index_in_dim(qg, r, axis=2, keepdims=False)
        k_band = lax.dynamic_slice_in_dim(kg, r_start, win_rows, axis=2)
        v_band = lax.dynamic_slice_in_dim(vg, r_start, win_rows, axis=2)
        k_win = k_band[:, :, :, col_idx]
        v_win = v_band[:, :, :, col_idx]
        row_bias_idx = r_start + jnp.arange(win_rows) - r + NA_ROWS - 1
        bias = rpb[:, row_bias_idx[None, :, None], col_bias_idx[:, None, :]]
        s_win = jnp.einsum('bhqd,bhrqcd->bhqrc', q_row, k_win).astype(jnp.float32) * ATTN_SCALE + bias.astype(jnp.float32)
        s_ctx = jnp.einsum('bhqd,bhmd->bhqm', q_row, kc).astype(jnp.float32) * ATTN_SCALE
        s = jnp.concatenate([s_ctx, s_win.reshape(b, h, GRID_W, win_rows * NA_COLS)], axis=-1)
        p = jax.nn.softmax(s, axis=-1).astype(vx.dtype)
        p_win = p[..., n_ctx:].reshape(b, h, GRID_W, win_rows, NA_COLS)
        return jnp.einsum('bhqm,bhmd->bhqd', p[..., :n_ctx], vc) + jnp.einsum('bhqrc,bhrqcd->bhqd', p_win, v_win)

    o = lax.map(row_block, jnp.arange(rows))
    o_x = jnp.moveaxis(o, 0, 2).reshape(b, h, n, d).transpose(0, 2, 1, 3).reshape(b, n, h * d)
    o_c = None
    if need_ctx:
        pcs = jax.nn.softmax(jnp.einsum('bhqd,bhmd->bhqm', qc, kc).astype(jnp.float32) * ATTN_SCALE, axis=-1)
        oc = jnp.einsum('bhqm,bhmd->bhqd', pcs.astype(vc.dtype), vc)
        o_c = oc.transpose(0, 2, 1, 3).reshape(b, n_ctx, h * d)
    return o_x, o_c


def odd_layer_mixer(hx, hc, w_in, w_out, gqa_gain, na_gain, rpb, ang_row, ang_col, need_ctx):
    px = hx @ w_in
    pc = hc @ w_in
    c_x, c_c = gqa_attention(px[..., :GQA_IN], pc[..., :GQA_IN], gqa_gain, ang_row, ang_col, need_ctx)
    d_x, d_c = neighbourhood_attention(px[..., GQA_IN:], pc[..., GQA_IN:], na_gain, rpb, need_ctx)
    yx = jnp.concatenate([c_x, d_x], axis=-1) @ w_out
    yc = jnp.concatenate([c_c, d_c], axis=-1) @ w_out if need_ctx else None
    return yx, yc


def moe_ffn(tokens, router, router_bias, w_gate, w_up, w_down, ws_gate, ws_up, ws_down):
    t = tokens.shape[0]
    scores = jax.nn.sigmoid(tokens.astype(jnp.float32) @ router.astype(jnp.float32))
    biased = scores + router_bias.astype(jnp.float32)
    group_score = lax.top_k(biased.reshape(t, N_GROUPS, EXPERTS_PER_GROUP), 2)[0].sum(-1)
    _, top_groups = lax.top_k(group_score, TOPK_GROUPS)
    group_mask = (top_groups[:, :, None] == jnp.arange(N_GROUPS)).any(axis=1)
    expert_mask = jnp.repeat(group_mask, EXPERTS_PER_GROUP, axis=1)
    _, idx = lax.top_k(jnp.where(expert_mask, biased, -jnp.inf), TOP_K)
    w = jnp.take_along_axis(scores, idx, axis=1)
    w = w / jnp.sum(w, axis=-1, keepdims=True) * ROUTED_SCALE
    n_assign = t * TOP_K
    flat_e = idx.reshape(-1)
    order = jnp.argsort(flat_e)
    e_sorted = flat_e[order]
    tok_sorted = (order // TOP_K).astype(jnp.int32)
    w_sorted = w.reshape(-1)[order]
    counts = jnp.bincount(flat_e, length=N_EXPERTS)
    padded = (counts + MOE_BLOCK - 1) // MOE_BLOCK * MOE_BLOCK
    start = jnp.cumsum(counts) - counts
    pad_end = jnp.cumsum(padded)
    pad_start = pad_end - padded
    dest = pad_start[e_sorted] + jnp.arange(n_assign) - start[e_sorted]
    n_rows = ((n_assign + MOE_BLOCK - 1) // MOE_BLOCK + N_EXPERTS) * MOE_BLOCK
    row_tok = jnp.zeros((n_rows,), jnp.int32).at[dest].set(tok_sorted)
    row_w = jnp.zeros((n_rows,), jnp.float32).at[dest].set(w_sorted)
    n_blocks = n_rows // MOE_BLOCK
    block_start = jnp.arange(n_blocks) * MOE_BLOCK
    block_expert = jnp.minimum(jnp.sum(block_start[:, None] >= pad_end[None, :], axis=1), N_EXPERTS - 1)
    shared = swiglu(tokens, ws_gate, ws_up, ws_down)

    def expert_block(acc, blk):
        rows, wts, e = blk
        y = swiglu(tokens[rows], w_gate[e], w_up[e], w_down[e])
        return acc.at[rows].add((y.astype(jnp.float32) * wts[:, None]).astype(acc.dtype)), None

    out, _ = lax.scan(expert_block, shared, (row_tok.reshape(n_blocks, MOE_BLOCK), row_w.reshape(n_blocks, MOE_BLOCK), block_expert))
    return out


def setup_inputs(seed: int = 0) -> dict:
    key = jax.random.key(seed)
    keys = iter(jax.random.split(key, 40))
    D = D_MODEL

    def nrm(shape, scale):
        return jax.random.normal(next(keys), shape, jnp.float32) * scale

    def gain(shape):
        return 1.0 + nrm(shape, 0.1)

    return {
        'x': nrm((BATCH, SEQ, D), 1.0),
        'c': nrm((BATCH, D), 1.0),
        'ctx': nrm((BATCH, CTX_LEN, D), 1.0),
        'c_ctx': nrm((D,), 1.0),
        'ada_w': nrm((DEPTH, D, 6 * D), 0.5 * D ** -0.5),
        'ada_b': nrm((DEPTH, 6 * D), 0.02),
        'norm_mix': gain((DEPTH, D)),
        'norm_ffn': gain((DEPTH, D)),
        'ev_w_in': nrm((N_EVEN, D, EVEN_IN), D ** -0.5),
        'ev_w_out': nrm((N_EVEN, EVEN_OUT, D), EVEN_OUT ** -0.5),
        'diff_qk_gain': gain((N_EVEN, 2, HEAD_DIM)),
        'diff_lambda': nrm((N_EVEN, 4, HEAD_DIM), 0.1),
        'diff_out_gain': gain((N_EVEN, 2 * HEAD_DIM)),
        'hgrn_lb': nrm((2, N_EVEN + 1, HGRN_WIDTH), 0.5),
        'hgrn_out_gain': gain((N_EVEN, HGRN_HEAD_DIM)),
        'od_w_in': nrm((N_ODD, D, ODD_IN), D ** -0.5),
        'od_w_out': nrm((N_ODD, ODD_OUT, D), ODD_OUT ** -0.5),
        'gqa_qk_gain': gain((N_ODD, 2, HEAD_DIM)),
        'na_qk_gain': gain((N_ODD, 2, HEAD_DIM)),
        'na_rpb': nrm((N_ODD, NA_HEADS, 2 * NA_ROWS - 1, 2 * NA_COLS - 1), 0.3),
        'moe_router': nrm((DEPTH, D, N_EXPERTS), D ** -0.5),
        'moe_router_bias': nrm((DEPTH, N_EXPERTS), 0.01),
        'moe_w_gate': nrm((DEPTH, N_EXPERTS, D, EXPERT_DIM), D ** -0.5),
        'moe_w_up': nrm((DEPTH, N_EXPERTS, D, EXPERT_DIM), D ** -0.5),
        'moe_w_down': nrm((DEPTH, N_EXPERTS, EXPERT_DIM, D), EXPERT_DIM ** -0.5),
        'shared_w_gate': nrm((DEPTH, D, SHARED_DIM), D ** -0.5),
        'shared_w_up': nrm((DEPTH, D, SHARED_DIM), D ** -0.5),
        'shared_w_down': nrm((DEPTH, SHARED_DIM, D), SHARED_DIM ** -0.5),
    }


def reference(x, c, ctx, c_ctx, ada_w, ada_b, norm_mix, norm_ffn, ev_w_in, ev_w_out, diff_qk_gain, diff_lambda, diff_out_gain, hgrn_lb, hgrn_out_gain, od_w_in, od_w_out, gqa_qk_gain, na_qk_gain, na_rpb, moe_router, moe_router_bias, moe_w_gate, moe_w_up, moe_w_down, shared_w_gate, shared_w_up, shared_w_down):
    bsz, n_lat, d = x.shape
    ang_row, ang_col = axial_rope_angles(n_lat, HEAD_DIM)
    lower_bounds = jnp.cumsum(jax.nn.softmax(hgrn_lb.astype(jnp.float32), axis=1), axis=1)
    cond_x = jax.nn.silu(c)
    cond_c = jax.nn.silu(c_ctx)[None, :]
    xc = ctx
    for layer in range(DEPTH):
        need_ctx = layer < DEPTH - 1
        mod_x = jnp.split(cond_x @ ada_w[layer] + ada_b[layer], 6, axis=-1)
        mod_c = jnp.split(cond_c @ ada_w[layer] + ada_b[layer], 6, axis=-1)
        hx = modulate(rms_norm(x, norm_mix[layer]), mod_x[0], mod_x[1])
        hc = modulate(rms_norm(xc, norm_mix[layer]), mod_c[0], mod_c[1])
        j = layer // 2
        if layer % 2 == 0:
            yx, yc = even_layer_mixer(hx, hc, ev_w_in[j], ev_w_out[j], diff_qk_gain[j], diff_lambda[j], diff_out_gain[j], lower_bounds[0, j], lower_bounds[1, j], hgrn_out_gain[j], layer, ang_row, ang_col, need_ctx)
        else:
            yx, yc = odd_layer_mixer(hx, hc, od_w_in[j], od_w_out[j], gqa_qk_gain[j], na_qk_gain[j], na_rpb[j], ang_row, ang_col, need_ctx)
        x = x + mod_x[2][:, None] * yx
        hx = modulate(rms_norm(x, norm_ffn[layer]), mod_x[3], mod_x[4])
        run_moe = lambda toks: moe_ffn(toks, moe_router[layer], moe_router_bias[layer], moe_w_gate[layer], moe_w_up[layer], moe_w_down[layer], shared_w_gate[layer], shared_w_up[layer], shared_w_down[layer])
        if need_ctx:
            xc = xc + mod_c[2][:, None] * yc
            hc = modulate(rms_norm(xc, norm_ffn[layer]), mod_c[3], mod_c[4])
            out = run_moe(jnp.concatenate([hx.reshape(-1, d), hc.reshape(-1, d)], axis=0))
            x = x + mod_x[5][:, None] * out[: bsz * n_lat].reshape(x.shape)
            xc = xc + mod_c[5][:, None] * out[bsz * n_lat:].reshape(xc.shape)
        else:
            x = x + mod_x[5][:, None] * run_moe(hx.reshape(-1, d)).reshape(x.shape)
    return x
```

```python
import functools
import math

import jax
import jax.numpy as jnp
from jax import lax
from jax.experimental import pallas as pl
from jax.experimental.pallas import tpu as pltpu

F32 = jnp.float32
BF16 = jnp.bfloat16
I32 = jnp.int32
HIGHEST = lax.Precision.HIGHEST

D_MODEL = 1024
GRID_W = 64
HEAD_DIM = 64
ATTN_SCALE = HEAD_DIM ** -0.5
ROPE_THETA = 10000.0
EPS = 1e-6
DIFF_HEADS = D_MODEL // 256
HGRN_HEADS = D_MODEL // 256
HGRN_CHUNK = 64
HGRN_SUB = 16
GQA_HEADS = D_MODEL // 128
GQA_KV_HEADS = GQA_HEADS // 4
NA_HEADS = D_MODEL // 128
NA_ROWS = 8
NA_COLS = 16
N_EXPERTS = 64
N_GROUPS = 8
EXPERTS_PER_GROUP = N_EXPERTS // N_GROUPS
TOPK_GROUPS = 4
TOP_K = 8
EXPERT_DIM = D_MODEL // 4
ROUTED_SCALE = 2.5
HALF = D_MODEL // 2

LANES = 128
VMEM_LIMIT_BYTES = 56 * 1024 * 1024
PROJ_ROWS = 512
FLASH_TK = 512
DIFF_TQ = 512
GQA_TQ = 256
NA_QROWS = 8
NA_BAND = 16
ROUTE_ROWS = 512
MOE_BLOCK = 256
MOE_TOK = 256
NEG_BIG = -1e30


def _cparams(sem):
    return pltpu.CompilerParams(dimension_semantics=sem, vmem_limit_bytes=VMEM_LIMIT_BYTES)


def _silu(x):
    return x * jax.nn.sigmoid(x)


def _dot(a, b):
    return jnp.dot(a, b, preferred_element_type=F32)


def _dot_nt(a, b):
    return lax.dot_general(a, b, (((1,), (1,)), ((), ())), preferred_element_type=F32)


def _dot_tn(a, b):
    return lax.dot_general(a, b, (((0,), (0,)), ((), ())), preferred_element_type=F32)


def _pack_pair(lo, hi):
    lo_bits = lax.bitcast_convert_type(lo.astype(BF16).astype(F32), I32)
    hi_bits = lax.bitcast_convert_type(hi.astype(BF16).astype(F32), I32)
    return lax.shift_right_logical(lo_bits, 16) | (hi_bits & jnp.int32(-65536))


def _unpack_pair(w):
    lo = lax.bitcast_convert_type(lax.shift_left(w, 16), F32)
    hi = lax.bitcast_convert_type(w & jnp.int32(-65536), F32)
    return lo, hi


def _ada_kernel(cond_ref, w_ref, b_ref, o_ref):
    s = _silu(cond_ref[...])
    o_ref[...] = jnp.dot(s, w_ref[...], precision=HIGHEST, preferred_element_type=F32) + b_ref[...]


def _ada_mod(cond, ada_w, ada_b):
    depth = ada_w.shape[0]
    rows = cond.shape[0]
    nblk = ada_w.shape[2] // D_MODEL
    return pl.pallas_call(
        _ada_kernel,
        grid=(depth, nblk),
        in_specs=[
            pl.BlockSpec((rows, D_MODEL), lambda l, j: (0, 0)),
            pl.BlockSpec((None, D_MODEL, D_MODEL), lambda l, j: (l, 0, j)),
            pl.BlockSpec((None, 1, D_MODEL), lambda l, j: (l, 0, j)),
        ],
        out_specs=pl.BlockSpec((None, rows, D_MODEL), lambda l, j: (l, 0, j)),
        out_shape=jax.ShapeDtypeStruct((depth, rows, nblk * D_MODEL), F32),
        compiler_params=_cparams(("parallel", "parallel")),
        name="ada_mod",
    )(cond, ada_w, ada_b.reshape(depth, 1, -1))


def _norm_mod(x, gain, shift, scale):
    ms = jnp.mean(x * x, axis=-1, keepdims=True)
    h = x * lax.rsqrt(ms + EPS) * gain
    return h * (1.0 + scale) + shift


def _seg_rms(acc, gain, bd):
    sq = acc * acc
    hi = sq.astype(BF16)
    lo = (sq - hi.astype(F32)).astype(BF16)
    ms = _dot(hi, bd) + _dot(lo, bd)
    return acc * lax.rsqrt(ms + EPS) * gain


def _rope(y, c, sm, sp):
    return y * c + pltpu.roll(y, LANES - 16, 1) * sm + pltpu.roll(y, 16, 1) * sp


def _rope_tables(n):
    pos = jnp.arange(n, dtype=I32)
    row = (pos // GRID_W).astype(F32)
    col = (pos % GRID_W).astype(F32)
    axis_dim = HEAD_DIM // 2
    inv_freq = ROPE_THETA ** (-jnp.arange(0, axis_dim, 2, dtype=F32) / axis_dim)
    ang_row = row[:, None] * inv_freq
    ang_col = col[:, None] * inv_freq
    lane = jnp.arange(LANES)
    p = lane % axis_dim
    f = p % (axis_dim // 2)
    on_row = ((lane % HEAD_DIM) // axis_dim) == 0
    ang = jnp.where(on_row[None, :], ang_row[:, f], ang_col[:, f])
    c = jnp.cos(ang)
    s = jnp.sin(ang)
    first = (p < axis_dim // 2)[None, :]
    return c, jnp.where(first, -s, 0.0), jnp.where(first, 0.0, s)


def _seg_mean_matrix():
    r = jnp.arange(LANES)
    return jnp.where((r[:, None] // HEAD_DIM) == (r[None, :] // HEAD_DIM), 1.0 / HEAD_DIM, 0.0).astype(BF16)


def _even_proj_kernel(*refs, rope, layer_slot):
    if rope:
        (x_ref, shift_ref, scale_ref, gain_ref, w_ref, qkg_ref, lbf_ref, lbb_ref, bd_ref, rc_ref, rm_ref, rp_ref,
         dq_ref, dk_ref, dv_ref, hq_ref, kf_ref, gf_ref, kb_ref, gb_ref, hv_ref, hg_ref) = refs
        tables = (rc_ref[...], rm_ref[...], rp_ref[...])
    else:
        (x_ref, shift_ref, scale_ref, gain_ref, w_ref, qkg_ref, lbf_ref, lbb_ref, bd_ref,
         dq_ref, dk_ref, dv_ref, hq_ref, kf_ref, gf_ref, kb_ref, gb_ref, hv_ref, hg_ref) = refs
        tables = None
    hb = _norm_mod(x_ref[...], gain_ref[...], shift_ref[...], scale_ref[...]).astype(BF16)
    bd = bd_ref[...]
    width = 4 * LANES

    def proj(group):
        return _dot(hb, w_ref[:, group * width:(group + 1) * width])

    def qk(group, gain, out_ref, mult):
        acc = proj(group)
        for s in range(4):
            y = _seg_rms(acc[:, s * LANES:(s + 1) * LANES], gain, bd)
            if tables is not None:
                y = _rope(y, *tables)
            out_ref[:, s * LANES:(s + 1) * LANES] = (y * mult).astype(BF16)

    qk(0, qkg_ref[0:1, :], dq_ref, ATTN_SCALE)
    qk(1, qkg_ref[1:2, :], dk_ref, 1.0)
    dv_ref[...] = proj(2).astype(BF16)
    hq_ref[...] = _silu(proj(3)).astype(BF16)

    def forget(group, lb_ref, k_ref, g_ref):
        raw = lb_ref[...]
        e = jnp.exp(raw - jnp.max(raw, axis=0, keepdims=True))
        lb = jnp.sum(e[0:layer_slot + 1, :], axis=0, keepdims=True) / jnp.sum(e, axis=0, keepdims=True)
        f = lb + (1.0 - lb) * jax.nn.sigmoid(proj(group))
        k_ref[...] = (1.0 - f).astype(BF16)
        g_ref[...] = jnp.log(f)

    forget(4, lbf_ref, kf_ref, gf_ref)
    forget(5, lbb_ref, kb_ref, gb_ref)
    hv_ref[...] = proj(6).astype(BF16)
    hg_ref[...] = _silu(proj(7)).astype(BF16)


def _row_spec(tm, width):
    return pl.BlockSpec((None, tm, width), lambda b, i: (b, i, 0))


def _bcast_spec(width):
    return pl.BlockSpec((None, 1, width), lambda b, i: (b, 0, 0))


def _const_spec(shape):
    nd = len(shape)
    return pl.BlockSpec(shape, lambda b, i: (0,) * nd)


def _even_proj(x, shift, scale, gain, w, qk_gain, lb_fwd, lb_bwd, layer_slot, rope):
    bsz, n, d = x.shape
    tm = min(PROJ_ROWS, n)
    width = 4 * LANES
    qkg = jnp.tile(qk_gain.astype(F32), (1, 2))
    in_specs = [
        _row_spec(tm, d), _bcast_spec(d), _bcast_spec(d), _const_spec((1, d)), _const_spec(w.shape),
        _const_spec((2, LANES)), _const_spec(lb_fwd.shape), _const_spec(lb_bwd.shape), _const_spec((LANES, LANES)),
    ]
    args = [x, shift, scale, gain.reshape(1, d), w, qkg, lb_fwd, lb_bwd, _seg_mean_matrix()]
    if rope:
        tab_spec = pl.BlockSpec((tm, LANES), lambda b, i: (i, 0))
        in_specs += [tab_spec] * 3
        args += list(_rope_tables(n))
    out_dtypes = [BF16, BF16, BF16, BF16, BF16, F32, BF16, F32, BF16, BF16]
    return pl.pallas_call(
        functools.partial(_even_proj_kernel, rope=rope, layer_slot=layer_slot),
        grid=(bsz, n // tm),
        in_specs=in_specs,
        out_specs=[_row_spec(tm, width)] * len(out_dtypes),
        out_shape=[jax.ShapeDtypeStruct((bsz, n, width), dt) for dt in out_dtypes],
        compiler_params=_cparams(("parallel", "parallel")),
        name="even_proj_x" if rope else "even_proj_ctx",
    )(*args)


def _odd_proj_kernel(*refs, rope):
    if rope:
        (x_ref, shift_ref, scale_ref, gain_ref, w_ref, gqg_ref, nag_ref, bd_ref, rc_ref, rm_ref, rp_ref,
         gq_ref, gkv_ref, nq_ref, nk_ref, nv_ref) = refs
        tables = (rc_ref[...], rm_ref[...], rp_ref[...])
    else:
        (x_ref, shift_ref, scale_ref, gain_ref, w_ref, gqg_ref, nag_ref, bd_ref,
         gq_ref, gkv_ref, nq_ref, nk_ref, nv_ref) = refs
        tables = None
    hb = _norm_mod(x_ref[...], gain_ref[...], shift_ref[...], scale_ref[...]).astype(BF16)
    bd = bd_ref[...]

    def slab(acc, s, gain, use_rope, mult):
        y = _seg_rms(acc[:, s * LANES:(s + 1) * LANES], gain, bd)
        if use_rope and tables is not None:
            y = _rope(y, *tables)
        return (y * mult).astype(BF16)

    q_w = GQA_HEADS * HEAD_DIM
    acc = _dot(hb, w_ref[:, 0:q_w])
    for s in range(q_w // LANES):
        gq_ref[:, s * LANES:(s + 1) * LANES] = slab(acc, s, gqg_ref[0:1, :], True, ATTN_SCALE)
    acc = _dot(hb, w_ref[:, q_w:q_w + 2 * LANES])
    gkv_ref[:, 0:LANES] = slab(acc, 0, gqg_ref[1:2, :], True, 1.0)
    gkv_ref[:, LANES:2 * LANES] = acc[:, LANES:2 * LANES].astype(BF16)
    base = q_w + 2 * LANES
    na_w = NA_HEADS * HEAD_DIM
    acc = _dot(hb, w_ref[:, base:base + na_w])
    for s in range(na_w // LANES):
        nq_ref[:, s * LANES:(s + 1) * LANES] = slab(acc, s, nag_ref[0:1, :], False, ATTN_SCALE)
    acc = _dot(hb, w_ref[:, base + na_w:base + 2 * na_w])
    for s in range(na_w // LANES):
        nk_ref[:, s * LANES:(s + 1) * LANES] = slab(acc, s, nag_ref[1:2, :], False, 1.0)
    nv_ref[...] = _dot(hb, w_ref[:, base + 2 * na_w:base + 3 * na_w]).astype(BF16)


def _odd_proj(x, shift, scale, gain, w, gqa_gain, na_gain, rope):
    bsz, n, d = x.shape
    tm = min(PROJ_ROWS, n)
    gqg = jnp.tile(gqa_gain.astype(F32), (1, 2))
    nag = jnp.tile(na_gain.astype(F32), (1, 2))
    in_specs = [
        _row_spec(tm, d), _bcast_spec(d), _bcast_spec(d), _const_spec((1, d)), _const_spec(w.shape),
        _const_spec((2, LANES)), _const_spec((2, LANES)), _const_spec((LANES, LANES)),
    ]
    args = [x, shift, scale, gain.reshape(1, d), w, gqg, nag, _seg_mean_matrix()]
    if rope:
        tab_spec = pl.BlockSpec((tm, LANES), lambda b, i: (i, 0))
        in_specs += [tab_spec] * 3
        args += list(_rope_tables(n))
    widths = [GQA_HEADS * HEAD_DIM, 2 * LANES, NA_HEADS * HEAD_DIM, NA_HEADS * HEAD_DIM, NA_HEADS * HEAD_DIM]
    return pl.pallas_call(
        functools.partial(_odd_proj_kernel, rope=rope),
        grid=(bsz, n // tm),
        in_specs=in_specs,
        out_specs=[_row_spec(tm, wd) for wd in widths],
        out_shape=[jax.ShapeDtypeStruct((bsz, n, wd), BF16) for wd in widths],
        compiler_params=_cparams(("parallel", "parallel")),
        name="odd_proj_x" if rope else "odd_proj_ctx",
    )(*args)


def _flash(qs, sources, tk):
    rows = qs.shape[0]

    def step(k, v, carry):
        m, l, acc = carry
        s = _dot_nt(qs, k)
        m_new = jnp.maximum(m, jnp.max(s, axis=-1, keepdims=True))
        alpha = jnp.exp(m - m_new)
        p = jnp.exp(s - m_new)
        l = alpha * l + jnp.sum(p, axis=-1, keepdims=True)
        acc = alpha * acc + _dot(p.astype(BF16), v)
        return m_new, l, acc

    carry = (jnp.full((rows, 1), -jnp.inf, F32), jnp.zeros((rows, 1), F32), jnp.zeros((rows, LANES), F32))
    for k_ref, v_ref, length in sources:
        chunk = min(tk, length)
        steps = length // chunk
        if steps == 1:
            carry = step(k_ref[...], v_ref[...], carry)
        else:
            def body(i, c, k_ref=k_ref, v_ref=v_ref, chunk=chunk):
                off = pl.multiple_of(i * chunk, chunk)
                return step(k_ref[pl.ds(off, chunk), :], v_ref[pl.ds(off, chunk), :], c)
            carry = lax.fori_loop(0, steps, body, carry)
    m, l, acc = carry
    return acc / l


def _lane_ids(shape):
    return lax.broadcasted_iota(I32, shape, len(shape) - 1)


def _diff_attn_kernel(*refs, n_src, lens, lam_init, tk):
    q_ref = refs[0]
    kv_refs = refs[1:1 + 2 * n_src]
    lam_ref, gain_ref, o_ref = refs[1 + 2 * n_src:]
    q = q_ref[...]
    tq = q.shape[0]
    lo = _lane_ids(q.shape) < HEAD_DIM
    zero = jnp.zeros_like(q)
    qs = jnp.concatenate([jnp.where(lo, q, zero), jnp.where(lo, zero, q)], axis=0)
    sources = [(kv_refs[2 * i], kv_refs[2 * i + 1], lens[i]) for i in range(n_src)]
    a = _flash(qs, sources, tk)
    lp = lam_ref[...]
    lam = (jnp.exp(jnp.sum(lp[0:1, :] * lp[1:2, :], axis=-1, keepdims=True))
           - jnp.exp(jnp.sum(lp[2:3, :] * lp[3:4, :], axis=-1, keepdims=True)) + lam_init)
    o = a[0:tq, :] - lam * a[tq:2 * tq, :]
    ms = jnp.mean(o * o, axis=-1, keepdims=True)
    o = o * lax.rsqrt(ms + EPS) * gain_ref[...] * (1.0 - lam_init)
    o_ref[...] = o.astype(BF16)


def _diff_attention(q, kv_list, lam_params, out_gain, lam_init):
    bsz, n, width = q.shape
    tq = min(DIFF_TQ, n)
    lens = tuple(k.shape[1] for k, _ in kv_list)
    in_specs = [pl.BlockSpec((None, tq, LANES), lambda b, h, i: (b, i, h))]
    args = [q]
    for (k, v), length in zip(kv_list, lens):
        spec = pl.BlockSpec((None, length, LANES), lambda b, h, i: (b, 0, h))
        in_specs += [spec, spec]
        args += [k, v]
    in_specs += [pl.BlockSpec(lam_params.shape, lambda b, h, i: (0, 0)),
                 pl.BlockSpec((1, LANES), lambda b, h, i: (0, 0))]
    args += [lam_params.astype(F32), out_gain.reshape(1, LANES).astype(F32)]
    return pl.pallas_call(
        functools.partial(_diff_attn_kernel, n_src=len(kv_list), lens=lens, lam_init=lam_init, tk=FLASH_TK),
        grid=(bsz, DIFF_HEADS, n // tq),
        in_specs=in_specs,
        out_specs=pl.BlockSpec((None, tq, LANES), lambda b, h, i: (b, i, h)),
        out_shape=jax.ShapeDtypeStruct((bsz, n, width), BF16),
        compiler_params=_cparams(("parallel", "parallel", "parallel")),
        name="diff_attn",
    )(*args)


def _gqa_kernel(q_ref, kc_ref, vc_ref, kx_ref, vx_ref, o_ref, *, lens, tk):
    tq = q_ref.shape[0]
    sources = [(kc_ref, vc_ref, lens[0]), (kx_ref, vx_ref, lens[1])]
    lanes = _lane_ids((tq, LANES))
    for kv in range(GQA_KV_HEADS):
        mine = (lanes // HEAD_DIM) == kv
        rows = []
        for half in range(2):
            hh = q_ref[:, (2 * kv + half) * LANES:(2 * kv + half + 1) * LANES]
            sw = pltpu.roll(hh.astype(F32), HEAD_DIM, 1).astype(BF16)
            zero = jnp.zeros_like(hh)
            a_here, b_here = (hh, sw) if kv == 0 else (sw, hh)
            rows += [jnp.where(mine, a_here, zero), jnp.where(mine, b_here, zero)]
        qs = jnp.concatenate(rows, axis=0)
        o = _flash(qs, sources, tk)
        for half in range(2):
            oa = o[(2 * half) * tq:(2 * half + 1) * tq, :]
            ob = o[(2 * half + 1) * tq:(2 * half + 2) * tq, :]
            oa_sw = pltpu.roll(oa, HEAD_DIM, 1)
            ob_sw = pltpu.roll(ob, HEAD_DIM, 1)
            if kv == 0:
                res = jnp.where(lanes < HEAD_DIM, oa, ob_sw)
            else:
                res = jnp.where(lanes < HEAD_DIM, oa_sw, ob)
            o_ref[:, (2 * kv + half) * LANES:(2 * kv + half + 1) * LANES] = res.astype(BF16)


def _gqa_attention(q, kv_c, kv_x):
    bsz, n, width = q.shape
    tq = min(GQA_TQ, n)
    lens = (kv_c.shape[1], kv_x.shape[1])

    def kspec(length, col):
        return pl.BlockSpec((None, length, LANES), lambda b, i, col=col: (b, 0, col))

    return pl.pallas_call(
        functools.partial(_gqa_kernel, lens=lens, tk=FLASH_TK),
        grid=(bsz, n // tq),
        in_specs=[pl.BlockSpec((None, tq, width), lambda b, i: (b, i, 0)),
                  kspec(lens[0], 0), kspec(lens[0], 1), kspec(lens[1], 0), kspec(lens[1], 1)],
        out_specs=pl.BlockSpec((None, tq, width), lambda b, i: (b, i, 0)),
        out_shape=jax.ShapeDtypeStruct((bsz, n, width), BF16),
        compiler_params=_cparams(("parallel", "parallel")),
        name="gqa_attn",
    )(q, kv_c, kv_c, kv_x, kv_x)


def _gla_chunk(q, k, g, v, st, tri, reverse):
    c, s16 = HGRN_CHUNK, HGRN_SUB
    cum = jnp.dot(tri, g, precision=HIGHEST, preferred_element_type=F32)
    tot = cum[0:1, :] if reverse else cum[c - 1:c, :]
    vb = v.astype(BF16)
    out_state = _dot_nt((q * jnp.exp(cum)).astype(BF16), st.astype(BF16))
    new_st = st * jnp.exp(tot) + _dot_tn(vb, (k * jnp.exp(tot - cum)).astype(BF16))
    ridx = lax.broadcasted_iota(I32, (c, 1), 0)
    sidx = lax.broadcasted_iota(I32, (s16, 1), 0)
    blocks = [None] * (c // s16)
    for i in range(c // s16):
        r0 = c - s16 * (i + 1) if reverse else s16 * i
        q_i, k_i, v_i, cum_i = q[r0:r0 + s16], k[r0:r0 + s16], v[r0:r0 + s16], cum[r0:r0 + s16]
        acc = jnp.zeros((s16, v.shape[1]), F32)
        if i > 0:
            if reverse:
                bnd = cum[r0 + s16:r0 + s16 + 1, :]
                prev = ridx >= r0 + s16
            else:
                bnd = cum[r0 - 1:r0, :]
                prev = ridx < r0
            qt = (q_i * jnp.exp(cum_i - bnd)).astype(BF16)
            kt = (k * jnp.exp(jnp.where(prev, bnd - cum, -jnp.inf))).astype(BF16)
            acc = acc + _dot(_dot_nt(qt, kt).astype(BF16), vb)
        for s in range(s16):
            valid = (sidx <= s) if reverse else (sidx >= s)
            d = jnp.where(valid, cum_i - cum_i[s:s + 1, :], -jnp.inf)
            a = jnp.sum(q_i * k_i[s:s + 1, :] * jnp.exp(d), axis=-1, keepdims=True)
            acc = acc + a * v_i[s:s + 1, :]
        blocks[c // s16 - 1 - i if reverse else i] = acc
    return out_state + jnp.concatenate(blocks, axis=0), new_st


def _hgrn_kernel(qx_ref, kfx_ref, gfx_ref, kbx_ref, gbx_ref, vx_ref, sgx_ref,
                 qc_ref, kfc_ref, gfc_ref, kbc_ref, gbc_ref, vc_ref, sgc_ref,
                 gain_ref, lt_ref, ut_ref, ox_ref, oc_ref, fx_ref, bx_ref, fc_ref, bc_ref):
    c = HGRN_CHUNK
    lt, ut = lt_ref[...], ut_ref[...]

    def sweep(q_ref, kf_ref, gf_ref, kb_ref, gb_ref, v_ref, f_ref, b_ref, states):
        nchunks = q_ref.shape[0] // c

        def body(i, carry):
            st_f, st_b = carry
            rf = pl.multiple_of(i * c, c)
            rb = pl.multiple_of((nchunks - 1 - i) * c, c)
            of, st_f = _gla_chunk(q_ref[pl.ds(rf, c), :].astype(F32), kf_ref[pl.ds(rf, c), :].astype(F32),
                                  gf_ref[pl.ds(rf, c), :], v_ref[pl.ds(rf, c), :].astype(F32), st_f, lt, False)
            ob, st_b = _gla_chunk(q_ref[pl.ds(rb, c), :].astype(F32), kb_ref[pl.ds(rb, c), :].astype(F32),
                                  gb_ref[pl.ds(rb, c), :], v_ref[pl.ds(rb, c), :].astype(F32), st_b, ut, True)
            f_ref[pl.ds(rf, c), :] = of
            b_ref[pl.ds(rb, c), :] = ob
            return st_f, st_b

        return lax.fori_loop(0, nchunks, body, states)

    dk = qx_ref.shape[1]
    zero = jnp.zeros((vx_ref.shape[1], dk), F32)
    states = sweep(qc_ref, kfc_ref, gfc_ref, kbc_ref, gbc_ref, vc_ref, fc_ref, bc_ref, (zero, zero))
    sweep(qx_ref, kfx_ref, gfx_ref, kbx_ref, gbx_ref, vx_ref, fx_ref, bx_ref, states)

    def finish(f_ref, b_ref, sg_ref, o_ref):
        rows = f_ref.shape[0]
        tile = min(rows, 512)

        def body(i, _):
            r = pl.multiple_of(i * tile, tile)
            o = f_ref[pl.ds(r, tile), :] + b_ref[pl.ds(r, tile), :]
            ms = jnp.mean(o * o, axis=-1, keepdims=True)
            o = o * lax.rsqrt(ms + EPS) * gain_ref[...]
            o_ref[pl.ds(r, tile), :] = (o * sg_ref[pl.ds(r, tile), :].astype(F32)).astype(BF16)
            return 0

        lax.fori_loop(0, rows // tile, body, 0)

    finish(fx_ref, bx_ref, sgx_ref, ox_ref)
    finish(fc_ref, bc_ref, sgc_ref, oc_ref)


def _hgrn(px, pc, out_gain):
    bsz, n, width = px[0].shape
    m = pc[0].shape[1]
    c = HGRN_CHUNK
    r = jnp.arange(c)
    lt = (r[:, None] >= r[None, :]).astype(F32)
    ut = (r[:, None] <= r[None, :]).astype(F32)

    def spec(length):
        return pl.BlockSpec((None, length, LANES), lambda b, h: (b, 0, h))

    const = lambda shape: pl.BlockSpec(shape, lambda b, h: (0, 0))
    return pl.pallas_call(
        _hgrn_kernel,
        grid=(bsz, HGRN_HEADS),
        in_specs=[spec(n)] * 7 + [spec(m)] * 7 + [const((1, LANES)), const((c, c)), const((c, c))],
        out_specs=[spec(n), spec(m)],
        out_shape=[jax.ShapeDtypeStruct((bsz, n, width), BF16), jax.ShapeDtypeStruct((bsz, m, width), BF16)],
        scratch_shapes=[pltpu.VMEM((n, LANES), F32), pltpu.VMEM((n, LANES), F32),
                        pltpu.VMEM((m, LANES), F32), pltpu.VMEM((m, LANES), F32)],
        compiler_params=_cparams(("parallel", "parallel")),
        name="hgrn2",
    )(*px, *pc, out_gain.reshape(1, LANES).astype(F32), lt, ut)


def _na_bias_tables(rpb, rows):
    qrows, band = NA_QROWS, NA_BAND
    a = jnp.arange(qrows)[:, None, None, None]
    qc = jnp.arange(GRID_W)[None, :, None, None]
    i = jnp.arange(band)[None, None, :, None]
    kc = jnp.arange(GRID_W)[None, None, None, :]
    cstart = jnp.clip(qc - NA_COLS // 2, 0, GRID_W - NA_COLS)
    col_ok = (kc >= cstart) & (kc < cstart + NA_COLS)
    dc = jnp.clip(kc - qc + NA_COLS - 1, 0, 2 * NA_COLS - 2)
    tabs = []
    for r0 in (0, qrows, rows - qrows):
        rs = min(max(r0 - NA_ROWS // 2, 0), rows - band)
        qr = r0 + a
        kr = rs + i
        rstart = jnp.clip(qr - NA_ROWS // 2, 0, rows - NA_ROWS)
        row_ok = (kr >= rstart) & (kr < rstart + NA_ROWS)
        dr = jnp.clip(kr - qr + NA_ROWS - 1, 0, 2 * NA_ROWS - 2)
        bias = rpb.astype(F32)[:, dr, dc]
        bias = jnp.where((row_ok & col_ok)[None], bias, NEG_BIG)
        tabs.append(bias.reshape(rpb.shape[0], qrows * GRID_W, band * GRID_W))
    return jnp.stack(tabs)


def _na_kernel(q_ref, k_ref, v_ref, kc_ref, vc_ref, bias_ref, o_ref, *, rows):
    j = pl.program_id(2)
    tq = q_ref.shape[0]
    band = NA_BAND * GRID_W
    rs = jnp.clip(j * NA_QROWS - NA_ROWS // 2, 0, rows - NA_BAND)
    start = pl.multiple_of(rs * GRID_W, NA_ROWS // 2 * GRID_W)
    q = q_ref[...]
    lo = _lane_ids(q.shape) < HEAD_DIM
    zero = jnp.zeros_like(q)
    qs = jnp.concatenate([jnp.where(lo, q, zero), jnp.where(lo, zero, q)], axis=0)
    kb = k_ref[pl.ds(start, band), :]
    vb = v_ref[pl.ds(start, band), :]
    s_win = _dot_nt(qs, kb) + jnp.concatenate([bias_ref[0], bias_ref[1]], axis=0)
    s_ctx = _dot_nt(qs, kc_ref[...])
    m = jnp.maximum(jnp.max(s_win, axis=-1, keepdims=True), jnp.max(s_ctx, axis=-1, keepdims=True))
    p_win = jnp.exp(s_win - m)
    p_ctx = jnp.exp(s_ctx - m)
    l = jnp.sum(p_win, axis=-1, keepdims=True) + jnp.sum(p_ctx, axis=-1, keepdims=True)
    o = (_dot(p_ctx.astype(BF16), vc_ref[...]) + _dot(p_win.astype(BF16), vb)) / l
    o_ref[...] = jnp.where(lo, o[0:tq, :], o[tq:2 * tq, :]).astype(BF16)


def _na_attention(q, k, v, kc, vc, rpb):
    bsz, n, width = q.shape
    rows = n // GRID_W
    tq = NA_QROWS * GRID_W
    nt = n // tq
    bias = _na_bias_tables(rpb, rows)
    m = kc.shape[1]

    def cls(j):
        return jnp.where(j == 0, 0, jnp.where(j == nt - 1, 2, 1))

    full = lambda length: pl.BlockSpec((None, length, LANES), lambda b, h, j: (b, 0, h))
    return pl.pallas_call(
        functools.partial(_na_kernel, rows=rows),
        grid=(bsz, NA_HEADS // 2, nt),
        in_specs=[pl.BlockSpec((None, tq, LANES), lambda b, h, j: (b, j, h)),
                  full(n), full(n), full(m), full(m),
                  pl.BlockSpec((None, 2, tq, NA_BAND * GRID_W), lambda b, h, j: (cls(j), h, 0, 0))],
        out_specs=pl.BlockSpec((None, tq, LANES), lambda b, h, j: (b, j, h)),
        out_shape=jax.ShapeDtypeStruct((bsz, n, width), BF16),
        compiler_params=_cparams(("parallel", "parallel", "parallel")),
        name="na_attn",
    )(q, k, v, kc, vc, bias)


def _out_proj_kernel(a_ref, b_ref, w_ref, x_ref, gate_ref, gain_ref, shift_ref, scale_ref, r_ref,
                     x1_ref, tok_ref, logit_ref):
    half = a_ref.shape[1]
    y = _dot(a_ref[...], w_ref[0:half, :]) + _dot(b_ref[...], w_ref[half:2 * half, :])
    x1 = x_ref[...] + gate_ref[...] * y
    x1_ref[...] = x1
    h = _norm_mod(x1, gain_ref[...], shift_ref[...], scale_ref[...])
    logit_ref[...] = jnp.dot(h, r_ref[...], precision=HIGHEST, preferred_element_type=F32)
    tok_ref[...] = _pack_pair(h[:, 0:HALF], h[:, HALF:2 * HALF])


def _out_proj(a, b, w, x, gate, gain, shift, scale, router_pad):
    bsz, n, d = x.shape
    tm = min(PROJ_ROWS, n)
    return pl.pallas_call(
        _out_proj_kernel,
        grid=(bsz, n // tm),
        in_specs=[_row_spec(tm, a.shape[2]), _row_spec(tm, b.shape[2]), _const_spec(w.shape), _row_spec(tm, d),
                  _bcast_spec(d), _const_spec((1, d)), _bcast_spec(d), _bcast_spec(d), _const_spec(router_pad.shape)],
        out_specs=[_row_spec(tm, d), _row_spec(tm, HALF), _row_spec(tm, LANES)],
        out_shape=[jax.ShapeDtypeStruct((bsz, n, d), F32), jax.ShapeDtypeStruct((bsz, n, HALF), I32),
                   jax.ShapeDtypeStruct((bsz, n, LANES), F32)],
        compiler_params=_cparams(("parallel", "parallel")),
        name="out_proj",
    )(a, b, w, x, gate, gain.reshape(1, d), shift, scale, router_pad)


def _route_kernel(logit_ref, bias_ref, tri_ref, idx_ref, w_ref, pos_ref, cnt_ref, masked_ref, carry_ref):
    step = pl.program_id(0)

    @pl.when(step == 0)
    def _():
        carry_ref[...] = jnp.zeros_like(carry_ref)

    tr = logit_ref.shape[0]
    scores = jax.nn.sigmoid(logit_ref[...].T[0:N_EXPERTS, :])
    biased = scores + bias_ref[...]
    gsz = EXPERTS_PER_GROUP
    sub = lax.broadcasted_iota(I32, (gsz, tr), 0).astype(F32)
    gscore = []
    for g in range(N_GROUPS):
        bg = biased[g * gsz:(g + 1) * gsz, :]
        m1 = jnp.max(bg, axis=0, keepdims=True)
        i1 = jnp.min(jnp.where(bg == m1, sub, float(gsz)), axis=0, keepdims=True)
        m2 = jnp.max(jnp.where(sub == i1, -jnp.inf, bg), axis=0, keepdims=True)
        gscore.append(m1 + m2)
    for g in range(N_GROUPS):
        beaten = jnp.zeros((1, tr), F32)
        for o in range(N_GROUPS):
            if o == g:
                continue
            wins = (gscore[o] >= gscore[g]) if o < g else (gscore[o] > gscore[g])
            beaten = beaten + jnp.where(wins, 1.0, 0.0)
        keep = beaten < float(TOPK_GROUPS)
        masked_ref[g * gsz:(g + 1) * gsz, :] = jnp.where(keep, biased[g * gsz:(g + 1) * gsz, :], -jnp.inf)
    cur = masked_ref[...]
    eid = lax.broadcasted_iota(I32, (N_EXPERTS, tr), 0).astype(F32)
    sel = jnp.zeros((N_EXPERTS, tr), F32)
    picks, weights = [], []
    for _ in range(TOP_K):
        m = jnp.max(cur, axis=0, keepdims=True)
        ik = jnp.min(jnp.where(cur == m, eid, float(N_EXPERTS)), axis=0, keepdims=True)
        hit = eid == ik
        weights.append(jnp.sum(jnp.where(hit, scores, 0.0), axis=0, keepdims=True))
        sel = sel + jnp.where(hit, 1.0, 0.0)
        cur = jnp.where(hit, -jnp.inf, cur)
        picks.append(ik)
    wsum = weights[0]
    for wk in weights[1:]:
        wsum = wsum + wk
    before = _dot(sel.astype(BF16), tri_ref[...]) + carry_ref[:, 0:1]
    for kk in range(TOP_K):
        idx_ref[kk:kk + 1, :] = picks[kk].astype(I32)
        w_ref[kk:kk + 1, :] = weights[kk] / wsum * ROUTED_SCALE
        pos_ref[kk:kk + 1, :] = jnp.sum(jnp.where(eid == picks[kk], before, 0.0), axis=0, keepdims=True).astype(I32)
    carry_ref[...] = carry_ref[...] + jnp.sum(sel, axis=1, keepdims=True)
    cnt_ref[...] = carry_ref[...].astype(I32)


def _route(logits, router_bias):
    t = logits.shape[0]
    tr = ROUTE_ROWS if t % ROUTE_ROWS == 0 else LANES
    r = jnp.arange(tr)
    tri = (r[:, None] < r[None, :]).astype(BF16)
    kt_spec = pl.BlockSpec((TOP_K, tr), lambda i: (0, i))
    return pl.pallas_call(
        _route_kernel,
        grid=(t // tr,),
        in_specs=[pl.BlockSpec((tr, LANES), lambda i: (i, 0)),
                  pl.BlockSpec((N_EXPERTS, 1), lambda i: (0, 0)),
                  pl.BlockSpec((tr, tr), lambda i: (0, 0))],
        out_specs=[kt_spec, kt_spec, kt_spec, pl.BlockSpec((N_EXPERTS, LANES), lambda i: (0, 0))],
        out_shape=[jax.ShapeDtypeStruct((TOP_K, t), I32), jax.ShapeDtypeStruct((TOP_K, t), F32),
                   jax.ShapeDtypeStruct((TOP_K, t), I32), jax.ShapeDtypeStruct((N_EXPERTS, LANES), I32)],
        scratch_shapes=[pltpu.VMEM((N_EXPERTS, tr), F32), pltpu.VMEM((N_EXPERTS, LANES), F32)],
        compiler_params=_cparams(("arbitrary",)),
        name="moe_route",
    )(logits, router_bias.astype(F32).reshape(N_EXPERTS, 1), tri)


def _row_copy(src_ref, dst_ref, sem):
    return pltpu.make_async_copy(src_ref, dst_ref, sem)


def _dispatch_kernel(dest_ref, tok_ref, rows_in_ref, rows_ref, sem):
    del rows_in_ref
    tt = tok_ref.shape[0]

    def body(t, _):
        for kk in range(TOP_K):
            d = dest_ref[t * TOP_K + kk]
            _row_copy(tok_ref.at[pl.ds(t, 1)], rows_ref.at[pl.ds(d, 1)], sem).start()
        return 0

    lax.fori_loop(0, tt, body, 0)
    for _ in range(TOP_K):
        _row_copy(tok_ref, rows_ref.at[pl.ds(0, tt)], sem).wait()


def _dispatch(dest_flat, tok, rows_buf):
    t = tok.shape[0]
    tt = min(MOE_TOK, t)
    return pl.pallas_call(
        _dispatch_kernel,
        grid=(t // tt,),
        in_specs=[pl.BlockSpec((tt * TOP_K,), lambda i: (i,), memory_space=pltpu.SMEM),
                  pl.BlockSpec((tt, HALF), lambda i: (i, 0)),
                  pl.BlockSpec(memory_space=pl.ANY)],
        out_specs=pl.BlockSpec(memory_space=pl.ANY),
        out_shape=jax.ShapeDtypeStruct(rows_buf.shape, rows_buf.dtype),
        scratch_shapes=[pltpu.SemaphoreType.DMA(())],
        input_output_aliases={2: 0},
        compiler_params=_cparams(("arbitrary",)),
        name="moe_dispatch",
    )(dest_flat, tok, rows_buf)


def _expert_kernel(be_ref, nused_ref, x_ref, wgu_ref, wd_ref, y_ref):
    @pl.when(pl.program_id(0) < nused_ref[0])
    def _():
        lo, hi = _unpack_pair(x_ref[...])
        gu = _dot(lo.astype(BF16), wgu_ref[0:HALF, :]) + _dot(hi.astype(BF16), wgu_ref[HALF:2 * HALF, :])
        h = (_silu(gu[:, 0:EXPERT_DIM]) * gu[:, EXPERT_DIM:2 * EXPERT_DIM]).astype(BF16)
        y = _dot(h, wd_ref[...])
        y_ref[...] = _pack_pair(y[:, 0:HALF], y[:, HALF:2 * HALF])

    @pl.when(pl.program_id(0) >= nused_ref[0])
    def _():
        y_ref[...] = jnp.zeros_like(y_ref)


def _experts(block_expert, nused, rows, wgu, wd):
    n_rows = rows.shape[0]
    nb = n_rows // MOE_BLOCK

    def row_map(i, be, nu):
        return (jnp.minimum(i, nu[0] - 1), 0)

    def w_map(i, be, nu):
        return (be[jnp.minimum(i, nu[0] - 1)], 0, 0)

    grid_spec = pltpu.PrefetchScalarGridSpec(
        num_scalar_prefetch=2,
        grid=(nb,),
        in_specs=[pl.BlockSpec((MOE_BLOCK, HALF), row_map),
                  pl.BlockSpec((None, D_MODEL, 2 * EXPERT_DIM), w_map),
                  pl.BlockSpec((None, EXPERT_DIM, D_MODEL), w_map)],
        out_specs=pl.BlockSpec((MOE_BLOCK, HALF), lambda i, be, nu: (i, 0)),
    )
    return pl.pallas_call(
        _expert_kernel,
        grid_spec=grid_spec,
        out_shape=jax.ShapeDtypeStruct((n_rows, HALF), I32),
        compiler_params=_cparams(("arbitrary",)),
        name="moe_experts",
    )(block_expert, nused, rows, wgu, wd)


def _combine_kernel(dest_ref, x1_ref, tok_ref, w_ref, gate_ref, wgu_ref, wd_ref, y_ref, o_ref, buf_ref, sem):
    tt = tok_ref.shape[0]

    def body(t, _):
        for kk in range(TOP_K):
            d = dest_ref[t * TOP_K + kk]
            _row_copy(y_ref.at[pl.ds(d, 1)], buf_ref.at[kk, pl.ds(t, 1)], sem).start()
        return 0

    lax.fori_loop(0, tt, body, 0)
    lo, hi = _unpack_pair(tok_ref[...])
    gu = _dot(lo.astype(BF16), wgu_ref[0:HALF, :]) + _dot(hi.astype(BF16), wgu_ref[HALF:2 * HALF, :])
    h = (_silu(gu[:, 0:EXPERT_DIM]) * gu[:, EXPERT_DIM:2 * EXPERT_DIM]).astype(BF16)
    shared = _dot(h, wd_ref[...])
    for kk in range(TOP_K):
        _row_copy(y_ref.at[pl.ds(0, tt)], buf_ref.at[kk], sem).wait()
    acc_lo = shared[:, 0:HALF]
    acc_hi = shared[:, HALF:2 * HALF]
    for kk in range(TOP_K):
        ylo, yhi = _unpack_pair(buf_ref[kk])
        wk = w_ref[:, kk:kk + 1]
        acc_lo = acc_lo + wk * ylo
        acc_hi = acc_hi + wk * yhi
    gate = gate_ref[...]
    o_ref[:, 0:HALF] = x1_ref[:, 0:HALF] + gate[:, 0:HALF] * acc_lo
    o_ref[:, HALF:2 * HALF] = x1_ref[:, HALF:2 * HALF] + gate[:, HALF:2 * HALF] * acc_hi


def _combine(dest_flat, x1, tok, w_tok, gate, wgu, wd, y_rows, tokens_per_gate):
    t, d = x1.shape
    tt = min(MOE_TOK, t)
    per = tokens_per_gate // tt
    return pl.pallas_call(
        _combine_kernel,
        grid=(t // tt,),
        in_specs=[pl.BlockSpec((tt * TOP_K,), lambda i: (i,), memory_space=pltpu.SMEM),
                  pl.BlockSpec((tt, d), lambda i: (i, 0)),
                  pl.BlockSpec((tt, HALF), lambda i: (i, 0)),
                  pl.BlockSpec((tt, TOP_K), lambda i: (i, 0)),
                  pl.BlockSpec((None, 1, d), lambda i: (i // per, 0, 0)),
                  pl.BlockSpec(wgu.shape, lambda i: (0, 0)),
                  pl.BlockSpec(wd.shape, lambda i: (0, 0)),
                  pl.BlockSpec(memory_space=pl.ANY)],
        out_specs=pl.BlockSpec((tt, d), lambda i: (i, 0)),
        out_shape=jax.ShapeDtypeStruct((t, d), F32),
        scratch_shapes=[pltpu.VMEM((TOP_K, tt, HALF), I32), pltpu.SemaphoreType.DMA(())],
        compiler_params=_cparams(("arbitrary",)),
        name="moe_combine",
    )(dest_flat, x1, tok, w_tok, gate, wgu, wd, y_rows)


def _moe(parts, router_bias, w_gate, w_up, w_down, ws_gate, ws_up, ws_down):
    logits = jnp.concatenate([p[2].reshape(-1, LANES) for p in parts], axis=0)
    t = logits.shape[0]
    idx, w, pos, cnt = _route(logits, router_bias)
    counts = cnt[:, 0]
    padded = (counts + MOE_BLOCK - 1) // MOE_BLOCK * MOE_BLOCK
    pad_end = jnp.cumsum(padded)
    pad_start = pad_end - padded
    dest = (pad_start[idx] + pos).T.reshape(-1)
    w_tok = w.T
    n_assign = t * TOP_K
    n_rows = ((n_assign + MOE_BLOCK - 1) // MOE_BLOCK + N_EXPERTS) * MOE_BLOCK
    nb = n_rows // MOE_BLOCK
    block_start = jnp.arange(nb, dtype=I32) * MOE_BLOCK
    block_expert = jnp.minimum(jnp.sum(block_start[:, None] >= pad_end[None, :], axis=1), N_EXPERTS - 1).astype(I32)
    nused = (pad_end[-1] // MOE_BLOCK).astype(I32).reshape(1)
    rows = jnp.zeros((n_rows, HALF), I32)
    off = 0
    for x1, tok, _, _, _ in parts:
        cnt_tok = tok.shape[0] * tok.shape[1]
        rows = _dispatch(dest[off * TOP_K:(off + cnt_tok) * TOP_K], tok.reshape(cnt_tok, HALF), rows)
        off += cnt_tok
    wgu = jnp.concatenate([w_gate, w_up], axis=-1).astype(BF16)
    y_rows = _experts(block_expert, nused, rows, wgu, w_down.astype(BF16))
    wsgu = jnp.concatenate([ws_gate, ws_up], axis=-1).astype(BF16)
    wsd = ws_down.astype(BF16)
    outs = []
    off = 0
    for x1, tok, _, gate, per in parts:
        cnt_tok = tok.shape[0] * tok.shape[1]
        o = _combine(dest[off * TOP_K:(off + cnt_tok) * TOP_K], x1.reshape(cnt_tok, D_MODEL),
                     tok.reshape(cnt_tok, HALF), w_tok[off:off + cnt_tok], gate, wsgu, wsd, y_rows, per)
        outs.append(o.reshape(x1.shape))
        off += cnt_tok
    return outs


def kernel(x, c, ctx, c_ctx, ada_w, ada_b, norm_mix, norm_ffn, ev_w_in, ev_w_out, diff_qk_gain, diff_lambda,
           diff_out_gain, hgrn_lb, hgrn_out_gain, od_w_in, od_w_out, gqa_qk_gain, na_qk_gain, na_rpb, moe_router,
           moe_router_bias, moe_w_gate, moe_w_up, moe_w_down, shared_w_gate, shared_w_up, shared_w_down):
    bsz, n, d = x.shape
    m = ctx.shape[1]
    depth = ada_w.shape[0]
    cond_rows = -(-(bsz + 1) // 8) * 8
    cond = jnp.zeros((cond_rows, d), F32).at[0:bsz].set(c).at[bsz].set(c_ctx)
    mods = _ada_mod(cond, ada_w, ada_b)

    xc = ctx
    for layer in range(depth):
        need_ctx = layer < depth - 1
        j = layer // 2
        mod = mods[layer].reshape(cond_rows, 6, d)
        mx = [mod[0:bsz, i][:, None, :] for i in range(6)]
        mc = [jnp.broadcast_to(mod[bsz:bsz + 1, i][:, None, :], (bsz, 1, d)) for i in range(6)]
        if layer % 2 == 0:
            w_in = ev_w_in[j].astype(BF16)
            px = _even_proj(x, mx[0], mx[1], norm_mix[layer], w_in, diff_qk_gain[j], hgrn_lb[0], hgrn_lb[1], j, True)
            pc = _even_proj(xc, mc[0], mc[1], norm_mix[layer], w_in, diff_qk_gain[j], hgrn_lb[0], hgrn_lb[1], j, False)
            lam_init = 0.8 - 0.6 * math.exp(-0.3 * layer)
            a_x = _diff_attention(px[0], [(pc[1], pc[2]), (px[1], px[2])], diff_lambda[j], diff_out_gain[j], lam_init)
            a_c = _diff_attention(pc[0], [(pc[1], pc[2])], diff_lambda[j], diff_out_gain[j], lam_init)
            b_x, b_c = _hgrn(px[3:], pc[3:], hgrn_out_gain[j])
            w_out = ev_w_out[j].astype(BF16)
        else:
            w_in = od_w_in[j].astype(BF16)
            px = _odd_proj(x, mx[0], mx[1], norm_mix[layer], w_in, gqa_qk_gain[j], na_qk_gain[j], True)
            pc = _odd_proj(xc, mc[0], mc[1], norm_mix[layer], w_in, gqa_qk_gain[j], na_qk_gain[j], False)
            a_x = _gqa_attention(px[0], pc[1], px[1])
            b_x = _na_attention(px[2], px[3], px[4], pc[3], pc[4], na_rpb[j])
            a_c = b_c = None
            w_out = od_w_out[j].astype(BF16)
        router_pad = jnp.zeros((d, LANES), F32).at[:, 0:N_EXPERTS].set(moe_router[layer].astype(F32))
        x1, tok_x, logit_x = _out_proj(a_x, b_x, w_out, x, mx[2], norm_ffn[layer], mx[3], mx[4], router_pad)
        parts = [(x1, tok_x, logit_x, mx[5], n)]
        if need_ctx:
            xc1, tok_c, logit_c = _out_proj(a_c, b_c, w_out, xc, mc[2], norm_ffn[layer], mc[3], mc[4], router_pad)
            parts.append((xc1, tok_c, logit_c, mc[5][0:1], bsz * m))
        outs = _moe(parts, moe_router_bias[layer], moe_w_gate[layer], moe_w_up[layer], moe_w_down[layer],
                    shared_w_gate[layer], shared_w_up[layer], shared_w_down[layer])
        x = outs[0]
        if need_ctx:
            xc = outs[1]
    return x
```

```python
import functools
import math

import jax
import jax.numpy as jnp
import numpy as np
from jax import lax
from jax.experimental import pallas as pl
from jax.experimental.pallas import tpu as pltpu

F32 = jnp.float32
BF16 = jnp.bfloat16
I32 = jnp.int32
HIGHEST = lax.Precision.HIGHEST

D_MODEL = 1024
GRID_W = 64
HEAD_DIM = 64
ATTN_SCALE = HEAD_DIM ** -0.5
ROPE_THETA = 10000.0
EPS = 1e-6
DIFF_HEADS = D_MODEL // 256
HGRN_HEADS = D_MODEL // 256
HGRN_CHUNK = 64
HGRN_SUB = 16
GQA_HEADS = D_MODEL // 128
GQA_KV_HEADS = GQA_HEADS // 4
NA_HEADS = D_MODEL // 128
NA_ROWS = 8
NA_COLS = 16
N_EXPERTS = 64
N_GROUPS = 8
EXPERTS_PER_GROUP = N_EXPERTS // N_GROUPS
TOPK_GROUPS = 4
TOP_K = 8
EXPERT_DIM = D_MODEL // 4
ROUTED_SCALE = 2.5
HALF = D_MODEL // 2

LANES = 128
VMEM_LIMIT_BYTES = 56 * 1024 * 1024
PROJ_ROWS = 512
FLASH_TK = 512
DIFF_TQ = 512
GQA_TQ = 256
NA_QROWS = 8
NA_BAND = 16
ROUTE_ROWS = 512
MOE_BLOCK = 256
MOE_TOK = 256
NEG_BIG = -1e30


def _cparams(sem):
    return pltpu.CompilerParams(dimension_semantics=sem, vmem_limit_bytes=VMEM_LIMIT_BYTES)


def _silu(x):
    return x * jax.nn.sigmoid(x)


def _dot(a, b):
    return jnp.dot(a, b, preferred_element_type=F32)


def _dot_nt(a, b):
    return lax.dot_general(a, b, (((1,), (1,)), ((), ())), preferred_element_type=F32)


def _dot_tn(a, b):
    return lax.dot_general(a, b, (((0,), (0,)), ((), ())), preferred_element_type=F32)


def _pack_pair(lo, hi):
    lo_bits = lax.bitcast_convert_type(lo.astype(BF16).astype(F32), I32)
    hi_bits = lax.bitcast_convert_type(hi.astype(BF16).astype(F32), I32)
    return lax.shift_right_logical(lo_bits, 16) | (hi_bits & jnp.int32(-65536))


def _unpack_pair(w):
    lo = lax.bitcast_convert_type(lax.shift_left(w, 16), F32)
    hi = lax.bitcast_convert_type(w & jnp.int32(-65536), F32)
    return lo, hi


def _ada_kernel(cond_ref, w_ref, b_ref, o_ref):
    s = _silu(cond_ref[...])
    o_ref[...] = jnp.dot(s, w_ref[...], precision=HIGHEST, preferred_element_type=F32) + b_ref[...]


def _ada_mod(cond, ada_w, ada_b):
    depth = ada_w.shape[0]
    rows = cond.shape[0]
    nblk = ada_w.shape[2] // D_MODEL
    return pl.pallas_call(
        _ada_kernel,
        grid=(depth, nblk),
        in_specs=[
            pl.BlockSpec((rows, D_MODEL), lambda l, j: (0, 0)),
            pl.BlockSpec((None, D_MODEL, D_MODEL), lambda l, j: (l, 0, j)),
            pl.BlockSpec((None, 1, D_MODEL), lambda l, j: (l, 0, j)),
        ],
        out_specs=pl.BlockSpec((None, rows, D_MODEL), lambda l, j: (l, 0, j)),
        out_shape=jax.ShapeDtypeStruct((depth, rows, nblk * D_MODEL), F32),
        compiler_params=_cparams(("parallel", "parallel")),
        name="ada_mod",
    )(cond, ada_w, ada_b.reshape(depth, 1, -1))


def _norm_mod(x, gain, shift, scale):
    ms = jnp.mean(x * x, axis=-1, keepdims=True)
    h = x * lax.rsqrt(ms + EPS) * gain
    return h * (1.0 + scale) + shift


def _seg_rms(acc, gain, bd):
    sq = acc * acc
    hi = sq.astype(BF16)
    lo = (sq - hi.astype(F32)).astype(BF16)
    ms = _dot(hi, bd) + _dot(lo, bd)
    return acc * lax.rsqrt(ms + EPS) * gain


def _rope(y, c, sm, sp):
    return y * c + pltpu.roll(y, LANES - 16, 1) * sm + pltpu.roll(y, 16, 1) * sp


def _rope_tables(n):
    pos = jnp.arange(n, dtype=I32)
    row = (pos // GRID_W).astype(F32)
    col = (pos % GRID_W).astype(F32)
    axis_dim = HEAD_DIM // 2
    inv_freq = ROPE_THETA ** (-jnp.arange(0, axis_dim, 2, dtype=F32) / axis_dim)
    ang_row = row[:, None] * inv_freq
    ang_col = col[:, None] * inv_freq
    lane = jnp.arange(LANES)
    p = lane % axis_dim
    f = p % (axis_dim // 2)
    on_row = ((lane % HEAD_DIM) // axis_dim) == 0
    ang = jnp.where(on_row[None, :], ang_row[:, f], ang_col[:, f])
    c = jnp.cos(ang)
    s = jnp.sin(ang)
    first = (p < axis_dim // 2)[None, :]
    return c, jnp.where(first, -s, 0.0), jnp.where(first, 0.0, s)


def _seg_mean_matrix():
    r = jnp.arange(LANES)
    return jnp.where((r[:, None] // HEAD_DIM) == (r[None, :] // HEAD_DIM), 1.0 / HEAD_DIM, 0.0).astype(BF16)


def _even_proj_kernel(*refs, rope, layer_slot):
    if rope:
        (x_ref, shift_ref, scale_ref, gain_ref, w_ref, qkg_ref, lbf_ref, lbb_ref, bd_ref, rc_ref, rm_ref, rp_ref,
         dq_ref, dk_ref, dv_ref, hq_ref, kf_ref, gf_ref, kb_ref, gb_ref, hv_ref, hg_ref) = refs
        tables = (rc_ref[...], rm_ref[...], rp_ref[...])
    else:
        (x_ref, shift_ref, scale_ref, gain_ref, w_ref, qkg_ref, lbf_ref, lbb_ref, bd_ref,
         dq_ref, dk_ref, dv_ref, hq_ref, kf_ref, gf_ref, kb_ref, gb_ref, hv_ref, hg_ref) = refs
        tables = None
    hb = _norm_mod(x_ref[...], gain_ref[...], shift_ref[...], scale_ref[...]).astype(BF16)
    bd = bd_ref[...]
    width = 4 * LANES

    def proj(group):
        return _dot(hb, w_ref[:, group * width:(group + 1) * width])

    def qk(group, gain, out_ref, mult):
        acc = proj(group)
        for s in range(4):
            y = _seg_rms(acc[:, s * LANES:(s + 1) * LANES], gain, bd)
            if tables is not None:
                y = _rope(y, *tables)
            out_ref[:, s * LANES:(s + 1) * LANES] = (y * mult).astype(BF16)

    qk(0, qkg_ref[0:1, :], dq_ref, ATTN_SCALE)
    qk(1, qkg_ref[1:2, :], dk_ref, 1.0)
    dv_ref[...] = proj(2).astype(BF16)
    hq_ref[...] = _silu(proj(3)).astype(BF16)

    def forget(group, lb_ref, k_ref, g_ref):
        raw = lb_ref[...]
        e = jnp.exp(raw - jnp.max(raw, axis=0, keepdims=True))
        lb = jnp.sum(e[0:layer_slot + 1, :], axis=0, keepdims=True) / jnp.sum(e, axis=0, keepdims=True)
        f = lb + (1.0 - lb) * jax.nn.sigmoid(proj(group))
        k_ref[...] = (1.0 - f).astype(BF16)
        g_ref[...] = jnp.log(f)

    forget(4, lbf_ref, kf_ref, gf_ref)
    forget(5, lbb_ref, kb_ref, gb_ref)
    hv_ref[...] = proj(6).astype(BF16)
    hg_ref[...] = _silu(proj(7)).astype(BF16)


def _row_spec(tm, width):
    return pl.BlockSpec((None, tm, width), lambda b, i: (b, i, 0))


def _bcast_spec(width):
    return pl.BlockSpec((None, 1, width), lambda b, i: (b, 0, 0))


def _const_spec(shape):
    nd = len(shape)
    return pl.BlockSpec(shape, lambda b, i: (0,) * nd)


def _even_proj(x, shift, scale, gain, w, qk_gain, lb_fwd, lb_bwd, layer_slot, rope):
    bsz, n, d = x.shape
    tm = min(PROJ_ROWS, n)
    width = 4 * LANES
    qkg = jnp.tile(qk_gain.astype(F32), (1, 2))
    in_specs = [
        _row_spec(tm, d), _bcast_spec(d), _bcast_spec(d), _const_spec((1, d)), _const_spec(w.shape),
        _const_spec((2, LANES)), _const_spec(lb_fwd.shape), _const_spec(lb_bwd.shape), _const_spec((LANES, LANES)),
    ]
    args = [x, shift, scale, gain.reshape(1, d), w, qkg, lb_fwd, lb_bwd, _seg_mean_matrix()]
    if rope:
        tab_spec = pl.BlockSpec((tm, LANES), lambda b, i: (i, 0))
        in_specs += [tab_spec] * 3
        args += list(_rope_tables(n))
    out_dtypes = [BF16, BF16, BF16, BF16, BF16, F32, BF16, F32, BF16, BF16]
    return pl.pallas_call(
        functools.partial(_even_proj_kernel, rope=rope, layer_slot=layer_slot),
        grid=(bsz, n // tm),
        in_specs=in_specs,
        out_specs=[_row_spec(tm, width)] * len(out_dtypes),
        out_shape=[jax.ShapeDtypeStruct((bsz, n, width), dt) for dt in out_dtypes],
        compiler_params=_cparams(("parallel", "parallel")),
        name="even_proj_x" if rope else "even_proj_ctx",
    )(*args)


def _odd_proj_kernel(*refs, rope):
    if rope:
        (x_ref, shift_ref, scale_ref, gain_ref, w_ref, gqg_ref, nag_ref, bd_ref, rc_ref, rm_ref, rp_ref,
         gq_ref, gkv_ref, nq_ref, nk_ref, nv_ref) = refs
        tables = (rc_ref[...], rm_ref[...], rp_ref[...])
    else:
        (x_ref, shift_ref, scale_ref, gain_ref, w_ref, gqg_ref, nag_ref, bd_ref,
         gq_ref, gkv_ref, nq_ref, nk_ref, nv_ref) = refs
        tables = None
    hb = _norm_mod(x_ref[...], gain_ref[...], shift_ref[...], scale_ref[...]).astype(BF16)
    bd = bd_ref[...]

    def slab(acc, s, gain, use_rope, mult):
        y = _seg_rms(acc[:, s * LANES:(s + 1) * LANES], gain, bd)
        if use_rope and tables is not None:
            y = _rope(y, *tables)
        return (y * mult).astype(BF16)

    q_w = GQA_HEADS * HEAD_DIM
    acc = _dot(hb, w_ref[:, 0:q_w])
    for s in range(q_w // LANES):
        gq_ref[:, s * LANES:(s + 1) * LANES] = slab(acc, s, gqg_ref[0:1, :], True, ATTN_SCALE)
    acc = _dot(hb, w_ref[:, q_w:q_w + 2 * LANES])
    gkv_ref[:, 0:LANES] = slab(acc, 0, gqg_ref[1:2, :], True, 1.0)
    gkv_ref[:, LANES:2 * LANES] = acc[:, LANES:2 * LANES].astype(BF16)
    base = q_w + 2 * LANES
    na_w = NA_HEADS * HEAD_DIM
    acc = _dot(hb, w_ref[:, base:base + na_w])
    for s in range(na_w // LANES):
        nq_ref[:, s * LANES:(s + 1) * LANES] = slab(acc, s, nag_ref[0:1, :], False, ATTN_SCALE)
    acc = _dot(hb, w_ref[:, base + na_w:base + 2 * na_w])
    for s in range(na_w // LANES):
        nk_ref[:, s * LANES:(s + 1) * LANES] = slab(acc, s, nag_ref[1:2, :], False, 1.0)
    nv_ref[...] = _dot(hb, w_ref[:, base + 2 * na_w:base + 3 * na_w]).astype(BF16)


def _odd_proj(x, shift, scale, gain, w, gqa_gain, na_gain, rope):
    bsz, n, d = x.shape
    tm = min(PROJ_ROWS, n)
    gqg = jnp.tile(gqa_gain.astype(F32), (1, 2))
    nag = jnp.tile(na_gain.astype(F32), (1, 2))
    in_specs = [
        _row_spec(tm, d), _bcast_spec(d), _bcast_spec(d), _const_spec((1, d)), _const_spec(w.shape),
        _const_spec((2, LANES)), _const_spec((2, LANES)), _const_spec((LANES, LANES)),
    ]
    args = [x, shift, scale, gain.reshape(1, d), w, gqg, nag, _seg_mean_matrix()]
    if rope:
        tab_spec = pl.BlockSpec((tm, LANES), lambda b, i: (i, 0))
        in_specs += [tab_spec] * 3
        args += list(_rope_tables(n))
    widths = [GQA_HEADS * HEAD_DIM, 2 * LANES, NA_HEADS * HEAD_DIM, NA_HEADS * HEAD_DIM, NA_HEADS * HEAD_DIM]
    return pl.pallas_call(
        functools.partial(_odd_proj_kernel, rope=rope),
        grid=(bsz, n // tm),
        in_specs=in_specs,
        out_specs=[_row_spec(tm, wd) for wd in widths],
        out_shape=[jax.ShapeDtypeStruct((bsz, n, wd), BF16) for wd in widths],
        compiler_params=_cparams(("parallel", "parallel")),
        name="odd_proj_x" if rope else "odd_proj_ctx",
    )(*args)


def _flash(qs, sources, tk):
    rows = qs.shape[0]

    def step(k, v, carry):
        m, l, acc = carry
        s = _dot_nt(qs, k)
        m_new = jnp.maximum(m, jnp.max(s, axis=-1, keepdims=True))
        alpha = jnp.exp(m - m_new)
        p = jnp.exp(s - m_new)
        l = alpha * l + jnp.sum(p, axis=-1, keepdims=True)
        acc = alpha * acc + _dot(p.astype(BF16), v)
        return m_new, l, acc

    carry = (jnp.full((rows, 1), -jnp.inf, F32), jnp.zeros((rows, 1), F32), jnp.zeros((rows, LANES), F32))
    for k_ref, v_ref, length in sources:
        chunk = min(tk, length)
        steps = length // chunk
        if steps == 1:
            carry = step(k_ref[...], v_ref[...], carry)
        else:
            def body(i, c, k_ref=k_ref, v_ref=v_ref, chunk=chunk):
                off = pl.multiple_of(i * chunk, chunk)
                return step(k_ref[pl.ds(off, chunk), :], v_ref[pl.ds(off, chunk), :], c)
            carry = lax.fori_loop(0, steps, body, carry)
    m, l, acc = carry
    return acc / l


def _lane_ids(shape):
    return lax.broadcasted_iota(I32, shape, len(shape) - 1)


def _diff_attn_kernel(*refs, n_src, lens, lam_init, tk):
    q_ref = refs[0]
    kv_refs = refs[1:1 + 2 * n_src]
    lam_ref, gain_ref, o_ref = refs[1 + 2 * n_src:]
    q = q_ref[...]
    tq = q.shape[0]
    lo = _lane_ids(q.shape) < HEAD_DIM
    zero = jnp.zeros_like(q)
    qs = jnp.concatenate([jnp.where(lo, q, zero), jnp.where(lo, zero, q)], axis=0)
    sources = [(kv_refs[2 * i], kv_refs[2 * i + 1], lens[i]) for i in range(n_src)]
    a = _flash(qs, sources, tk)
    lp = lam_ref[...]
    lam = (jnp.exp(jnp.sum(lp[0:1, :] * lp[1:2, :], axis=-1, keepdims=True))
           - jnp.exp(jnp.sum(lp[2:3, :] * lp[3:4, :], axis=-1, keepdims=True)) + lam_init)
    o = a[0:tq, :] - lam * a[tq:2 * tq, :]
    ms = jnp.mean(o * o, axis=-1, keepdims=True)
    o = o * lax.rsqrt(ms + EPS) * gain_ref[...] * (1.0 - lam_init)
    o_ref[...] = o.astype(BF16)


def _diff_attention(q, kv_list, lam_params, out_gain, lam_init):
    bsz, n, width = q.shape
    tq = min(DIFF_TQ, n)
    lens = tuple(k.shape[1] for k, _ in kv_list)
    in_specs = [pl.BlockSpec((None, tq, LANES), lambda b, h, i: (b, i, h))]
    args = [q]
    for (k, v), length in zip(kv_list, lens):
        spec = pl.BlockSpec((None, length, LANES), lambda b, h, i: (b, 0, h))
        in_specs += [spec, spec]
        args += [k, v]
    in_specs += [pl.BlockSpec(lam_params.shape, lambda b, h, i: (0, 0)),
                 pl.BlockSpec((1, LANES), lambda b, h, i: (0, 0))]
    args += [lam_params.astype(F32), out_gain.reshape(1, LANES).astype(F32)]
    return pl.pallas_call(
        functools.partial(_diff_attn_kernel, n_src=len(kv_list), lens=lens, lam_init=lam_init, tk=FLASH_TK),
        grid=(bsz, DIFF_HEADS, n // tq),
        in_specs=in_specs,
        out_specs=pl.BlockSpec((None, tq, LANES), lambda b, h, i: (b, i, h)),
        out_shape=jax.ShapeDtypeStruct((bsz, n, width), BF16),
        compiler_params=_cparams(("parallel", "parallel", "parallel")),
        name="diff_attn",
    )(*args)


def _gqa_kernel(q_ref, kc_ref, vc_ref, kx_ref, vx_ref, o_ref, *, lens, tk):
    tq = q_ref.shape[0]
    sources = [(kc_ref, vc_ref, lens[0]), (kx_ref, vx_ref, lens[1])]
    lanes = _lane_ids((tq, LANES))
    for kv in range(GQA_KV_HEADS):
        mine = (lanes // HEAD_DIM) == kv
        rows = []
        for half in range(2):
            hh = q_ref[:, (2 * kv + half) * LANES:(2 * kv + half + 1) * LANES]
            sw = pltpu.roll(hh.astype(F32), HEAD_DIM, 1).astype(BF16)
            zero = jnp.zeros_like(hh)
            a_here, b_here = (hh, sw) if kv == 0 else (sw, hh)
            rows += [jnp.where(mine, a_here, zero), jnp.where(mine, b_here, zero)]
        qs = jnp.concatenate(rows, axis=0)
        o = _flash(qs, sources, tk)
        for half in range(2):
            oa = o[(2 * half) * tq:(2 * half + 1) * tq, :]
            ob = o[(2 * half + 1) * tq:(2 * half + 2) * tq, :]
            oa_sw = pltpu.roll(oa, HEAD_DIM, 1)
            ob_sw = pltpu.roll(ob, HEAD_DIM, 1)
            if kv == 0:
                res = jnp.where(lanes < HEAD_DIM, oa, ob_sw)
            else:
                res = jnp.where(lanes < HEAD_DIM, oa_sw, ob)
            o_ref[:, (2 * kv + half) * LANES:(2 * kv + half + 1) * LANES] = res.astype(BF16)


def _gqa_attention(q, kv_c, kv_x):
    bsz, n, width = q.shape
    tq = min(GQA_TQ, n)
    lens = (kv_c.shape[1], kv_x.shape[1])

    def kspec(length, col):
        return pl.BlockSpec((None, length, LANES), lambda b, i, col=col: (b, 0, col))

    return pl.pallas_call(
        functools.partial(_gqa_kernel, lens=lens, tk=FLASH_TK),
        grid=(bsz, n // tq),
        in_specs=[pl.BlockSpec((None, tq, width), lambda b, i: (b, i, 0)),
                  kspec(lens[0], 0), kspec(lens[0], 1), kspec(lens[1], 0), kspec(lens[1], 1)],
        out_specs=pl.BlockSpec((None, tq, width), lambda b, i: (b, i, 0)),
        out_shape=jax.ShapeDtypeStruct((bsz, n, width), BF16),
        compiler_params=_cparams(("parallel", "parallel")),
        name="gqa_attn",
    )(q, kv_c, kv_c, kv_x, kv_x)


def _gla_chunk(q, k, g, v, st, tri, reverse):
    c, s16 = HGRN_CHUNK, HGRN_SUB
    cum = jnp.dot(tri, g, precision=HIGHEST, preferred_element_type=F32)
    tot = cum[0:1, :] if reverse else cum[c - 1:c, :]
    vb = v.astype(BF16)
    out_state = _dot_nt((q * jnp.exp(cum)).astype(BF16), st.astype(BF16))
    new_st = st * jnp.exp(tot) + _dot_tn(vb, (k * jnp.exp(tot - cum)).astype(BF16))
    ridx = lax.broadcasted_iota(I32, (c, 1), 0)
    sidx = lax.broadcasted_iota(I32, (s16, 1), 0)
    blocks = [None] * (c // s16)
    for i in range(c // s16):
        r0 = c - s16 * (i + 1) if reverse else s16 * i
        q_i, k_i, v_i, cum_i = q[r0:r0 + s16], k[r0:r0 + s16], v[r0:r0 + s16], cum[r0:r0 + s16]
        acc = jnp.zeros((s16, v.shape[1]), F32)
        if i > 0:
            if reverse:
                bnd = cum[r0 + s16:r0 + s16 + 1, :]
                prev = ridx >= r0 + s16
            else:
                bnd = cum[r0 - 1:r0, :]
                prev = ridx < r0
            qt = (q_i * jnp.exp(cum_i - bnd)).astype(BF16)
            kt = (k * jnp.exp(jnp.where(prev, bnd - cum, -jnp.inf))).astype(BF16)
            acc = acc + _dot(_dot_nt(qt, kt).astype(BF16), vb)
        for s in range(s16):
            valid = (sidx <= s) if reverse else (sidx >= s)
            d = jnp.where(valid, cum_i - cum_i[s:s + 1, :], -jnp.inf)
            a = jnp.sum(q_i * k_i[s:s + 1, :] * jnp.exp(d), axis=-1, keepdims=True)
            acc = acc + a * v_i[s:s + 1, :]
        blocks[c // s16 - 1 - i if reverse else i] = acc
    return out_state + jnp.concatenate(blocks, axis=0), new_st


def _hgrn_kernel(qx_ref, kfx_ref, gfx_ref, kbx_ref, gbx_ref, vx_ref, sgx_ref,
                 qc_ref, kfc_ref, gfc_ref, kbc_ref, gbc_ref, vc_ref, sgc_ref,
                 gain_ref, lt_ref, ut_ref, ox_ref, oc_ref, fx_ref, bx_ref, fc_ref, bc_ref):
    c = HGRN_CHUNK
    lt, ut = lt_ref[...], ut_ref[...]

    def sweep(q_ref, kf_ref, gf_ref, kb_ref, gb_ref, v_ref, f_ref, b_ref, states):
        nchunks = q_ref.shape[0] // c

        def body(i, carry):
            st_f, st_b = carry
            rf = pl.multiple_of(i * c, c)
            rb = pl.multiple_of((nchunks - 1 - i) * c, c)
            of, st_f = _gla_chunk(q_ref[pl.ds(rf, c), :].astype(F32), kf_ref[pl.ds(rf, c), :].astype(F32),
                                  gf_ref[pl.ds(rf, c), :], v_ref[pl.ds(rf, c), :].astype(F32), st_f, lt, False)
            ob, st_b = _gla_chunk(q_ref[pl.ds(rb, c), :].astype(F32), kb_ref[pl.ds(rb, c), :].astype(F32),
                                  gb_ref[pl.ds(rb, c), :], v_ref[pl.ds(rb, c), :].astype(F32), st_b, ut, True)
            f_ref[pl.ds(rf, c), :] = of
            b_ref[pl.ds(rb, c), :] = ob
            return st_f, st_b

        return lax.fori_loop(0, nchunks, body, states)

    dk = qx_ref.shape[1]
    zero = jnp.zeros((vx_ref.shape[1], dk), F32)
    states = sweep(qc_ref, kfc_ref, gfc_ref, kbc_ref, gbc_ref, vc_ref, fc_ref, bc_ref, (zero, zero))
    sweep(qx_ref, kfx_ref, gfx_ref, kbx_ref, gbx_ref, vx_ref, fx_ref, bx_ref, states)

    def finish(f_ref, b_ref, sg_ref, o_ref):
        rows = f_ref.shape[0]
        tile = min(rows, 512)

        def body(i, _):
            r = pl.multiple_of(i * tile, tile)
            o = f_ref[pl.ds(r, tile), :] + b_ref[pl.ds(r, tile), :]
            ms = jnp.mean(o * o, axis=-1, keepdims=True)
            o = o * lax.rsqrt(ms + EPS) * gain_ref[...]
            o_ref[pl.ds(r, tile), :] = (o * sg_ref[pl.ds(r, tile), :].astype(F32)).astype(BF16)
            return 0

        lax.fori_loop(0, rows // tile, body, 0)

    finish(fx_ref, bx_ref, sgx_ref, ox_ref)
    finish(fc_ref, bc_ref, sgc_ref, oc_ref)


def _hgrn(px, pc, out_gain):
    bsz, n, width = px[0].shape
    m = pc[0].shape[1]
    c = HGRN_CHUNK
    r = jnp.arange(c)
    lt = (r[:, None] >= r[None, :]).astype(F32)
    ut = (r[:, None] <= r[None, :]).astype(F32)

    def spec(length):
        return pl.BlockSpec((None, length, LANES), lambda b, h: (b, 0, h))

    const = lambda shape: pl.BlockSpec(shape, lambda b, h: (0, 0))
    return pl.pallas_call(
        _hgrn_kernel,
        grid=(bsz, HGRN_HEADS),
        in_specs=[spec(n)] * 7 + [spec(m)] * 7 + [const((1, LANES)), const((c, c)), const((c, c))],
        out_specs=[spec(n), spec(m)],
        out_shape=[jax.ShapeDtypeStruct((bsz, n, width), BF16), jax.ShapeDtypeStruct((bsz, m, width), BF16)],
        scratch_shapes=[pltpu.VMEM((n, LANES), F32), pltpu.VMEM((n, LANES), F32),
                        pltpu.VMEM((m, LANES), F32), pltpu.VMEM((m, LANES), F32)],
        compiler_params=_cparams(("parallel", "parallel")),
        name="hgrn2",
    )(*px, *pc, out_gain.reshape(1, LANES).astype(F32), lt, ut)


def _na_bias_tables(rpb, rows):
    qrows, band, heads = NA_QROWS, NA_BAND, rpb.shape[0]
    pad = GRID_W - NA_COLS
    wide = jnp.pad(rpb.astype(F32), ((0, 0), (0, 0), (pad, pad)))
    toeplitz = jnp.stack([wide[:, :, GRID_W - 1 - qc:2 * GRID_W - 1 - qc] for qc in range(GRID_W)], axis=2)
    qc = np.arange(GRID_W)[:, None]
    kc = np.arange(GRID_W)[None, :]
    cstart = np.clip(qc - NA_COLS // 2, 0, GRID_W - NA_COLS)
    col_ok = (kc >= cstart) & (kc < cstart + NA_COLS)
    toeplitz = jnp.where(col_ok[None, None], toeplitz, NEG_BIG)
    tabs = []
    for r0 in (0, qrows, rows - qrows):
        rs = min(max(r0 - NA_ROWS // 2, 0), rows - band)
        qr = r0 + np.arange(qrows)[:, None]
        kr = rs + np.arange(band)[None, :]
        rstart = np.clip(qr - NA_ROWS // 2, 0, rows - NA_ROWS)
        row_ok = (kr >= rstart) & (kr < rstart + NA_ROWS)
        dr = np.clip(kr - qr + NA_ROWS - 1, 0, 2 * NA_ROWS - 2)
        tiles = jnp.take(toeplitz, jnp.asarray(dr.reshape(-1), I32), axis=1)
        tiles = jnp.where(row_ok.reshape(1, -1, 1, 1), tiles, NEG_BIG)
        tiles = tiles.reshape(heads, qrows, band, GRID_W, GRID_W).transpose(0, 1, 3, 2, 4)
        tabs.append(tiles.reshape(heads, qrows * GRID_W, band * GRID_W))
    return jnp.stack(tabs)


def _na_kernel(q_ref, k_ref, v_ref, kc_ref, vc_ref, bias_ref, o_ref, *, rows):
    j = pl.program_id(2)
    tq = q_ref.shape[0]
    band = NA_BAND * GRID_W
    rs = jnp.clip(j * NA_QROWS - NA_ROWS // 2, 0, rows - NA_BAND)
    start = pl.multiple_of(rs * GRID_W, NA_ROWS // 2 * GRID_W)
    q = q_ref[...]
    lo = _lane_ids(q.shape) < HEAD_DIM
    zero = jnp.zeros_like(q)
    qs = jnp.concatenate([jnp.where(lo, q, zero), jnp.where(lo, zero, q)], axis=0)
    kb = k_ref[pl.ds(start, band), :]
    vb = v_ref[pl.ds(start, band), :]
    s_win = _dot_nt(qs, kb) + jnp.concatenate([bias_ref[0], bias_ref[1]], axis=0)
    s_ctx = _dot_nt(qs, kc_ref[...])
    m = jnp.maximum(jnp.max(s_win, axis=-1, keepdims=True), jnp.max(s_ctx, axis=-1, keepdims=True))
    p_win = jnp.exp(s_win - m)
    p_ctx = jnp.exp(s_ctx - m)
    l = jnp.sum(p_win, axis=-1, keepdims=True) + jnp.sum(p_ctx, axis=-1, keepdims=True)
    o = (_dot(p_ctx.astype(BF16), vc_ref[...]) + _dot(p_win.astype(BF16), vb)) / l
    o_ref[...] = jnp.where(lo, o[0:tq, :], o[tq:2 * tq, :]).astype(BF16)


def _na_attention(q, k, v, kc, vc, rpb):
    bsz, n, width = q.shape
    rows = n // GRID_W
    tq = NA_QROWS * GRID_W
    nt = n // tq
    bias = _na_bias_tables(rpb, rows)
    m = kc.shape[1]

    def cls(j):
        return jnp.where(j == 0, 0, jnp.where(j == nt - 1, 2, 1))

    full = lambda length: pl.BlockSpec((None, length, LANES), lambda b, h, j: (b, 0, h))
    return pl.pallas_call(
        functools.partial(_na_kernel, rows=rows),
        grid=(bsz, NA_HEADS // 2, nt),
        in_specs=[pl.BlockSpec((None, tq, LANES), lambda b, h, j: (b, j, h)),
                  full(n), full(n), full(m), full(m),
                  pl.BlockSpec((None, 2, tq, NA_BAND * GRID_W), lambda b, h, j: (cls(j), h, 0, 0))],
        out_specs=pl.BlockSpec((None, tq, LANES), lambda b, h, j: (b, j, h)),
        out_shape=jax.ShapeDtypeStruct((bsz, n, width), BF16),
        compiler_params=_cparams(("parallel", "parallel", "parallel")),
        name="na_attn",
    )(q, k, v, kc, vc, bias)


def _out_proj_kernel(a_ref, b_ref, w_ref, x_ref, gate_ref, gain_ref, shift_ref, scale_ref, r_ref,
                     x1_ref, tok_ref, logit_ref):
    half = a_ref.shape[1]
    y = _dot(a_ref[...], w_ref[0:half, :]) + _dot(b_ref[...], w_ref[half:2 * half, :])
    x1 = x_ref[...] + gate_ref[...] * y
    x1_ref[...] = x1
    h = _norm_mod(x1, gain_ref[...], shift_ref[...], scale_ref[...])
    logit_ref[...] = jnp.dot(h, r_ref[...], precision=HIGHEST, preferred_element_type=F32)
    tok_ref[...] = _pack_pair(h[:, 0:HALF], h[:, HALF:2 * HALF])


def _out_proj(a, b, w, x, gate, gain, shift, scale, router_pad):
    bsz, n, d = x.shape
    tm = min(PROJ_ROWS, n)
    return pl.pallas_call(
        _out_proj_kernel,
        grid=(bsz, n // tm),
        in_specs=[_row_spec(tm, a.shape[2]), _row_spec(tm, b.shape[2]), _const_spec(w.shape), _row_spec(tm, d),
                  _bcast_spec(d), _const_spec((1, d)), _bcast_spec(d), _bcast_spec(d), _const_spec(router_pad.shape)],
        out_specs=[_row_spec(tm, d), _row_spec(tm, HALF), _row_spec(tm, LANES)],
        out_shape=[jax.ShapeDtypeStruct((bsz, n, d), F32), jax.ShapeDtypeStruct((bsz, n, HALF), I32),
                   jax.ShapeDtypeStruct((bsz, n, LANES), F32)],
        compiler_params=_cparams(("parallel", "parallel")),
        name="out_proj",
    )(a, b, w, x, gate, gain.reshape(1, d), shift, scale, router_pad)


def _route_kernel(logit_ref, bias_ref, tri_ref, idx_ref, w_ref, pos_ref, cnt_ref, masked_ref, carry_ref):
    step = pl.program_id(0)

    @pl.when(step == 0)
    def _():
        carry_ref[...] = jnp.zeros_like(carry_ref)

    tr = logit_ref.shape[0]
    scores = jax.nn.sigmoid(logit_ref[...].T[0:N_EXPERTS, :])
    biased = scores + bias_ref[...]
    gsz = EXPERTS_PER_GROUP
    sub = lax.broadcasted_iota(I32, (gsz, tr), 0).astype(F32)
    gscore = []
    for g in range(N_GROUPS):
        bg = biased[g * gsz:(g + 1) * gsz, :]
        m1 = jnp.max(bg, axis=0, keepdims=True)
        i1 = jnp.min(jnp.where(bg == m1, sub, float(gsz)), axis=0, keepdims=True)
        m2 = jnp.max(jnp.where(sub == i1, -jnp.inf, bg), axis=0, keepdims=True)
        gscore.append(m1 + m2)
    for g in range(N_GROUPS):
        beaten = jnp.zeros((1, tr), F32)
        for o in range(N_GROUPS):
            if o == g:
                continue
            wins = (gscore[o] >= gscore[g]) if o < g else (gscore[o] > gscore[g])
            beaten = beaten + jnp.where(wins, 1.0, 0.0)
        keep = beaten < float(TOPK_GROUPS)
        masked_ref[g * gsz:(g + 1) * gsz, :] = jnp.where(keep, biased[g * gsz:(g + 1) * gsz, :], -jnp.inf)
    cur = masked_ref[...]
    eid = lax.broadcasted_iota(I32, (N_EXPERTS, tr), 0).astype(F32)
    sel = jnp.zeros((N_EXPERTS, tr), F32)
    picks, weights = [], []
    for _ in range(TOP_K):
        m = jnp.max(cur, axis=0, keepdims=True)
        ik = jnp.min(jnp.where(cur == m, eid, float(N_EXPERTS)), axis=0, keepdims=True)
        hit = eid == ik
        weights.append(jnp.sum(jnp.where(hit, scores, 0.0), axis=0, keepdims=True))
        sel = sel + jnp.where(hit, 1.0, 0.0)
        cur = jnp.where(hit, -jnp.inf, cur)
        picks.append(ik)
    wsum = weights[0]
    for wk in weights[1:]:
        wsum = wsum + wk
    before = _dot(sel.astype(BF16), tri_ref[...]) + carry_ref[:, 0:1]
    for kk in range(TOP_K):
        idx_ref[kk:kk + 1, :] = picks[kk].astype(I32)
        w_ref[kk:kk + 1, :] = weights[kk] / wsum * ROUTED_SCALE
        pos_ref[kk:kk + 1, :] = jnp.sum(jnp.where(eid == picks[kk], before, 0.0), axis=0, keepdims=True).astype(I32)
    carry_ref[...] = carry_ref[...] + jnp.sum(sel, axis=1, keepdims=True)
    cnt_ref[...] = carry_ref[...].astype(I32)


def _route(logits, router_bias):
    t = logits.shape[0]
    tr = ROUTE_ROWS if t % ROUTE_ROWS == 0 else LANES
    r = jnp.arange(tr)
    tri = (r[:, None] < r[None, :]).astype(BF16)
    kt_spec = pl.BlockSpec((TOP_K, tr), lambda i: (0, i))
    return pl.pallas_call(
        _route_kernel,
        grid=(t // tr,),
        in_specs=[pl.BlockSpec((tr, LANES), lambda i: (i, 0)),
                  pl.BlockSpec((N_EXPERTS, 1), lambda i: (0, 0)),
                  pl.BlockSpec((tr, tr), lambda i: (0, 0))],
        out_specs=[kt_spec, kt_spec, kt_spec, pl.BlockSpec((N_EXPERTS, LANES), lambda i: (0, 0))],
        out_shape=[jax.ShapeDtypeStruct((TOP_K, t), I32), jax.ShapeDtypeStruct((TOP_K, t), F32),
                   jax.ShapeDtypeStruct((TOP_K, t), I32), jax.ShapeDtypeStruct((N_EXPERTS, LANES), I32)],
        scratch_shapes=[pltpu.VMEM((N_EXPERTS, tr), F32), pltpu.VMEM((N_EXPERTS, LANES), F32)],
        compiler_params=_cparams(("arbitrary",)),
        name="moe_route",
    )(logits, router_bias.astype(F32).reshape(N_EXPERTS, 1), tri)


def _row_copy(src_ref, dst_ref, sem):
    return pltpu.make_async_copy(src_ref, dst_ref, sem)


def _dispatch_kernel(dest_ref, tok_ref, rows_in_ref, rows_ref, sem):
    del rows_in_ref
    tt = tok_ref.shape[0]

    def body(t, _):
        for kk in range(TOP_K):
            d = dest_ref[t * TOP_K + kk]
            _row_copy(tok_ref.at[pl.ds(t, 1)], rows_ref.at[pl.ds(d, 1)], sem).start(priority=kk % 2)
        return 0

    lax.fori_loop(0, tt, body, 0)
    for _ in range(TOP_K):
        _row_copy(tok_ref, rows_ref.at[pl.ds(0, tt)], sem).wait()


def _dispatch(dest_flat, tok, rows_buf):
    t = tok.shape[0]
    tt = min(MOE_TOK, t)
    return pl.pallas_call(
        _dispatch_kernel,
        grid=(t // tt,),
        in_specs=[pl.BlockSpec((tt * TOP_K,), lambda i: (i,), memory_space=pltpu.SMEM),
                  pl.BlockSpec((tt, HALF), lambda i: (i, 0)),
                  pl.BlockSpec(memory_space=pl.ANY)],
        out_specs=pl.BlockSpec(memory_space=pl.ANY),
        out_shape=jax.ShapeDtypeStruct(rows_buf.shape, rows_buf.dtype),
        scratch_shapes=[pltpu.SemaphoreType.DMA(())],
        input_output_aliases={2: 0},
        compiler_params=_cparams(("arbitrary",)),
        name="moe_dispatch",
    )(dest_flat, tok, rows_buf)


def _expert_kernel(be_ref, nused_ref, x_ref, wgu_ref, wd_ref, y_ref):
    @pl.when(pl.program_id(0) < nused_ref[0])
    def _():
        lo, hi = _unpack_pair(x_ref[...])
        gu = _dot(lo.astype(BF16), wgu_ref[0:HALF, :]) + _dot(hi.astype(BF16), wgu_ref[HALF:2 * HALF, :])
        h = (_silu(gu[:, 0:EXPERT_DIM]) * gu[:, EXPERT_DIM:2 * EXPERT_DIM]).astype(BF16)
        y = _dot(h, wd_ref[...])
        y_ref[...] = _pack_pair(y[:, 0:HALF], y[:, HALF:2 * HALF])

    @pl.when(pl.program_id(0) >= nused_ref[0])
    def _():
        y_ref[...] = jnp.zeros_like(y_ref)


def _experts(block_expert, nused, rows, wgu, wd):
    n_rows = rows.shape[0]
    nb = n_rows // MOE_BLOCK

    def row_map(i, be, nu):
        return (jnp.minimum(i, nu[0] - 1), 0)

    def w_map(i, be, nu):
        return (be[jnp.minimum(i, nu[0] - 1)], 0, 0)

    grid_spec = pltpu.PrefetchScalarGridSpec(
        num_scalar_prefetch=2,
        grid=(nb,),
        in_specs=[pl.BlockSpec((MOE_BLOCK, HALF), row_map),
                  pl.BlockSpec((None, D_MODEL, 2 * EXPERT_DIM), w_map),
                  pl.BlockSpec((None, EXPERT_DIM, D_MODEL), w_map)],
        out_specs=pl.BlockSpec((MOE_BLOCK, HALF), lambda i, be, nu: (i, 0)),
    )
    return pl.pallas_call(
        _expert_kernel,
        grid_spec=grid_spec,
        out_shape=jax.ShapeDtypeStruct((n_rows, HALF), I32),
        compiler_params=_cparams(("arbitrary",)),
        name="moe_experts",
    )(block_expert, nused, rows, wgu, wd)


def _combine_kernel(dest_ref, x1_ref, tok_ref, w_ref, gate_ref, wgu_ref, wd_ref, y_ref, o_ref, buf_ref, sem):
    tt = tok_ref.shape[0]

    def body(t, _):
        for kk in range(TOP_K):
            d = dest_ref[t * TOP_K + kk]
            _row_copy(y_ref.at[pl.ds(d, 1)], buf_ref.at[kk, pl.ds(t, 1)], sem).start(priority=kk % 2)
        return 0

    lax.fori_loop(0, tt, body, 0)
    lo, hi = _unpack_pair(tok_ref[...])
    gu = _dot(lo.astype(BF16), wgu_ref[0:HALF, :]) + _dot(hi.astype(BF16), wgu_ref[HALF:2 * HALF, :])
    h = (_silu(gu[:, 0:EXPERT_DIM]) * gu[:, EXPERT_DIM:2 * EXPERT_DIM]).astype(BF16)
    shared = _dot(h, wd_ref[...])
    for kk in range(TOP_K):
        _row_copy(y_ref.at[pl.ds(0, tt)], buf_ref.at[kk], sem).wait()
    acc_lo = shared[:, 0:HALF]
    acc_hi = shared[:, HALF:2 * HALF]
    for kk in range(TOP_K):
        ylo, yhi = _unpack_pair(buf_ref[kk])
        wk = w_ref[:, kk:kk + 1]
        acc_lo = acc_lo + wk * ylo
        acc_hi = acc_hi + wk * yhi
    gate = gate_ref[...]
    o_ref[:, 0:HALF] = x1_ref[:, 0:HALF] + gate[:, 0:HALF] * acc_lo
    o_ref[:, HALF:2 * HALF] = x1_ref[:, HALF:2 * HALF] + gate[:, HALF:2 * HALF] * acc_hi


def _combine(dest_flat, x1, tok, w_tok, gate, wgu, wd, y_rows, tokens_per_gate):
    t, d = x1.shape
    tt = min(MOE_TOK, t)
    per = tokens_per_gate // tt
    return pl.pallas_call(
        _combine_kernel,
        grid=(t // tt,),
        in_specs=[pl.BlockSpec((tt * TOP_K,), lambda i: (i,), memory_space=pltpu.SMEM),
                  pl.BlockSpec((tt, d), lambda i: (i, 0)),
                  pl.BlockSpec((tt, HALF), lambda i: (i, 0)),
                  pl.BlockSpec((tt, TOP_K), lambda i: (i, 0)),
                  pl.BlockSpec((None, 1, d), lambda i: (i // per, 0, 0)),
                  pl.BlockSpec(wgu.shape, lambda i: (0, 0)),
                  pl.BlockSpec(wd.shape, lambda i: (0, 0)),
                  pl.BlockSpec(memory_space=pl.ANY)],
        out_specs=pl.BlockSpec((tt, d), lambda i: (i, 0)),
        out_shape=jax.ShapeDtypeStruct((t, d), F32),
        scratch_shapes=[pltpu.VMEM((TOP_K, tt, HALF), I32), pltpu.SemaphoreType.DMA(())],
        compiler_params=_cparams(("arbitrary",)),
        name="moe_combine",
    )(dest_flat, x1, tok, w_tok, gate, wgu, wd, y_rows)


def _moe(parts, router_bias, w_gate, w_up, w_down, ws_gate, ws_up, ws_down):
    logits = jnp.concatenate([p[2].reshape(-1, LANES) for p in parts], axis=0)
    t = logits.shape[0]
    idx, w, pos, cnt = _route(logits, router_bias)
    counts = cnt[:, 0]
    padded = (counts + MOE_BLOCK - 1) // MOE_BLOCK * MOE_BLOCK
    pad_end = jnp.cumsum(padded)
    pad_start = pad_end - padded
    experts = jnp.arange(N_EXPERTS, dtype=I32)[:, None, None]
    seg_start = jnp.sum(jnp.where(idx[None] == experts, pad_start[:, None, None].astype(I32), 0), axis=0)
    dest = (seg_start + pos).T.reshape(-1)
    w_tok = w.T
    n_assign = t * TOP_K
    n_rows = ((n_assign + MOE_BLOCK - 1) // MOE_BLOCK + N_EXPERTS) * MOE_BLOCK
    nb = n_rows // MOE_BLOCK
    block_start = jnp.arange(nb, dtype=I32) * MOE_BLOCK
    block_expert = jnp.minimum(jnp.sum(block_start[:, None] >= pad_end[None, :], axis=1), N_EXPERTS - 1).astype(I32)
    nused = (pad_end[-1] // MOE_BLOCK).astype(I32).reshape(1)
    rows = jnp.zeros((n_rows, HALF), I32)
    off = 0
    for x1, tok, _, _, _ in parts:
        cnt_tok = tok.shape[0] * tok.shape[1]
        rows = _dispatch(dest[off * TOP_K:(off + cnt_tok) * TOP_K], tok.reshape(cnt_tok, HALF), rows)
        off += cnt_tok
    wgu = jnp.concatenate([w_gate, w_up], axis=-1).astype(BF16)
    y_rows = _experts(block_expert, nused, rows, wgu, w_down.astype(BF16))
    wsgu = jnp.concatenate([ws_gate, ws_up], axis=-1).astype(BF16)
    wsd = ws_down.astype(BF16)
    outs = []
    off = 0
    for x1, tok, _, gate, per in parts:
        cnt_tok = tok.shape[0] * tok.shape[1]
        o = _combine(dest[off * TOP_K:(off + cnt_tok) * TOP_K], x1.reshape(cnt_tok, D_MODEL),
                     tok.reshape(cnt_tok, HALF), w_tok[off:off + cnt_tok], gate, wsgu, wsd, y_rows, per)
        outs.append(o.reshape(x1.shape))
        off += cnt_tok
    return outs


def kernel(x, c, ctx, c_ctx, ada_w, ada_b, norm_mix, norm_ffn, ev_w_in, ev_w_out, diff_qk_gain, diff_lambda,
           diff_out_gain, hgrn_lb, hgrn_out_gain, od_w_in, od_w_out, gqa_qk_gain, na_qk_gain, na_rpb, moe_router,
           moe_router_bias, moe_w_gate, moe_w_up, moe_w_down, shared_w_gate, shared_w_up, shared_w_down):
    bsz, n, d = x.shape
    m = ctx.shape[1]
    depth = ada_w.shape[0]
    cond_rows = -(-(bsz + 1) // 8) * 8
    cond = jnp.zeros((cond_rows, d), F32).at[0:bsz].set(c).at[bsz].set(c_ctx)
    mods = _ada_mod(cond, ada_w, ada_b)

    xc = ctx
    for layer in range(depth):
        need_ctx = layer < depth - 1
        j = layer // 2
        mod = mods[layer].reshape(cond_rows, 6, d)
        mx = [mod[0:bsz, i][:, None, :] for i in range(6)]
        mc = [jnp.broadcast_to(mod[bsz:bsz + 1, i][:, None, :], (bsz, 1, d)) for i in range(6)]
        if layer % 2 == 0:
            w_in = ev_w_in[j].astype(BF16)
            px = _even_proj(x, mx[0], mx[1], norm_mix[layer], w_in, diff_qk_gain[j], hgrn_lb[0], hgrn_lb[1], j, True)
            pc = _even_proj(xc, mc[0], mc[1], norm_mix[layer], w_in, diff_qk_gain[j], hgrn_lb[0], hgrn_lb[1], j, False)
            lam_init = 0.8 - 0.6 * math.exp(-0.3 * layer)
            a_x = _diff_attention(px[0], [(pc[1], pc[2]), (px[1], px[2])], diff_lambda[j], diff_out_gain[j], lam_init)
            a_c = _diff_attention(pc[0], [(pc[1], pc[2])], diff_lambda[j], diff_out_gain[j], lam_init)
            b_x, b_c = _hgrn(px[3:], pc[3:], hgrn_out_gain[j])
            w_out = ev_w_out[j].astype(BF16)
        else:
            w_in = od_w_in[j].astype(BF16)
            px = _odd_proj(x, mx[0], mx[1], norm_mix[layer], w_in, gqa_qk_gain[j], na_qk_gain[j], True)
            pc = _odd_proj(xc, mc[0], mc[1], norm_mix[layer], w_in, gqa_qk_gain[j], na_qk_gain[j], False)
            a_x = _gqa_attention(px[0], pc[1], px[1])
            b_x = _na_attention(px[2], px[3], px[4], pc[3], pc[4], na_rpb[j])
            a_c = b_c = None
            w_out = od_w_out[j].astype(BF16)
        router_pad = jnp.zeros((d, LANES), F32).at[:, 0:N_EXPERTS].set(moe_router[layer].astype(F32))
        x1, tok_x, logit_x = _out_proj(a_x, b_x, w_out, x, mx[2], norm_ffn[layer], mx[3], mx[4], router_pad)
        parts = [(x1, tok_x, logit_x, mx[5], n)]
        if need_ctx:
            xc1, tok_c, logit_c = _out_proj(a_c, b_c, w_out, xc, mc[2], norm_ffn[layer], mc[3], mc[4], router_pad)
            parts.append((xc1, tok_c, logit_c, mc[5][0:1], bsz * m))
        outs = _moe(parts, moe_router_bias[layer], moe_w_gate[layer], moe_w_up[layer], moe_w_down[layer],
                    shared_w_gate[layer], shared_w_up[layer], shared_w_down[layer])
        x = outs[0]
        if need_ctx:
            xc = outs[1]
    return x
```

```python
import functools
import math

import jax
import jax.numpy as jnp
import numpy as np
from jax import lax
from jax.experimental import pallas as pl
from jax.experimental.pallas import tpu as pltpu

F32 = jnp.float32
BF16 = jnp.bfloat16
I32 = jnp.int32
HIGHEST = lax.Precision.HIGHEST

D_MODEL = 1024
GRID_W = 64
HEAD_DIM = 64
ATTN_SCALE = HEAD_DIM ** -0.5
ROPE_THETA = 10000.0
EPS = 1e-6
DIFF_HEADS = D_MODEL // 256
HGRN_HEADS = D_MODEL // 256
HGRN_CHUNK = 64
HGRN_SUB = 16
GQA_HEADS = D_MODEL // 128
GQA_KV_HEADS = GQA_HEADS // 4
NA_HEADS = D_MODEL // 128
NA_ROWS = 8
NA_COLS = 16
N_EXPERTS = 64
N_GROUPS = 8
EXPERTS_PER_GROUP = N_EXPERTS // N_GROUPS
TOPK_GROUPS = 4
TOP_K = 8
EXPERT_DIM = D_MODEL // 4
ROUTED_SCALE = 2.5
HALF = D_MODEL // 2

LANES = 128
VMEM_LIMIT_BYTES = 56 * 1024 * 1024
PROJ_ROWS = 512
FLASH_TK = 512
FLASH_SB = 64
DIFF_TQ = 512
GQA_TQ = 256
NA_QROWS = 8
NA_BAND = 16
ROUTE_ROWS = 512
HGRN_HEADS_PER_STEP = 2
MOE_BLOCK = 512
MOE_TOK = 256
NEG_BIG = -1e30


def _cparams(sem):
    return pltpu.CompilerParams(dimension_semantics=sem, vmem_limit_bytes=VMEM_LIMIT_BYTES)


def _silu(x):
    return x * jax.nn.sigmoid(x)


def _dot(a, b):
    return jnp.dot(a, b, preferred_element_type=F32)


def _dot_nt(a, b):
    return lax.dot_general(a, b, (((1,), (1,)), ((), ())), preferred_element_type=F32)


def _dot_tn(a, b):
    return lax.dot_general(a, b, (((0,), (0,)), ((), ())), preferred_element_type=F32)


def _pack_pair(lo, hi):
    lo_bits = lax.bitcast_convert_type(lo.astype(BF16).astype(F32), I32)
    hi_bits = lax.bitcast_convert_type(hi.astype(BF16).astype(F32), I32)
    return lax.shift_right_logical(lo_bits, 16) | (hi_bits & jnp.int32(-65536))


def _unpack_pair(w):
    lo = lax.bitcast_convert_type(lax.shift_left(w, 16), F32)
    hi = lax.bitcast_convert_type(w & jnp.int32(-65536), F32)
    return lo, hi


def _ada_kernel(cond_ref, w_ref, b_ref, o_ref):
    s = _silu(cond_ref[...])
    o_ref[...] = jnp.dot(s, w_ref[...], precision=HIGHEST, preferred_element_type=F32) + b_ref[...]


def _ada_mod(cond, ada_w, ada_b):
    depth = ada_w.shape[0]
    rows = cond.shape[0]
    nblk = ada_w.shape[2] // D_MODEL
    return pl.pallas_call(
        _ada_kernel,
        grid=(depth, nblk),
        in_specs=[
            pl.BlockSpec((rows, D_MODEL), lambda l, j: (0, 0)),
            pl.BlockSpec((None, D_MODEL, D_MODEL), lambda l, j: (l, 0, j)),
            pl.BlockSpec((None, 1, D_MODEL), lambda l, j: (l, 0, j)),
        ],
        out_specs=pl.BlockSpec((None, rows, D_MODEL), lambda l, j: (l, 0, j)),
        out_shape=jax.ShapeDtypeStruct((depth, rows, nblk * D_MODEL), F32),
        compiler_params=_cparams(("parallel", "parallel")),
        name="ada_mod",
    )(cond, ada_w, ada_b.reshape(depth, 1, -1))


def _norm_mod(x, gain, shift, scale):
    ms = jnp.mean(x * x, axis=-1, keepdims=True)
    h = x * lax.rsqrt(ms + EPS) * gain
    return h * (1.0 + scale) + shift


def _seg_rms(acc, gain, bd):
    sq = acc * acc
    hi = sq.astype(BF16)
    lo = (sq - hi.astype(F32)).astype(BF16)
    ms = _dot(hi, bd) + _dot(lo, bd)
    return acc * lax.rsqrt(ms + EPS) * gain


def _rope(y, c, sm, sp):
    return y * c + pltpu.roll(y, LANES - 16, 1) * sm + pltpu.roll(y, 16, 1) * sp


def _rope_tables(n):
    pos = jnp.arange(n, dtype=I32)
    row = (pos // GRID_W).astype(F32)
    col = (pos % GRID_W).astype(F32)
    axis_dim = HEAD_DIM // 2
    inv_freq = ROPE_THETA ** (-jnp.arange(0, axis_dim, 2, dtype=F32) / axis_dim)
    ang_row = row[:, None] * inv_freq
    ang_col = col[:, None] * inv_freq
    lane = jnp.arange(LANES)
    p = lane % axis_dim
    f = p % (axis_dim // 2)
    on_row = ((lane % HEAD_DIM) // axis_dim) == 0
    ang = jnp.where(on_row[None, :], ang_row[:, f], ang_col[:, f])
    c = jnp.cos(ang)
    s = jnp.sin(ang)
    first = (p < axis_dim // 2)[None, :]
    return c, jnp.where(first, -s, 0.0), jnp.where(first, 0.0, s)


def _seg_mean_matrix():
    r = jnp.arange(LANES)
    return jnp.where((r[:, None] // HEAD_DIM) == (r[None, :] // HEAD_DIM), 1.0 / HEAD_DIM, 0.0).astype(BF16)


def _even_proj_kernel(*refs, rope, layer_slot):
    if rope:
        (x_ref, shift_ref, scale_ref, gain_ref, w_ref, qkg_ref, lbf_ref, lbb_ref, bd_ref, rc_ref, rm_ref, rp_ref,
         dq_ref, dk_ref, dv_ref, hq_ref, kf_ref, gf_ref, kb_ref, gb_ref, hv_ref, hg_ref) = refs
        tables = (rc_ref[...], rm_ref[...], rp_ref[...])
    else:
        (x_ref, shift_ref, scale_ref, gain_ref, w_ref, qkg_ref, lbf_ref, lbb_ref, bd_ref,
         dq_ref, dk_ref, dv_ref, hq_ref, kf_ref, gf_ref, kb_ref, gb_ref, hv_ref, hg_ref) = refs
        tables = None
    hb = _norm_mod(x_ref[...], gain_ref[...], shift_ref[...], scale_ref[...]).astype(BF16)
    bd = bd_ref[...]
    width = 4 * LANES

    def proj(group):
        return _dot(hb, w_ref[:, group * width:(group + 1) * width])

    def qk(group, gain, out_ref, mult):
        acc = proj(group)
        for s in range(4):
            y = _seg_rms(acc[:, s * LANES:(s + 1) * LANES], gain, bd)
            if tables is not None:
                y = _rope(y, *tables)
            out_ref[:, s * LANES:(s + 1) * LANES] = (y * mult).astype(BF16)

    qk(0, qkg_ref[0:1, :], dq_ref, ATTN_SCALE)
    qk(1, qkg_ref[1:2, :], dk_ref, 1.0)
    dv_ref[...] = proj(2).astype(BF16)
    hq_ref[...] = _silu(proj(3)).astype(BF16)

    def forget(group, lb_ref, k_ref, g_ref):
        raw = lb_ref[...]
        e = jnp.exp(raw - jnp.max(raw, axis=0, keepdims=True))
        lb = jnp.sum(e[0:layer_slot + 1, :], axis=0, keepdims=True) / jnp.sum(e, axis=0, keepdims=True)
        f = lb + (1.0 - lb) * jax.nn.sigmoid(proj(group))
        k_ref[...] = (1.0 - f).astype(BF16)
        g_ref[...] = jnp.log(f)

    forget(4, lbf_ref, kf_ref, gf_ref)
    forget(5, lbb_ref, kb_ref, gb_ref)
    hv_ref[...] = proj(6).astype(BF16)
    hg_ref[...] = _silu(proj(7)).astype(BF16)


def _row_spec(tm, width):
    return pl.BlockSpec((None, tm, width), lambda b, i: (b, i, 0))


def _bcast_spec(width):
    return pl.BlockSpec((None, 1, width), lambda b, i: (b, 0, 0))


def _const_spec(shape):
    nd = len(shape)
    return pl.BlockSpec(shape, lambda b, i: (0,) * nd)


def _even_proj(x, shift, scale, gain, w, qk_gain, lb_fwd, lb_bwd, layer_slot, rope):
    bsz, n, d = x.shape
    tm = min(PROJ_ROWS, n)
    width = 4 * LANES
    qkg = jnp.tile(qk_gain.astype(F32), (1, 2))
    in_specs = [
        _row_spec(tm, d), _bcast_spec(d), _bcast_spec(d), _const_spec((1, d)), _const_spec(w.shape),
        _const_spec((2, LANES)), _const_spec(lb_fwd.shape), _const_spec(lb_bwd.shape), _const_spec((LANES, LANES)),
    ]
    args = [x, shift, scale, gain.reshape(1, d), w, qkg, lb_fwd, lb_bwd, _seg_mean_matrix()]
    if rope:
        tab_spec = pl.BlockSpec((tm, LANES), lambda b, i: (i, 0))
        in_specs += [tab_spec] * 3
        args += list(_rope_tables(n))
    out_dtypes = [BF16, BF16, BF16, BF16, BF16, F32, BF16, F32, BF16, BF16]
    return pl.pallas_call(
        functools.partial(_even_proj_kernel, rope=rope, layer_slot=layer_slot),
        grid=(bsz, n // tm),
        in_specs=in_specs,
        out_specs=[_row_spec(tm, width)] * len(out_dtypes),
        out_shape=[jax.ShapeDtypeStruct((bsz, n, width), dt) for dt in out_dtypes],
        compiler_params=_cparams(("parallel", "parallel")),
        name="even_proj_x" if rope else "even_proj_ctx",
    )(*args)


def _odd_proj_kernel(*refs, rope):
    if rope:
        (x_ref, shift_ref, scale_ref, gain_ref, w_ref, gqg_ref, nag_ref, bd_ref, rc_ref, rm_ref, rp_ref,
         gq_ref, gkv_ref, nq_ref, nk_ref, nv_ref) = refs
        tables = (rc_ref[...], rm_ref[...], rp_ref[...])
    else:
        (x_ref, shift_ref, scale_ref, gain_ref, w_ref, gqg_ref, nag_ref, bd_ref,
         gq_ref, gkv_ref, nq_ref, nk_ref, nv_ref) = refs
        tables = None
    hb = _norm_mod(x_ref[...], gain_ref[...], shift_ref[...], scale_ref[...]).astype(BF16)
    bd = bd_ref[...]

    def slab(acc, s, gain, use_rope, mult):
        y = _seg_rms(acc[:, s * LANES:(s + 1) * LANES], gain, bd)
        if use_rope and tables is not None:
            y = _rope(y, *tables)
        return (y * mult).astype(BF16)

    q_w = GQA_HEADS * HEAD_DIM
    acc = _dot(hb, w_ref[:, 0:q_w])
    for s in range(q_w // LANES):
        gq_ref[:, s * LANES:(s + 1) * LANES] = slab(acc, s, gqg_ref[0:1, :], True, ATTN_SCALE)
    acc = _dot(hb, w_ref[:, q_w:q_w + 2 * LANES])
    gkv_ref[:, 0:LANES] = slab(acc, 0, gqg_ref[1:2, :], True, 1.0)
    gkv_ref[:, LANES:2 * LANES] = acc[:, LANES:2 * LANES].astype(BF16)
    base = q_w + 2 * LANES
    na_w = NA_HEADS * HEAD_DIM
    acc = _dot(hb, w_ref[:, base:base + na_w])
    for s in range(na_w // LANES):
        nq_ref[:, s * LANES:(s + 1) * LANES] = slab(acc, s, nag_ref[0:1, :], False, ATTN_SCALE)
    acc = _dot(hb, w_ref[:, base + na_w:base + 2 * na_w])
    for s in range(na_w // LANES):
        nk_ref[:, s * LANES:(s + 1) * LANES] = slab(acc, s, nag_ref[1:2, :], False, 1.0)
    nv_ref[...] = _dot(hb, w_ref[:, base + 2 * na_w:base + 3 * na_w]).astype(BF16)


def _odd_proj(x, shift, scale, gain, w, gqa_gain, na_gain, rope):
    bsz, n, d = x.shape
    tm = min(PROJ_ROWS, n)
    gqg = jnp.tile(gqa_gain.astype(F32), (1, 2))
    nag = jnp.tile(na_gain.astype(F32), (1, 2))
    in_specs = [
        _row_spec(tm, d), _bcast_spec(d), _bcast_spec(d), _const_spec((1, d)), _const_spec(w.shape),
        _const_spec((2, LANES)), _const_spec((2, LANES)), _const_spec((LANES, LANES)),
    ]
    args = [x, shift, scale, gain.reshape(1, d), w, gqg, nag, _seg_mean_matrix()]
    if rope:
        tab_spec = pl.BlockSpec((tm, LANES), lambda b, i: (i, 0))
        in_specs += [tab_spec] * 3
        args += list(_rope_tables(n))
    widths = [GQA_HEADS * HEAD_DIM, 2 * LANES, NA_HEADS * HEAD_DIM, NA_HEADS * HEAD_DIM, NA_HEADS * HEAD_DIM]
    return pl.pallas_call(
        functools.partial(_odd_proj_kernel, rope=rope),
        grid=(bsz, n // tm),
        in_specs=in_specs,
        out_specs=[_row_spec(tm, wd) for wd in widths],
        out_shape=[jax.ShapeDtypeStruct((bsz, n, wd), BF16) for wd in widths],
        compiler_params=_cparams(("parallel", "parallel")),
        name="odd_proj_x" if rope else "odd_proj_ctx",
    )(*args)


def _flash_scratch(rows):
    return [pltpu.VMEM((rows, FLASH_TK), F32),
            pltpu.VMEM((rows, FLASH_TK), F32),
            pltpu.VMEM((rows, 1), F32),
            pltpu.VMEM((rows, 2 * LANES), F32)]


def _flash(qs, sources, scratch):
    s0_ref, s1_ref, m_ref, acc_ref = scratch
    s_refs = (s0_ref, s1_ref)
    rows = qs.shape[0]
    tk = s0_ref.shape[1]
    m_ref[...] = jnp.full(m_ref.shape, -jnp.inf, F32)
    acc_ref[...] = jnp.zeros(acc_ref.shape, F32)

    def issue(slot, width, k):
        s_refs[slot][:, 0:width] = _dot_nt(qs, k)

    def consume(slot, width, v):
        s = s_refs[slot][:, 0:width]
        m_old = m_ref[...]
        m_new = jnp.maximum(m_old, jnp.max(s, axis=-1, keepdims=True))
        alpha = jnp.exp(m_old - m_new)
        p = jnp.exp((s - m_new).astype(BF16))
        ones = (lax.broadcasted_iota(I32, (width, LANES), 1) == 0).astype(BF16)
        m_ref[...] = m_new
        acc_ref[...] = alpha * acc_ref[...] + _dot(p, jnp.concatenate([v, ones], axis=1))

    issued = 0
    prev = None
    for k_ref, v_ref, length in sorted(sources, key=lambda src: src[2]):
        chunk = min(tk, length)
        steps = length // chunk
        if steps == 1:
            slot = issued % 2
            issue(slot, chunk, k_ref[...])
            if prev is not None:
                consume(prev[0], prev[1], prev[2]())
            prev = (slot, chunk, lambda v_ref=v_ref: v_ref[...])
            issued += 1
            continue
        assert steps % 2 == 0 and chunk == tk
        base = issued % 2

        def kv(ref, c):
            return ref[pl.ds(pl.multiple_of(c * tk, tk), tk), :]

        issue(base, tk, kv(k_ref, 0))
        if prev is not None:
            consume(prev[0], prev[1], prev[2]())

        def body(j, carry, k_ref=k_ref, v_ref=v_ref, base=base):
            issue(1 - base, tk, kv(k_ref, 2 * j + 1))
            consume(base, tk, kv(v_ref, 2 * j))
            issue(base, tk, kv(k_ref, 2 * j + 2))
            consume(1 - base, tk, kv(v_ref, 2 * j + 1))
            return carry

        lax.fori_loop(0, steps // 2 - 1, body, 0)
        issue(1 - base, tk, kv(k_ref, steps - 1))
        consume(base, tk, kv(v_ref, steps - 2))
        prev = (1 - base, tk, lambda v_ref=v_ref, steps=steps: kv(v_ref, steps - 1))
        issued += steps
    consume(prev[0], prev[1], prev[2]())
    return acc_ref[:, 0:LANES] / acc_ref[:, LANES:LANES + 1]


def _lane_ids(shape):
    return lax.broadcasted_iota(I32, shape, len(shape) - 1)


def _diff_attn_kernel(*refs, n_src, lens, lam_init):
    q_ref = refs[0]
    kv_refs = refs[1:1 + 2 * n_src]
    lam_ref, gain_ref, o_ref = refs[1 + 2 * n_src:4 + 2 * n_src]
    scratch = refs[4 + 2 * n_src:]
    q = q_ref[...]
    tq = q.shape[0]
    lo = _lane_ids(q.shape) < HEAD_DIM
    zero = jnp.zeros_like(q)
    qs = jnp.concatenate([jnp.where(lo, q, zero), jnp.where(lo, zero, q)], axis=0)
    sources = [(kv_refs[2 * i], kv_refs[2 * i + 1], lens[i]) for i in range(n_src)]
    a = _flash(qs, sources, scratch)
    lp = lam_ref[...]
    lam = (jnp.exp(jnp.sum(lp[0:1, :] * lp[1:2, :], axis=-1, keepdims=True))
           - jnp.exp(jnp.sum(lp[2:3, :] * lp[3:4, :], axis=-1, keepdims=True)) + lam_init)
    o = a[0:tq, :] - lam * a[tq:2 * tq, :]
    ms = jnp.mean(o * o, axis=-1, keepdims=True)
    o = o * lax.rsqrt(ms + EPS) * gain_ref[...] * (1.0 - lam_init)
    o_ref[...] = o.astype(BF16)


def _diff_attention(q, kv_list, lam_params, out_gain, lam_init):
    bsz, n, width = q.shape
    tq = min(DIFF_TQ, n)
    lens = tuple(k.shape[1] for k, _ in kv_list)
    in_specs = [pl.BlockSpec((None, tq, LANES), lambda b, h, i: (b, i, h))]
    args = [q]
    for (k, v), length in zip(kv_list, lens):
        spec = pl.BlockSpec((None, length, LANES), lambda b, h, i: (b, 0, h))
        in_specs += [spec, spec]
        args += [k, v]
    in_specs += [pl.BlockSpec(lam_params.shape, lambda b, h, i: (0, 0)),
                 pl.BlockSpec((1, LANES), lambda b, h, i: (0, 0))]
    args += [lam_params.astype(F32), out_gain.reshape(1, LANES).astype(F32)]
    return pl.pallas_call(
        functools.partial(_diff_attn_kernel, n_src=len(kv_list), lens=lens, lam_init=lam_init),
        grid=(bsz, DIFF_HEADS, n // tq),
        in_specs=in_specs,
        out_specs=pl.BlockSpec((None, tq, LANES), lambda b, h, i: (b, i, h)),
        out_shape=jax.ShapeDtypeStruct((bsz, n, width), BF16),
        scratch_shapes=_flash_scratch(2 * tq),
        compiler_params=_cparams(("parallel", "parallel", "parallel")),
        name="diff_attn",
    )(*args)


def _gqa_kernel(q_ref, kc_ref, vc_ref, kx_ref, vx_ref, o_ref, *scratch, lens):
    tq = q_ref.shape[0]
    sources = [(kc_ref, vc_ref, lens[0]), (kx_ref, vx_ref, lens[1])]
    lanes = _lane_ids((tq, LANES))
    for kv in range(GQA_KV_HEADS):
        mine = (lanes // HEAD_DIM) == kv
        rows = []
        for half in range(2):
            hh = q_ref[:, (2 * kv + half) * LANES:(2 * kv + half + 1) * LANES]
            sw = pltpu.roll(hh.astype(F32), HEAD_DIM, 1).astype(BF16)
            zero = jnp.zeros_like(hh)
            a_here, b_here = (hh, sw) if kv == 0 else (sw, hh)
            rows += [jnp.where(mine, a_here, zero), jnp.where(mine, b_here, zero)]
        qs = jnp.concatenate(rows, axis=0)
        o = _flash(qs, sources, scratch)
        for half in range(2):
            oa = o[(2 * half) * tq:(2 * half + 1) * tq, :]
            ob = o[(2 * half + 1) * tq:(2 * half + 2) * tq, :]
            oa_sw = pltpu.roll(oa, HEAD_DIM, 1)
            ob_sw = pltpu.roll(ob, HEAD_DIM, 1)
            if kv == 0:
                res = jnp.where(lanes < HEAD_DIM, oa, ob_sw)
            else:
                res = jnp.where(lanes < HEAD_DIM, oa_sw, ob)
            o_ref[:, (2 * kv + half) * LANES:(2 * kv + half + 1) * LANES] = res.astype(BF16)


def _gqa_attention(q, kv_c, kv_x):
    bsz, n, width = q.shape
    tq = min(GQA_TQ, n)
    lens = (kv_c.shape[1], kv_x.shape[1])

    def kspec(length, col):
        return pl.BlockSpec((None, length, LANES), lambda b, i, col=col: (b, 0, col))

    return pl.pallas_call(
        functools.partial(_gqa_kernel, lens=lens),
        grid=(bsz, n // tq),
        in_specs=[pl.BlockSpec((None, tq, width), lambda b, i: (b, i, 0)),
                  kspec(lens[0], 0), kspec(lens[0], 1), kspec(lens[1], 0), kspec(lens[1], 1)],
        out_specs=pl.BlockSpec((None, tq, width), lambda b, i: (b, i, 0)),
        out_shape=jax.ShapeDtypeStruct((bsz, n, width), BF16),
        scratch_shapes=_flash_scratch(4 * tq),
        compiler_params=_cparams(("parallel", "parallel")),
        name="gqa_attn",
    )(q, kv_c, kv_c, kv_x, kv_x)


def _gla_chunks(chains, lt, ut):
    c, s16 = HGRN_CHUNK, HGRN_SUB
    nsub = c // s16
    ridx = lax.broadcasted_iota(I32, (c, 1), 0)
    sidx = lax.broadcasted_iota(I32, (s16, 1), 0)
    cums = [jnp.dot(ut if rev else lt, g, precision=HIGHEST, preferred_element_type=F32)
            for _, _, g, _, _, rev in chains]
    tots = [cum[0:1, :] if ch[5] else cum[c - 1:c, :] for ch, cum in zip(chains, cums)]
    vbs = [ch[3].astype(BF16) for ch in chains]
    out_states = [_dot_nt((ch[0] * jnp.exp(cum)).astype(BF16), ch[4].astype(BF16))
                  for ch, cum in zip(chains, cums)]
    new_sts = [ch[4] * jnp.exp(tot) + _dot_tn(vb, (ch[1] * jnp.exp(tot - cum)).astype(BF16))
               for ch, cum, tot, vb in zip(chains, cums, tots, vbs)]
    blocks = [[None] * nsub for _ in chains]
    for i in range(nsub):
        starts = [c - s16 * (i + 1) if ch[5] else s16 * i for ch in chains]
        accs = [jnp.zeros((s16, ch[3].shape[1]), F32) for ch in chains]
        if i > 0:
            scores = []
            for (q, k, _, _, _, rev), cum, r0 in zip(chains, cums, starts):
                if rev:
                    bnd = cum[r0 + s16:r0 + s16 + 1, :]
                    prev = ridx >= r0 + s16
                else:
                    bnd = cum[r0 - 1:r0, :]
                    prev = ridx < r0
                qt = (q[r0:r0 + s16] * jnp.exp(cum[r0:r0 + s16] - bnd)).astype(BF16)
                kt = (k * jnp.exp(jnp.where(prev, bnd - cum, -jnp.inf))).astype(BF16)
                scores.append(_dot_nt(qt, kt).astype(BF16))
            accs = [_dot(a, vb) for a, vb in zip(scores, vbs)]
        for s in range(s16):
            for ci, ((q, k, _, v, _, rev), cum, r0) in enumerate(zip(chains, cums, starts)):
                cum_i = cum[r0:r0 + s16]
                valid = (sidx <= s) if rev else (sidx >= s)
                d = jnp.where(valid, cum_i - cum_i[s:s + 1, :], -jnp.inf)
                a = jnp.sum(q[r0:r0 + s16] * k[r0 + s:r0 + s + 1, :] * jnp.exp(d), axis=-1, keepdims=True)
                accs[ci] = accs[ci] + a * v[r0 + s:r0 + s + 1, :]
        for ci, ch in enumerate(chains):
            blocks[ci][nsub - 1 - i if ch[5] else i] = accs[ci]
    return [(o + jnp.concatenate(b, axis=0), st) for o, b, st in zip(out_states, blocks, new_sts)]


def _hgrn_kernel(qx_ref, kfx_ref, gfx_ref, kbx_ref, gbx_ref, vx_ref, sgx_ref,
                 qc_ref, kfc_ref, gfc_ref, kbc_ref, gbc_ref, vc_ref, sgc_ref,
                 gain_ref, lt_ref, ut_ref, ox_ref, oc_ref, accx_ref, accc_ref):
    c = HGRN_CHUNK
    lt, ut = lt_ref[...], ut_ref[...]
    heads = qx_ref.shape[1] // LANES

    def sweep(q_ref, kf_ref, gf_ref, kb_ref, gb_ref, v_ref, acc_ref, states):
        nchunks = q_ref.shape[0] // c
        acc_ref[...] = jnp.zeros(acc_ref.shape, F32)

        def body(i, carry):
            rf = pl.multiple_of(i * c, c)
            rb = pl.multiple_of((nchunks - 1 - i) * c, c)
            chains = []
            for h in range(heads):
                st_f, st_b = carry[h]
                cols = slice(h * LANES, (h + 1) * LANES)

                def rows(ref, r0, cols=cols):
                    return ref[pl.ds(r0, c), cols].astype(F32)

                chains.append((rows(q_ref, rf), rows(kf_ref, rf), rows(gf_ref, rf), rows(v_ref, rf), st_f, False))
                chains.append((rows(q_ref, rb), rows(kb_ref, rb), rows(gb_ref, rb), rows(v_ref, rb), st_b, True))
            res = _gla_chunks(chains, lt, ut)
            for h in range(heads):
                cols = slice(h * LANES, (h + 1) * LANES)
                acc_ref[pl.ds(rf, c), cols] += res[2 * h][0]
                acc_ref[pl.ds(rb, c), cols] += res[2 * h + 1][0]
            return tuple((res[2 * h][1], res[2 * h + 1][1]) for h in range(heads))

        return lax.fori_loop(0, nchunks, body, states)

    zero = jnp.zeros((LANES, LANES), F32)
    states = sweep(qc_ref, kfc_ref, gfc_ref, kbc_ref, gbc_ref, vc_ref, accc_ref,
                   tuple((zero, zero) for _ in range(heads)))
    sweep(qx_ref, kfx_ref, gfx_ref, kbx_ref, gbx_ref, vx_ref, accx_ref, states)

    def finish(acc_ref, sg_ref, o_ref):
        rows = acc_ref.shape[0]
        tile = min(rows, 512)

        def body(i, _):
            r = pl.multiple_of(i * tile, tile)
            for h in range(heads):
                cols = slice(h * LANES, (h + 1) * LANES)
                o = acc_ref[pl.ds(r, tile), cols]
                ms = jnp.mean(o * o, axis=-1, keepdims=True)
                o = o * lax.rsqrt(ms + EPS) * gain_ref[...]
                o_ref[pl.ds(r, tile), cols] = (o * sg_ref[pl.ds(r, tile), cols].astype(F32)).astype(BF16)
            return 0

        lax.fori_loop(0, rows // tile, body, 0)

    finish(accx_ref, sgx_ref, ox_ref)
    finish(accc_ref, sgc_ref, oc_ref)


def _hgrn(px, pc, out_gain):
    bsz, n, width = px[0].shape
    m = pc[0].shape[1]
    c = HGRN_CHUNK
    r = jnp.arange(c)
    lt = (r[:, None] >= r[None, :]).astype(F32)
    ut = (r[:, None] <= r[None, :]).astype(F32)

    bw = HGRN_HEADS_PER_STEP * LANES

    def spec(length):
        return pl.BlockSpec((None, length, bw), lambda b, h: (b, 0, h))

    const = lambda shape: pl.BlockSpec(shape, lambda b, h: (0, 0))
    return pl.pallas_call(
        _hgrn_kernel,
        grid=(bsz, width // bw),
        in_specs=[spec(n)] * 7 + [spec(m)] * 7 + [const((1, LANES)), const((c, c)), const((c, c))],
        out_specs=[spec(n), spec(m)],
        out_shape=[jax.ShapeDtypeStruct((bsz, n, width), BF16), jax.ShapeDtypeStruct((bsz, m, width), BF16)],
        scratch_shapes=[pltpu.VMEM((n, bw), F32), pltpu.VMEM((m, bw), F32)],
        compiler_params=_cparams(("parallel", "parallel")),
        name="hgrn2",
    )(*px, *pc, out_gain.reshape(1, LANES).astype(F32), lt, ut)


def _na_bias_tables(rpb, rows):
    qrows, band, heads = NA_QROWS, NA_BAND, rpb.shape[0]
    pad = GRID_W - NA_COLS
    wide = jnp.pad(rpb.astype(F32), ((0, 0), (0, 0), (pad, pad)))
    toeplitz = jnp.stack([wide[:, :, GRID_W - 1 - qc:2 * GRID_W - 1 - qc] for qc in range(GRID_W)], axis=2)
    qc = np.arange(GRID_W)[:, None]
    kc = np.arange(GRID_W)[None, :]
    cstart = np.clip(qc - NA_COLS // 2, 0, GRID_W - NA_COLS)
    col_ok = (kc >= cstart) & (kc < cstart + NA_COLS)
    toeplitz = jnp.where(col_ok[None, None], toeplitz, NEG_BIG)
    tabs = []
    for r0 in (0, qrows, rows - qrows):
        rs = min(max(r0 - NA_ROWS // 2, 0), rows - band)
        qr = r0 + np.arange(qrows)[:, None]
        kr = rs + np.arange(band)[None, :]
        rstart = np.clip(qr - NA_ROWS // 2, 0, rows - NA_ROWS)
        row_ok = (kr >= rstart) & (kr < rstart + NA_ROWS)
        dr = np.clip(kr - qr + NA_ROWS - 1, 0, 2 * NA_ROWS - 2)
        tiles = jnp.take(toeplitz, jnp.asarray(dr.reshape(-1), I32), axis=1)
        tiles = jnp.where(row_ok.reshape(1, -1, 1, 1), tiles, NEG_BIG)
        tiles = tiles.reshape(heads, qrows, band, GRID_W, GRID_W).transpose(0, 1, 3, 2, 4)
        tabs.append(tiles.reshape(heads, qrows * GRID_W, band * GRID_W))
    return jnp.stack(tabs)


def _na_kernel(q_ref, k_ref, v_ref, kc_ref, vc_ref, bias_ref, o_ref, *, rows):
    j = pl.program_id(2)
    tq = q_ref.shape[0]
    band = NA_BAND * GRID_W
    rs = jnp.clip(j * NA_QROWS - NA_ROWS // 2, 0, rows - NA_BAND)
    start = pl.multiple_of(rs * GRID_W, NA_ROWS // 2 * GRID_W)
    q = q_ref[...]
    lo = _lane_ids(q.shape) < HEAD_DIM
    zero = jnp.zeros_like(q)
    qs = jnp.concatenate([jnp.where(lo, q, zero), jnp.where(lo, zero, q)], axis=0)
    kb = k_ref[pl.ds(start, band), :]
    vb = v_ref[pl.ds(start, band), :]
    s_win = _dot_nt(qs, kb) + jnp.concatenate([bias_ref[0], bias_ref[1]], axis=0)
    s_ctx = _dot_nt(qs, kc_ref[...])
    m = jnp.maximum(jnp.max(s_win, axis=-1, keepdims=True), jnp.max(s_ctx, axis=-1, keepdims=True))
    p_win = jnp.exp(s_win - m)
    p_ctx = jnp.exp(s_ctx - m)
    l = jnp.sum(p_win, axis=-1, keepdims=True) + jnp.sum(p_ctx, axis=-1, keepdims=True)
    o = (_dot(p_ctx.astype(BF16), vc_ref[...]) + _dot(p_win.astype(BF16), vb)) / l
    o_ref[...] = jnp.where(lo, o[0:tq, :], o[tq:2 * tq, :]).astype(BF16)


def _na_attention(q, k, v, kc, vc, rpb):
    bsz, n, width = q.shape
    rows = n // GRID_W
    tq = NA_QROWS * GRID_W
    nt = n // tq
    bias = _na_bias_tables(rpb, rows)
    m = kc.shape[1]

    def cls(j):
        return jnp.where(j == 0, 0, jnp.where(j == nt - 1, 2, 1))

    full = lambda length: pl.BlockSpec((None, length, LANES), lambda b, h, j: (b, 0, h))
    return pl.pallas_call(
        functools.partial(_na_kernel, rows=rows),
        grid=(bsz, NA_HEADS // 2, nt),
        in_specs=[pl.BlockSpec((None, tq, LANES), lambda b, h, j: (b, j, h)),
                  full(n), full(n), full(m), full(m),
                  pl.BlockSpec((None, 2, tq, NA_BAND * GRID_W), lambda b, h, j: (cls(j), h, 0, 0))],
        out_specs=pl.BlockSpec((None, tq, LANES), lambda b, h, j: (b, j, h)),
        out_shape=jax.ShapeDtypeStruct((bsz, n, width), BF16),
        compiler_params=_cparams(("parallel", "parallel", "parallel")),
        name="na_attn",
    )(q, k, v, kc, vc, bias)


def _out_proj_kernel(a_ref, b_ref, w_ref, x_ref, gate_ref, gain_ref, shift_ref, scale_ref, r_ref,
                     x1_ref, tok_ref, logit_ref):
    half = a_ref.shape[1]
    y = _dot(a_ref[...], w_ref[0:half, :]) + _dot(b_ref[...], w_ref[half:2 * half, :])
    x1 = x_ref[...] + gate_ref[...] * y
    x1_ref[...] = x1
    h = _norm_mod(x1, gain_ref[...], shift_ref[...], scale_ref[...])
    logit_ref[...] = jnp.dot(h, r_ref[...], precision=HIGHEST, preferred_element_type=F32)
    tok_ref[...] = _pack_pair(h[:, 0:HALF], h[:, HALF:2 * HALF])


def _out_proj(a, b, w, x, gate, gain, shift, scale, router_pad):
    bsz, n, d = x.shape
    tm = min(PROJ_ROWS, n)
    return pl.pallas_call(
        _out_proj_kernel,
        grid=(bsz, n // tm),
        in_specs=[_row_spec(tm, a.shape[2]), _row_spec(tm, b.shape[2]), _const_spec(w.shape), _row_spec(tm, d),
                  _bcast_spec(d), _const_spec((1, d)), _bcast_spec(d), _bcast_spec(d), _const_spec(router_pad.shape)],
        out_specs=[_row_spec(tm, d), _row_spec(tm, HALF), _row_spec(tm, LANES)],
        out_shape=[jax.ShapeDtypeStruct((bsz, n, d), F32), jax.ShapeDtypeStruct((bsz, n, HALF), I32),
                   jax.ShapeDtypeStruct((bsz, n, LANES), F32)],
        compiler_params=_cparams(("parallel", "parallel")),
        name="out_proj",
    )(a, b, w, x, gate, gain.reshape(1, d), shift, scale, router_pad)


def _route_kernel(logit_ref, bias_ref, tri_ref, idx_ref, w_ref, pos_ref, cnt_ref, masked_ref, carry_ref):
    step = pl.program_id(0)

    @pl.when(step == 0)
    def _():
        carry_ref[...] = jnp.zeros_like(carry_ref)

    tr = logit_ref.shape[0]
    scores = jax.nn.sigmoid(logit_ref[...].T[0:N_EXPERTS, :])
    biased = scores + bias_ref[...]
    gsz = EXPERTS_PER_GROUP
    sub = lax.broadcasted_iota(I32, (gsz, tr), 0).astype(F32)
    gscore = []
    for g in range(N_GROUPS):
        bg = biased[g * gsz:(g + 1) * gsz, :]
        m1 = jnp.max(bg, axis=0, keepdims=True)
        i1 = jnp.min(jnp.where(bg == m1, sub, float(gsz)), axis=0, keepdims=True)
        m2 = jnp.max(jnp.where(sub == i1, -jnp.inf, bg), axis=0, keepdims=True)
        gscore.append(m1 + m2)
    for g in range(N_GROUPS):
        beaten = jnp.zeros((1, tr), F32)
        for o in range(N_GROUPS):
            if o == g:
                continue
            wins = (gscore[o] >= gscore[g]) if o < g else (gscore[o] > gscore[g])
            beaten = beaten + jnp.where(wins, 1.0, 0.0)
        keep = beaten < float(TOPK_GROUPS)
        masked_ref[g * gsz:(g + 1) * gsz, :] = jnp.where(keep, biased[g * gsz:(g + 1) * gsz, :], -jnp.inf)
    cur = masked_ref[...]
    eid = lax.broadcasted_iota(I32, (N_EXPERTS, tr), 0).astype(F32)
    sel = jnp.zeros((N_EXPERTS, tr), F32)
    picks, weights = [], []
    for _ in range(TOP_K):
        m = jnp.max(cur, axis=0, keepdims=True)
        ik = jnp.min(jnp.where(cur == m, eid, float(N_EXPERTS)), axis=0, keepdims=True)
        hit = eid == ik
        weights.append(jnp.sum(jnp.where(hit, scores, 0.0), axis=0, keepdims=True))
        sel = sel + jnp.where(hit, 1.0, 0.0)
        cur = jnp.where(hit, -jnp.inf, cur)
        picks.append(ik)
    wsum = weights[0]
    for wk in weights[1:]:
        wsum = wsum + wk
    before = _dot(sel.astype(BF16), tri_ref[...]) + carry_ref[:, 0:1]
    for kk in range(TOP_K):
        idx_ref[kk:kk + 1, :] = picks[kk].astype(I32)
        w_ref[kk:kk + 1, :] = weights[kk] / wsum * ROUTED_SCALE
        pos_ref[kk:kk + 1, :] = jnp.sum(jnp.where(eid == picks[kk], before, 0.0), axis=0, keepdims=True).astype(I32)
    carry_ref[...] = carry_ref[...] + jnp.sum(sel, axis=1, keepdims=True)
    cnt_ref[...] = carry_ref[...].astype(I32)


def _route(logits, router_bias):
    t = logits.shape[0]
    tr = ROUTE_ROWS if t % ROUTE_ROWS == 0 else LANES
    r = jnp.arange(tr)
    tri = (r[:, None] < r[None, :]).astype(BF16)
    kt_spec = pl.BlockSpec((TOP_K, tr), lambda i: (0, i))
    return pl.pallas_call(
        _route_kernel,
        grid=(t // tr,),
        in_specs=[pl.BlockSpec((tr, LANES), lambda i: (i, 0)),
                  pl.BlockSpec((N_EXPERTS, 1), lambda i: (0, 0)),
                  pl.BlockSpec((tr, tr), lambda i: (0, 0))],
        out_specs=[kt_spec, kt_spec, kt_spec, pl.BlockSpec((N_EXPERTS, LANES), lambda i: (0, 0))],
        out_shape=[jax.ShapeDtypeStruct((TOP_K, t), I32), jax.ShapeDtypeStruct((TOP_K, t), F32),
                   jax.ShapeDtypeStruct((TOP_K, t), I32), jax.ShapeDtypeStruct((N_EXPERTS, LANES), I32)],
        scratch_shapes=[pltpu.VMEM((N_EXPERTS, tr), F32), pltpu.VMEM((N_EXPERTS, LANES), F32)],
        compiler_params=_cparams(("arbitrary",)),
        name="moe_route",
    )(logits, router_bias.astype(F32).reshape(N_EXPERTS, 1), tri)


def _row_copy(src_ref, dst_ref, sem):
    return pltpu.make_async_copy(src_ref, dst_ref, sem)


def _dispatch_kernel(dest_ref, tok_ref, rows_in_ref, rows_ref, sem):
    del rows_in_ref
    tt = tok_ref.shape[0]

    def body(t, _):
        for kk in range(TOP_K):
            d = dest_ref[t * TOP_K + kk]
            _row_copy(tok_ref.at[pl.ds(t, 1)], rows_ref.at[pl.ds(d, 1)], sem).start(priority=kk % 2)
        return 0

    lax.fori_loop(0, tt, body, 0)
    for _ in range(TOP_K):
        _row_copy(tok_ref, rows_ref.at[pl.ds(0, tt)], sem).wait()


def _dispatch(dest_flat, tok, rows_buf):
    t = tok.shape[0]
    tt = min(MOE_TOK, t)
    return pl.pallas_call(
        _dispatch_kernel,
        grid=(t // tt,),
        in_specs=[pl.BlockSpec((tt * TOP_K,), lambda i: (i,), memory_space=pltpu.SMEM),
                  pl.BlockSpec((tt, HALF), lambda i: (i, 0)),
                  pl.BlockSpec(memory_space=pl.ANY)],
        out_specs=pl.BlockSpec(memory_space=pl.ANY),
        out_shape=jax.ShapeDtypeStruct(rows_buf.shape, rows_buf.dtype),
        scratch_shapes=[pltpu.SemaphoreType.DMA(())],
        input_output_aliases={2: 0},
        compiler_params=_cparams(("arbitrary",)),
        name="moe_dispatch",
    )(dest_flat, tok, rows_buf)


def _expert_kernel(be_ref, nused_ref, x_ref, wgu_ref, wd_ref, y_ref):
    @pl.when(pl.program_id(0) < nused_ref[0])
    def _():
        lo, hi = _unpack_pair(x_ref[...])
        gu = _dot(lo.astype(BF16), wgu_ref[0:HALF, :]) + _dot(hi.astype(BF16), wgu_ref[HALF:2 * HALF, :])
        h = (_silu(gu[:, 0:EXPERT_DIM]) * gu[:, EXPERT_DIM:2 * EXPERT_DIM]).astype(BF16)
        y = _dot(h, wd_ref[...])
        y_ref[...] = _pack_pair(y[:, 0:HALF], y[:, HALF:2 * HALF])

    @pl.when(pl.program_id(0) >= nused_ref[0])
    def _():
        y_ref[...] = jnp.zeros_like(y_ref)


def _experts(block_expert, nused, rows, wgu, wd):
    n_rows = rows.shape[0]
    nb = n_rows // MOE_BLOCK

    def row_map(i, be, nu):
        return (jnp.minimum(i, nu[0] - 1), 0)

    def w_map(i, be, nu):
        return (be[jnp.minimum(i, nu[0] - 1)], 0, 0)

    grid_spec = pltpu.PrefetchScalarGridSpec(
        num_scalar_prefetch=2,
        grid=(nb,),
        in_specs=[pl.BlockSpec((MOE_BLOCK, HALF), row_map),
                  pl.BlockSpec((None, D_MODEL, 2 * EXPERT_DIM), w_map),
                  pl.BlockSpec((None, EXPERT_DIM, D_MODEL), w_map)],
        out_specs=pl.BlockSpec((MOE_BLOCK, HALF), lambda i, be, nu: (i, 0)),
    )
    return pl.pallas_call(
        _expert_kernel,
        grid_spec=grid_spec,
        out_shape=jax.ShapeDtypeStruct((n_rows, HALF), I32),
        compiler_params=_cparams(("arbitrary",)),
        name="moe_experts",
    )(block_expert, nused, rows, wgu, wd)


def _combine_kernel(dest_ref, x1_ref, tok_ref, w_ref, gate_ref, wgu_ref, wd_ref, y_ref, o_ref, buf_ref, sem):
    tt = tok_ref.shape[0]

    def body(t, _):
        for kk in range(TOP_K):
            d = dest_ref[t * TOP_K + kk]
            _row_copy(y_ref.at[pl.ds(d, 1)], buf_ref.at[kk, pl.ds(t, 1)], sem).start(priority=kk % 2)
        return 0

    lax.fori_loop(0, tt, body, 0)
    lo, hi = _unpack_pair(tok_ref[...])
    gu = _dot(lo.astype(BF16), wgu_ref[0:HALF, :]) + _dot(hi.astype(BF16), wgu_ref[HALF:2 * HALF, :])
    h = (_silu(gu[:, 0:EXPERT_DIM]) * gu[:, EXPERT_DIM:2 * EXPERT_DIM]).astype(BF16)
    shared = _dot(h, wd_ref[...])
    for kk in range(TOP_K):
        _row_copy(y_ref.at[pl.ds(0, tt)], buf_ref.at[kk], sem).wait()
    acc_lo = shared[:, 0:HALF]
    acc_hi = shared[:, HALF:2 * HALF]
    for kk in range(TOP_K):
        ylo, yhi = _unpack_pair(buf_ref[kk])
        wk = w_ref[:, kk:kk + 1]
        acc_lo = acc_lo + wk * ylo
        acc_hi = acc_hi + wk * yhi
    gate = gate_ref[...]
    o_ref[:, 0:HALF] = x1_ref[:, 0:HALF] + gate[:, 0:HALF] * acc_lo
    o_ref[:, HALF:2 * HALF] = x1_ref[:, HALF:2 * HALF] + gate[:, HALF:2 * HALF] * acc_hi


def _combine(dest_flat, x1, tok, w_tok, gate, wgu, wd, y_rows, tokens_per_gate):
    t, d = x1.shape
    tt = min(MOE_TOK, t)
    per = tokens_per_gate // tt
    return pl.pallas_call(
        _combine_kernel,
        grid=(t // tt,),
        in_specs=[pl.BlockSpec((tt * TOP_K,), lambda i: (i,), memory_space=pltpu.SMEM),
                  pl.BlockSpec((tt, d), lambda i: (i, 0)),
                  pl.BlockSpec((tt, HALF), lambda i: (i, 0)),
                  pl.BlockSpec((tt, TOP_K), lambda i: (i, 0)),
                  pl.BlockSpec((None, 1, d), lambda i: (i // per, 0, 0)),
                  pl.BlockSpec(wgu.shape, lambda i: (0, 0)),
                  pl.BlockSpec(wd.shape, lambda i: (0, 0)),
                  pl.BlockSpec(memory_space=pl.ANY)],
        out_specs=pl.BlockSpec((tt, d), lambda i: (i, 0)),
        out_shape=jax.ShapeDtypeStruct((t, d), F32),
        scratch_shapes=[pltpu.VMEM((TOP_K, tt, HALF), I32), pltpu.SemaphoreType.DMA(())],
        compiler_params=_cparams(("arbitrary",)),
        name="moe_combine",
    )(dest_flat, x1, tok, w_tok, gate, wgu, wd, y_rows)


def _moe(parts, router_bias, w_gate, w_up, w_down, ws_gate, ws_up, ws_down):
    logits = jnp.concatenate([p[2].reshape(-1, LANES) for p in parts], axis=0)
    t = logits.shape[0]
    idx, w, pos, cnt = _route(logits, router_bias)
    counts = cnt[:, 0]
    padded = (counts + MOE_BLOCK - 1) // MOE_BLOCK * MOE_BLOCK
    pad_end = jnp.cumsum(padded)
    pad_start = pad_end - padded
    experts = jnp.arange(N_EXPERTS, dtype=I32)[:, None, None]
    seg_start = jnp.sum(jnp.where(idx[None] == experts, pad_start[:, None, None].astype(I32), 0), axis=0)
    dest = (seg_start + pos).T.reshape(-1)
    w_tok = w.T
    n_assign = t * TOP_K
    n_rows = ((n_assign + MOE_BLOCK - 1) // MOE_BLOCK + N_EXPERTS) * MOE_BLOCK
    nb = n_rows // MOE_BLOCK
    block_start = jnp.arange(nb, dtype=I32) * MOE_BLOCK
    block_expert = jnp.minimum(jnp.sum(block_start[:, None] >= pad_end[None, :], axis=1), N_EXPERTS - 1).astype(I32)
    nused = (pad_end[-1] // MOE_BLOCK).astype(I32).reshape(1)
    rows = jnp.zeros((n_rows, HALF), I32)
    off = 0
    for x1, tok, _, _, _ in parts:
        cnt_tok = tok.shape[0] * tok.shape[1]
        rows = _dispatch(dest[off * TOP_K:(off + cnt_tok) * TOP_K], tok.reshape(cnt_tok, HALF), rows)
        off += cnt_tok
    wgu = jnp.concatenate([w_gate, w_up], axis=-1).astype(BF16)
    y_rows = _experts(block_expert, nused, rows, wgu, w_down.astype(BF16))
    wsgu = jnp.concatenate([ws_gate, ws_up], axis=-1).astype(BF16)
    wsd = ws_down.astype(BF16)
    outs = []
    off = 0
    for x1, tok, _, gate, per in parts:
        cnt_tok = tok.shape[0] * tok.shape[1]
        o = _combine(dest[off * TOP_K:(off + cnt_tok) * TOP_K], x1.reshape(cnt_tok, D_MODEL),
                     tok.reshape(cnt_tok, HALF), w_tok[off:off + cnt_tok], gate, wsgu, wsd, y_rows, per)
        outs.append(o.reshape(x1.shape))
        off += cnt_tok
    return outs


def kernel(x, c, ctx, c_ctx, ada_w, ada_b, norm_mix, norm_ffn, ev_w_in, ev_w_out, diff_qk_gain, diff_lambda,
           diff_out_gain, hgrn_lb, hgrn_out_gain, od_w_in, od_w_out, gqa_qk_gain, na_qk_gain, na_rpb, moe_router,
           moe_router_bias, moe_w_gate, moe_w_up, moe_w_down, shared_w_gate, shared_w_up, shared_w_down):
    bsz, n, d = x.shape
    m = ctx.shape[1]
    depth = ada_w.shape[0]
    cond_rows = -(-(bsz + 1) // 8) * 8
    cond = jnp.zeros((cond_rows, d), F32).at[0:bsz].set(c).at[bsz].set(c_ctx)
    mods = _ada_mod(cond, ada_w, ada_b)

    xc = ctx
    for layer in range(depth):
        need_ctx = layer < depth - 1
        j = layer // 2
        mod = mods[layer].reshape(cond_rows, 6, d)
        mx = [mod[0:bsz, i][:, None, :] for i in range(6)]
        mc = [jnp.broadcast_to(mod[bsz:bsz + 1, i][:, None, :], (bsz, 1, d)) for i in range(6)]
        if layer % 2 == 0:
            w_in = ev_w_in[j].astype(BF16)
            px = _even_proj(x, mx[0], mx[1], norm_mix[layer], w_in, diff_qk_gain[j], hgrn_lb[0], hgrn_lb[1], j, True)
            pc = _even_proj(xc, mc[0], mc[1], norm_mix[layer], w_in, diff_qk_gain[j], hgrn_lb[0], hgrn_lb[1], j, False)
            lam_init = 0.8 - 0.6 * math.exp(-0.3 * layer)
            a_x = _diff_attention(px[0], [(pc[1], pc[2]), (px[1], px[2])], diff_lambda[j], diff_out_gain[j], lam_init)
            a_c = _diff_attention(pc[0], [(pc[1], pc[2])], diff_lambda[j], diff_out_gain[j], lam_init)
            b_x, b_c = _hgrn(px[3:], pc[3:], hgrn_out_gain[j])
            w_out = ev_w_out[j].astype(BF16)
        else:
            w_in = od_w_in[j].astype(BF16)
            px = _odd_proj(x, mx[0], mx[1], norm_mix[layer], w_in, gqa_qk_gain[j], na_qk_gain[j], True)
            pc = _odd_proj(xc, mc[0], mc[1], norm_mix[layer], w_in, gqa_qk_gain[j], na_qk_gain[j], False)
            a_x = _gqa_attention(px[0], pc[1], px[1])
            b_x = _na_attention(px[2], px[3], px[4], pc[3], pc[4], na_rpb[j])
            a_c = b_c = None
            w_out = od_w_out[j].astype(BF16)
        router_pad = jnp.zeros((d, LANES), F32).at[:, 0:N_EXPERTS].set(moe_router[layer].astype(F32))
        x1, tok_x, logit_x = _out_proj(a_x, b_x, w_out, x, mx[2], norm_ffn[layer], mx[3], mx[4], router_pad)
        parts = [(x1, tok_x, logit_x, mx[5], n)]
        if need_ctx:
            xc1, tok_c, logit_c = _out_proj(a_c, b_c, w_out, xc, mc[2], norm_ffn[layer], mc[3], mc[4], router_pad)
            parts.append((xc1, tok_c, logit_c, mc[5][0:1], bsz * m))
        outs = _moe(parts, moe_router_bias[layer], moe_w_gate[layer], moe_w_up[layer], moe_w_down[layer],
                    shared_w_gate[layer], shared_w_up[layer], shared_w_down[layer])
        x = outs[0]
        if need_ctx:
            xc = outs[1]
    return x
```

```python
import functools
import math

import jax
import jax.numpy as jnp
import numpy as np
from jax import lax
from jax.experimental import pallas as pl
from jax.experimental.pallas import tpu as pltpu

F32 = jnp.float32
BF16 = jnp.bfloat16
I32 = jnp.int32
HIGHEST = lax.Precision.HIGHEST

D_MODEL = 1024
GRID_W = 64
HEAD_DIM = 64
ATTN_SCALE = HEAD_DIM ** -0.5
ROPE_THETA = 10000.0
EPS = 1e-6
DIFF_HEADS = D_MODEL // 256
HGRN_HEADS = D_MODEL // 256
HGRN_CHUNK = 64
HGRN_SUB = 16
GQA_HEADS = D_MODEL // 128
GQA_KV_HEADS = GQA_HEADS // 4
NA_HEADS = D_MODEL // 128
NA_ROWS = 8
NA_COLS = 16
N_EXPERTS = 64
N_GROUPS = 8
EXPERTS_PER_GROUP = N_EXPERTS // N_GROUPS
TOPK_GROUPS = 4
TOP_K = 8
EXPERT_DIM = D_MODEL // 4
ROUTED_SCALE = 2.5
HALF = D_MODEL // 2

LANES = 128
VMEM_LIMIT_BYTES = 56 * 1024 * 1024
PROJ_ROWS = 512
FLASH_TK = 512
FLASH_SB = 64
DIFF_TQ = 512
GQA_TQ = 256
NA_QROWS = 8
NA_BAND = 16
HGRN_HEADS_PER_STEP = 2
MOE_BLOCK = 512
MOE_TILE = 256
RUN_ALIGN = 8
NEG_BIG = -1e30


def _cparams(sem):
    return pltpu.CompilerParams(dimension_semantics=sem, vmem_limit_bytes=VMEM_LIMIT_BYTES)


def _silu(x):
    return x * jax.nn.sigmoid(x)


def _dot(a, b):
    return jnp.dot(a, b, preferred_element_type=F32)


def _dot_nt(a, b):
    return lax.dot_general(a, b, (((1,), (1,)), ((), ())), preferred_element_type=F32)


def _dot_tn(a, b):
    return lax.dot_general(a, b, (((0,), (0,)), ((), ())), preferred_element_type=F32)


def _pack_pair(lo, hi):
    lo_bits = lax.bitcast_convert_type(lo.astype(BF16).astype(F32), I32)
    hi_bits = lax.bitcast_convert_type(hi.astype(BF16).astype(F32), I32)
    return lax.shift_right_logical(lo_bits, 16) | (hi_bits & jnp.int32(-65536))


def _pack_exact_pair(lo, hi):
    lo_bits = lax.bitcast_convert_type(lo, I32)
    hi_bits = lax.bitcast_convert_type(hi, I32)
    return lax.shift_right_logical(lo_bits, 16) | (hi_bits & jnp.int32(-65536))


def _unpack_pair(w):
    lo = lax.bitcast_convert_type(lax.shift_left(w, 16), F32)
    hi = lax.bitcast_convert_type(w & jnp.int32(-65536), F32)
    return lo, hi


def _ada_kernel(cond_ref, w_ref, b_ref, o_ref):
    s = _silu(cond_ref[...])
    o_ref[...] = jnp.dot(s, w_ref[...], precision=HIGHEST, preferred_element_type=F32) + b_ref[...]


def _ada_mod(cond, ada_w, ada_b):
    depth = ada_w.shape[0]
    rows = cond.shape[0]
    nblk = ada_w.shape[2] // D_MODEL
    return pl.pallas_call(
        _ada_kernel,
        grid=(depth, nblk),
        in_specs=[
            pl.BlockSpec((rows, D_MODEL), lambda l, j: (0, 0)),
            pl.BlockSpec((None, D_MODEL, D_MODEL), lambda l, j: (l, 0, j)),
            pl.BlockSpec((None, 1, D_MODEL), lambda l, j: (l, 0, j)),
        ],
        out_specs=pl.BlockSpec((None, rows, D_MODEL), lambda l, j: (l, 0, j)),
        out_shape=jax.ShapeDtypeStruct((depth, rows, nblk * D_MODEL), F32),
        compiler_params=_cparams(("parallel", "parallel")),
        name="ada_mod",
    )(cond, ada_w, ada_b.reshape(depth, 1, -1))


def _norm_mod(x, gain, shift, scale):
    ms = jnp.mean(x * x, axis=-1, keepdims=True)
    h = x * lax.rsqrt(ms + EPS) * gain
    return h * (1.0 + scale) + shift


def _seg_rms(acc, gain, bd):
    sq = acc * acc
    hi = sq.astype(BF16)
    lo = (sq - hi.astype(F32)).astype(BF16)
    ms = _dot(hi, bd) + _dot(lo, bd)
    return acc * lax.rsqrt(ms + EPS) * gain


def _rope(y, c, sm, sp):
    return y * c + pltpu.roll(y, LANES - 16, 1) * sm + pltpu.roll(y, 16, 1) * sp


def _rope_tables(n):
    pos = jnp.arange(n, dtype=I32)
    row = (pos // GRID_W).astype(F32)
    col = (pos % GRID_W).astype(F32)
    axis_dim = HEAD_DIM // 2
    inv_freq = ROPE_THETA ** (-jnp.arange(0, axis_dim, 2, dtype=F32) / axis_dim)
    ang_row = row[:, None] * inv_freq
    ang_col = col[:, None] * inv_freq
    lane = jnp.arange(LANES)
    p = lane % axis_dim
    f = p % (axis_dim // 2)
    on_row = ((lane % HEAD_DIM) // axis_dim) == 0
    ang = jnp.where(on_row[None, :], ang_row[:, f], ang_col[:, f])
    c = jnp.cos(ang)
    s = jnp.sin(ang)
    first = (p < axis_dim // 2)[None, :]
    return c, jnp.where(first, -s, 0.0), jnp.where(first, 0.0, s)


def _seg_mean_matrix():
    r = jnp.arange(LANES)
    return jnp.where((r[:, None] // HEAD_DIM) == (r[None, :] // HEAD_DIM), 1.0 / HEAD_DIM, 0.0).astype(BF16)


def _even_proj_kernel(*refs, rope, layer_slot):
    if rope:
        (x_ref, shift_ref, scale_ref, gain_ref, w_ref, qkg_ref, lbf_ref, lbb_ref, bd_ref, rc_ref, rm_ref, rp_ref,
         dq_ref, dk_ref, dv_ref, hq_ref, kf_ref, gf_ref, kb_ref, gb_ref, hv_ref, hg_ref) = refs
        tables = (rc_ref[...], rm_ref[...], rp_ref[...])
    else:
        (x_ref, shift_ref, scale_ref, gain_ref, w_ref, qkg_ref, lbf_ref, lbb_ref, bd_ref,
         dq_ref, dk_ref, dv_ref, hq_ref, kf_ref, gf_ref, kb_ref, gb_ref, hv_ref, hg_ref) = refs
        tables = None
    hb = _norm_mod(x_ref[...], gain_ref[...], shift_ref[...], scale_ref[...]).astype(BF16)
    bd = bd_ref[...]
    width = 4 * LANES

    def proj(group):
        return _dot(hb, w_ref[:, group * width:(group + 1) * width])

    def qk(group, gain, out_ref, mult):
        acc = proj(group)
        for s in range(4):
            y = _seg_rms(acc[:, s * LANES:(s + 1) * LANES], gain, bd)
            if tables is not None:
                y = _rope(y, *tables)
            out_ref[:, s * LANES:(s + 1) * LANES] = (y * mult).astype(BF16)

    qk(0, qkg_ref[0:1, :], dq_ref, ATTN_SCALE)
    qk(1, qkg_ref[1:2, :], dk_ref, 1.0)
    dv_ref[...] = proj(2).astype(BF16)
    hq_ref[...] = _silu(proj(3)).astype(BF16)

    def forget(group, lb_ref, k_ref, g_ref):
        raw = lb_ref[...]
        e = jnp.exp(raw - jnp.max(raw, axis=0, keepdims=True))
        lb = jnp.sum(e[0:layer_slot + 1, :], axis=0, keepdims=True) / jnp.sum(e, axis=0, keepdims=True)
        f = lb + (1.0 - lb) * jax.nn.sigmoid(proj(group))
        k_ref[...] = (1.0 - f).astype(BF16)
        g_ref[...] = jnp.log(f)

    forget(4, lbf_ref, kf_ref, gf_ref)
    forget(5, lbb_ref, kb_ref, gb_ref)
    hv_ref[...] = proj(6).astype(BF16)
    hg_ref[...] = _silu(proj(7)).astype(BF16)


def _row_spec(tm, width):
    return pl.BlockSpec((None, tm, width), lambda b, i: (b, i, 0))


def _bcast_spec(width):
    return pl.BlockSpec((None, 1, width), lambda b, i: (b, 0, 0))


def _const_spec(shape):
    nd = len(shape)
    return pl.BlockSpec(shape, lambda b, i: (0,) * nd)


def _even_proj(x, shift, scale, gain, w, qk_gain, lb_fwd, lb_bwd, layer_slot, rope):
    bsz, n, d = x.shape
    tm = min(PROJ_ROWS, n)
    width = 4 * LANES
    qkg = jnp.tile(qk_gain.astype(F32), (1, 2))
    in_specs = [
        _row_spec(tm, d), _bcast_spec(d), _bcast_spec(d), _const_spec((1, d)), _const_spec(w.shape),
        _const_spec((2, LANES)), _const_spec(lb_fwd.shape), _const_spec(lb_bwd.shape), _const_spec((LANES, LANES)),
    ]
    args = [x, shift, scale, gain.reshape(1, d), w, qkg, lb_fwd, lb_bwd, _seg_mean_matrix()]
    if rope:
        tab_spec = pl.BlockSpec((tm, LANES), lambda b, i: (i, 0))
        in_specs += [tab_spec] * 3
        args += list(_rope_tables(n))
    out_dtypes = [BF16, BF16, BF16, BF16, BF16, F32, BF16, F32, BF16, BF16]
    return pl.pallas_call(
        functools.partial(_even_proj_kernel, rope=rope, layer_slot=layer_slot),
        grid=(bsz, n // tm),
        in_specs=in_specs,
        out_specs=[_row_spec(tm, width)] * len(out_dtypes),
        out_shape=[jax.ShapeDtypeStruct((bsz, n, width), dt) for dt in out_dtypes],
        compiler_params=_cparams(("parallel", "parallel")),
        name="even_proj_x" if rope else "even_proj_ctx",
    )(*args)


def _odd_proj_kernel(*refs, rope):
    if rope:
        (x_ref, shift_ref, scale_ref, gain_ref, w_ref, gqg_ref, nag_ref, bd_ref, rc_ref, rm_ref, rp_ref,
         gq_ref, gkv_ref, nq_ref, nk_ref, nv_ref) = refs
        tables = (rc_ref[...], rm_ref[...], rp_ref[...])
    else:
        (x_ref, shift_ref, scale_ref, gain_ref, w_ref, gqg_ref, nag_ref, bd_ref,
         gq_ref, gkv_ref, nq_ref, nk_ref, nv_ref) = refs
        tables = None
    hb = _norm_mod(x_ref[...], gain_ref[...], shift_ref[...], scale_ref[...]).astype(BF16)
    bd = bd_ref[...]

    def slab(acc, s, gain, use_rope, mult):
        y = _seg_rms(acc[:, s * LANES:(s + 1) * LANES], gain, bd)
        if use_rope and tables is not None:
            y = _rope(y, *tables)
        return (y * mult).astype(BF16)

    q_w = GQA_HEADS * HEAD_DIM
    acc = _dot(hb, w_ref[:, 0:q_w])
    for s in range(q_w // LANES):
        gq_ref[:, s * LANES:(s + 1) * LANES] = slab(acc, s, gqg_ref[0:1, :], True, ATTN_SCALE)
    acc = _dot(hb, w_ref[:, q_w:q_w + 2 * LANES])
    gkv_ref[:, 0:LANES] = slab(acc, 0, gqg_ref[1:2, :], True, 1.0)
    gkv_ref[:, LANES:2 * LANES] = acc[:, LANES:2 * LANES].astype(BF16)
    base = q_w + 2 * LANES
    na_w = NA_HEADS * HEAD_DIM
    acc = _dot(hb, w_ref[:, base:base + na_w])
    for s in range(na_w // LANES):
        nq_ref[:, s * LANES:(s + 1) * LANES] = slab(acc, s, nag_ref[0:1, :], False, ATTN_SCALE)
    acc = _dot(hb, w_ref[:, base + na_w:base + 2 * na_w])
    for s in range(na_w // LANES):
        nk_ref[:, s * LANES:(s + 1) * LANES] = slab(acc, s, nag_ref[1:2, :], False, 1.0)
    nv_ref[...] = _dot(hb, w_ref[:, base + 2 * na_w:base + 3 * na_w]).astype(BF16)


def _odd_proj(x, shift, scale, gain, w, gqa_gain, na_gain, rope):
    bsz, n, d = x.shape
    tm = min(PROJ_ROWS, n)
    gqg = jnp.tile(gqa_gain.astype(F32), (1, 2))
    nag = jnp.tile(na_gain.astype(F32), (1, 2))
    in_specs = [
        _row_spec(tm, d), _bcast_spec(d), _bcast_spec(d), _const_spec((1, d)), _const_spec(w.shape),
        _const_spec((2, LANES)), _const_spec((2, LANES)), _const_spec((LANES, LANES)),
    ]
    args = [x, shift, scale, gain.reshape(1, d), w, gqg, nag, _seg_mean_matrix()]
    if rope:
        tab_spec = pl.BlockSpec((tm, LANES), lambda b, i: (i, 0))
        in_specs += [tab_spec] * 3
        args += list(_rope_tables(n))
    widths = [GQA_HEADS * HEAD_DIM, 2 * LANES, NA_HEADS * HEAD_DIM, NA_HEADS * HEAD_DIM, NA_HEADS * HEAD_DIM]
    return pl.pallas_call(
        functools.partial(_odd_proj_kernel, rope=rope),
        grid=(bsz, n // tm),
        in_specs=in_specs,
        out_specs=[_row_spec(tm, wd) for wd in widths],
        out_shape=[jax.ShapeDtypeStruct((bsz, n, wd), BF16) for wd in widths],
        compiler_params=_cparams(("parallel", "parallel")),
        name="odd_proj_x" if rope else "odd_proj_ctx",
    )(*args)


def _flash_scratch(rows):
    return [pltpu.VMEM((rows, FLASH_TK), F32),
            pltpu.VMEM((rows, FLASH_TK), F32),
            pltpu.VMEM((rows, 1), F32),
            pltpu.VMEM((rows, 2 * LANES), F32)]


def _flash(qs, sources, scratch):
    s0_ref, s1_ref, m_ref, acc_ref = scratch
    s_refs = (s0_ref, s1_ref)
    rows = qs.shape[0]
    tk = s0_ref.shape[1]
    m_ref[...] = jnp.full(m_ref.shape, -jnp.inf, F32)
    acc_ref[...] = jnp.zeros(acc_ref.shape, F32)

    def issue(slot, width, k):
        s_refs[slot][:, 0:width] = _dot_nt(qs, k)

    def consume(slot, width, v):
        s = s_refs[slot][:, 0:width]
        m_old = m_ref[...]
        m_new = jnp.maximum(m_old, jnp.max(s, axis=-1, keepdims=True))
        alpha = jnp.exp(m_old - m_new)
        p = jnp.exp((s - m_new).astype(BF16))
        ones = (lax.broadcasted_iota(I32, (width, LANES), 1) == 0).astype(BF16)
        m_ref[...] = m_new
        acc_ref[...] = alpha * acc_ref[...] + _dot(p, jnp.concatenate([v, ones], axis=1))

    issued = 0
    prev = None
    for k_ref, v_ref, length in sorted(sources, key=lambda src: src[2]):
        chunk = min(tk, length)
        steps = length // chunk
        if steps == 1:
            slot = issued % 2
            issue(slot, chunk, k_ref[...])
            if prev is not None:
                consume(prev[0], prev[1], prev[2]())
            prev = (slot, chunk, lambda v_ref=v_ref: v_ref[...])
            issued += 1
            continue
        assert steps % 2 == 0 and chunk == tk
        base = issued % 2

        def kv(ref, c):
            return ref[pl.ds(pl.multiple_of(c * tk, tk), tk), :]

        issue(base, tk, kv(k_ref, 0))
        if prev is not None:
            consume(prev[0], prev[1], prev[2]())

        def body(j, carry, k_ref=k_ref, v_ref=v_ref, base=base):
            issue(1 - base, tk, kv(k_ref, 2 * j + 1))
            consume(base, tk, kv(v_ref, 2 * j))
            issue(base, tk, kv(k_ref, 2 * j + 2))
            consume(1 - base, tk, kv(v_ref, 2 * j + 1))
            return carry

        lax.fori_loop(0, steps // 2 - 1, body, 0)
        issue(1 - base, tk, kv(k_ref, steps - 1))
        consume(base, tk, kv(v_ref, steps - 2))
        prev = (1 - base, tk, lambda v_ref=v_ref, steps=steps: kv(v_ref, steps - 1))
        issued += steps
    consume(prev[0], prev[1], prev[2]())
    return acc_ref[:, 0:LANES] / acc_ref[:, LANES:LANES + 1]


def _lane_ids(shape):
    return lax.broadcasted_iota(I32, shape, len(shape) - 1)


def _diff_attn_kernel(*refs, n_src, lens, lam_init):
    q_ref = refs[0]
    kv_refs = refs[1:1 + 2 * n_src]
    lam_ref, gain_ref, o_ref = refs[1 + 2 * n_src:4 + 2 * n_src]
    scratch = refs[4 + 2 * n_src:]
    q = q_ref[...]
    tq = q.shape[0]
    lo = _lane_ids(q.shape) < HEAD_DIM
    zero = jnp.zeros_like(q)
    qs = jnp.concatenate([jnp.where(lo, q, zero), jnp.where(lo, zero, q)], axis=0)
    sources = [(kv_refs[2 * i], kv_refs[2 * i + 1], lens[i]) for i in range(n_src)]
    a = _flash(qs, sources, scratch)
    lp = lam_ref[...]
    lam = (jnp.exp(jnp.sum(lp[0:1, :] * lp[1:2, :], axis=-1, keepdims=True))
           - jnp.exp(jnp.sum(lp[2:3, :] * lp[3:4, :], axis=-1, keepdims=True)) + lam_init)
    o = a[0:tq, :] - lam * a[tq:2 * tq, :]
    ms = jnp.mean(o * o, axis=-1, keepdims=True)
    o = o * lax.rsqrt(ms + EPS) * gain_ref[...] * (1.0 - lam_init)
    o_ref[...] = o.astype(BF16)


def _diff_attention(q, kv_list, lam_params, out_gain, lam_init):
    bsz, n, width = q.shape
    tq = min(DIFF_TQ, n)
    lens = tuple(k.shape[1] for k, _ in kv_list)
    in_specs = [pl.BlockSpec((None, tq, LANES), lambda b, h, i: (b, i, h))]
    args = [q]
    for (k, v), length in zip(kv_list, lens):
        spec = pl.BlockSpec((None, length, LANES), lambda b, h, i: (b, 0, h))
        in_specs += [spec, spec]
        args += [k, v]
    in_specs += [pl.BlockSpec(lam_params.shape, lambda b, h, i: (0, 0)),
                 pl.BlockSpec((1, LANES), lambda b, h, i: (0, 0))]
    args += [lam_params.astype(F32), out_gain.reshape(1, LANES).astype(F32)]
    return pl.pallas_call(
        functools.partial(_diff_attn_kernel, n_src=len(kv_list), lens=lens, lam_init=lam_init),
        grid=(bsz, DIFF_HEADS, n // tq),
        in_specs=in_specs,
        out_specs=pl.BlockSpec((None, tq, LANES), lambda b, h, i: (b, i, h)),
        out_shape=jax.ShapeDtypeStruct((bsz, n, width), BF16),
        scratch_shapes=_flash_scratch(2 * tq),
        compiler_params=_cparams(("parallel", "parallel", "parallel")),
        name="diff_attn",
    )(*args)


def _gqa_kernel(q_ref, kc_ref, vc_ref, kx_ref, vx_ref, o_ref, *scratch, lens):
    tq = q_ref.shape[0]
    sources = [(kc_ref, vc_ref, lens[0]), (kx_ref, vx_ref, lens[1])]
    lanes = _lane_ids((tq, LANES))
    for kv in range(GQA_KV_HEADS):
        mine = (lanes // HEAD_DIM) == kv
        rows = []
        for half in range(2):
            hh = q_ref[:, (2 * kv + half) * LANES:(2 * kv + half + 1) * LANES]
            sw = pltpu.roll(hh.astype(F32), HEAD_DIM, 1).astype(BF16)
            zero = jnp.zeros_like(hh)
            a_here, b_here = (hh, sw) if kv == 0 else (sw, hh)
            rows += [jnp.where(mine, a_here, zero), jnp.where(mine, b_here, zero)]
        qs = jnp.concatenate(rows, axis=0)
        o = _flash(qs, sources, scratch)
        for half in range(2):
            oa = o[(2 * half) * tq:(2 * half + 1) * tq, :]
            ob = o[(2 * half + 1) * tq:(2 * half + 2) * tq, :]
            oa_sw = pltpu.roll(oa, HEAD_DIM, 1)
            ob_sw = pltpu.roll(ob, HEAD_DIM, 1)
            if kv == 0:
                res = jnp.where(lanes < HEAD_DIM, oa, ob_sw)
            else:
                res = jnp.where(lanes < HEAD_DIM, oa_sw, ob)
            o_ref[:, (2 * kv + half) * LANES:(2 * kv + half + 1) * LANES] = res.astype(BF16)


def _gqa_attention(q, kv_c, kv_x):
    bsz, n, width = q.shape
    tq = min(GQA_TQ, n)
    lens = (kv_c.shape[1], kv_x.shape[1])

    def kspec(length, col):
        return pl.BlockSpec((None, length, LANES), lambda b, i, col=col: (b, 0, col))

    return pl.pallas_call(
        functools.partial(_gqa_kernel, lens=lens),
        grid=(bsz, n // tq),
        in_specs=[pl.BlockSpec((None, tq, width), lambda b, i: (b, i, 0)),
                  kspec(lens[0], 0), kspec(lens[0], 1), kspec(lens[1], 0), kspec(lens[1], 1)],
        out_specs=pl.BlockSpec((None, tq, width), lambda b, i: (b, i, 0)),
        out_shape=jax.ShapeDtypeStruct((bsz, n, width), BF16),
        scratch_shapes=_flash_scratch(4 * tq),
        compiler_params=_cparams(("parallel", "parallel")),
        name="gqa_attn",
    )(q, kv_c, kv_c, kv_x, kv_x)


def _gla_chunks(chains, lt, ut):
    c, s16 = HGRN_CHUNK, HGRN_SUB
    nsub = c // s16
    ridx = lax.broadcasted_iota(I32, (c, 1), 0)
    sidx = lax.broadcasted_iota(I32, (s16, 1), 0)
    cums = [jnp.dot(ut if rev else lt, g, precision=HIGHEST, preferred_element_type=F32)
            for _, _, g, _, _, rev in chains]
    tots = [cum[0:1, :] if ch[5] else cum[c - 1:c, :] for ch, cum in zip(chains, cums)]
    vbs = [ch[3].astype(BF16) for ch in chains]
    out_states = [_dot_nt((ch[0] * jnp.exp(cum)).astype(BF16), ch[4].astype(BF16))
                  for ch, cum in zip(chains, cums)]
    new_sts = [ch[4] * jnp.exp(tot) + _dot_tn(vb, (ch[1] * jnp.exp(tot - cum)).astype(BF16))
               for ch, cum, tot, vb in zip(chains, cums, tots, vbs)]
    blocks = [[None] * nsub for _ in chains]
    for i in range(nsub):
        starts = [c - s16 * (i + 1) if ch[5] else s16 * i for ch in chains]
        accs = [jnp.zeros((s16, ch[3].shape[1]), F32) for ch in chains]
        if i > 0:
            scores = []
            for (q, k, _, _, _, rev), cum, r0 in zip(chains, cums, starts):
                if rev:
                    bnd = cum[r0 + s16:r0 + s16 + 1, :]
                    prev = ridx >= r0 + s16
                else:
                    bnd = cum[r0 - 1:r0, :]
                    prev = ridx < r0
                qt = (q[r0:r0 + s16] * jnp.exp(cum[r0:r0 + s16] - bnd)).astype(BF16)
                kt = (k * jnp.exp(jnp.where(prev, bnd - cum, -jnp.inf))).astype(BF16)
                scores.append(_dot_nt(qt, kt).astype(BF16))
            accs = [_dot(a, vb) for a, vb in zip(scores, vbs)]
        for s in range(s16):
            for ci, ((q, k, _, v, _, rev), cum, r0) in enumerate(zip(chains, cums, starts)):
                cum_i = cum[r0:r0 + s16]
                valid = (sidx <= s) if rev else (sidx >= s)
                d = jnp.where(valid, cum_i - cum_i[s:s + 1, :], -jnp.inf)
                a = jnp.sum(q[r0:r0 + s16] * k[r0 + s:r0 + s + 1, :] * jnp.exp(d), axis=-1, keepdims=True)
                accs[ci] = accs[ci] + a * v[r0 + s:r0 + s + 1, :]
        for ci, ch in enumerate(chains):
            blocks[ci][nsub - 1 - i if ch[5] else i] = accs[ci]
    return [(o + jnp.concatenate(b, axis=0), st) for o, b, st in zip(out_states, blocks, new_sts)]


def _hgrn_kernel(qx_ref, kfx_ref, gfx_ref, kbx_ref, gbx_ref, vx_ref, sgx_ref,
                 qc_ref, kfc_ref, gfc_ref, kbc_ref, gbc_ref, vc_ref, sgc_ref,
                 gain_ref, lt_ref, ut_ref, ox_ref, oc_ref, accx_ref, accc_ref):
    c = HGRN_CHUNK
    lt, ut = lt_ref[...], ut_ref[...]
    heads = qx_ref.shape[1] // LANES

    def sweep(q_ref, kf_ref, gf_ref, kb_ref, gb_ref, v_ref, acc_ref, states):
        nchunks = q_ref.shape[0] // c
        acc_ref[...] = jnp.zeros(acc_ref.shape, F32)

        def body(i, carry):
            rf = pl.multiple_of(i * c, c)
            rb = pl.multiple_of((nchunks - 1 - i) * c, c)
            chains = []
            for h in range(heads):
                st_f, st_b = carry[h]
                cols = slice(h * LANES, (h + 1) * LANES)

                def rows(ref, r0, cols=cols):
                    return ref[pl.ds(r0, c), cols].astype(F32)

                chains.append((rows(q_ref, rf), rows(kf_ref, rf), rows(gf_ref, rf), rows(v_ref, rf), st_f, False))
                chains.append((rows(q_ref, rb), rows(kb_ref, rb), rows(gb_ref, rb), rows(v_ref, rb), st_b, True))
            res = _gla_chunks(chains, lt, ut)
            for h in range(heads):
                cols = slice(h * LANES, (h + 1) * LANES)
                acc_ref[pl.ds(rf, c), cols] += res[2 * h][0]
                acc_ref[pl.ds(rb, c), cols] += res[2 * h + 1][0]
            return tuple((res[2 * h][1], res[2 * h + 1][1]) for h in range(heads))

        return lax.fori_loop(0, nchunks, body, states)

    zero = jnp.zeros((LANES, LANES), F32)
    states = sweep(qc_ref, kfc_ref, gfc_ref, kbc_ref, gbc_ref, vc_ref, accc_ref,
                   tuple((zero, zero) for _ in range(heads)))
    sweep(qx_ref, kfx_ref, gfx_ref, kbx_ref, gbx_ref, vx_ref, accx_ref, states)

    def finish(acc_ref, sg_ref, o_ref):
        rows = acc_ref.shape[0]
        tile = min(rows, 512)

        def body(i, _):
            r = pl.multiple_of(i * tile, tile)
            for h in range(heads):
                cols = slice(h * LANES, (h + 1) * LANES)
                o = acc_ref[pl.ds(r, tile), cols]
                ms = jnp.mean(o * o, axis=-1, keepdims=True)
                o = o * lax.rsqrt(ms + EPS) * gain_ref[...]
                o_ref[pl.ds(r, tile), cols] = (o * sg_ref[pl.ds(r, tile), cols].astype(F32)).astype(BF16)
            return 0

        lax.fori_loop(0, rows // tile, body, 0)

    finish(accx_ref, sgx_ref, ox_ref)
    finish(accc_ref, sgc_ref, oc_ref)


def _hgrn(px, pc, out_gain):
    bsz, n, width = px[0].shape
    m = pc[0].shape[1]
    c = HGRN_CHUNK
    r = jnp.arange(c)
    lt = (r[:, None] >= r[None, :]).astype(F32)
    ut = (r[:, None] <= r[None, :]).astype(F32)

    bw = HGRN_HEADS_PER_STEP * LANES

    def spec(length):
        return pl.BlockSpec((None, length, bw), lambda b, h: (b, 0, h))

    const = lambda shape: pl.BlockSpec(shape, lambda b, h: (0, 0))
    return pl.pallas_call(
        _hgrn_kernel,
        grid=(bsz, width // bw),
        in_specs=[spec(n)] * 7 + [spec(m)] * 7 + [const((1, LANES)), const((c, c)), const((c, c))],
        out_specs=[spec(n), spec(m)],
        out_shape=[jax.ShapeDtypeStruct((bsz, n, width), BF16), jax.ShapeDtypeStruct((bsz, m, width), BF16)],
        scratch_shapes=[pltpu.VMEM((n, bw), F32), pltpu.VMEM((m, bw), F32)],
        compiler_params=_cparams(("parallel", "parallel")),
        name="hgrn2",
    )(*px, *pc, out_gain.reshape(1, LANES).astype(F32), lt, ut)


def _na_bias_tables(rpb, rows):
    qrows, band, heads = NA_QROWS, NA_BAND, rpb.shape[0]
    pad = GRID_W - NA_COLS
    wide = jnp.pad(rpb.astype(F32), ((0, 0), (0, 0), (pad, pad)))
    toeplitz = jnp.stack([wide[:, :, GRID_W - 1 - qc:2 * GRID_W - 1 - qc] for qc in range(GRID_W)], axis=2)
    qc = np.arange(GRID_W)[:, None]
    kc = np.arange(GRID_W)[None, :]
    cstart = np.clip(qc - NA_COLS // 2, 0, GRID_W - NA_COLS)
    col_ok = (kc >= cstart) & (kc < cstart + NA_COLS)
    toeplitz = jnp.where(col_ok[None, None], toeplitz, NEG_BIG)
    tabs = []
    for r0 in (0, qrows, rows - qrows):
        rs = min(max(r0 - NA_ROWS // 2, 0), rows - band)
        qr = r0 + np.arange(qrows)[:, None]
        kr = rs + np.arange(band)[None, :]
        rstart = np.clip(qr - NA_ROWS // 2, 0, rows - NA_ROWS)
        row_ok = (kr >= rstart) & (kr < rstart + NA_ROWS)
        dr = np.clip(kr - qr + NA_ROWS - 1, 0, 2 * NA_ROWS - 2)
        tiles = jnp.take(toeplitz, jnp.asarray(dr.reshape(-1), I32), axis=1)
        tiles = jnp.where(row_ok.reshape(1, -1, 1, 1), tiles, NEG_BIG)
        tiles = tiles.reshape(heads, qrows, band, GRID_W, GRID_W).transpose(0, 1, 3, 2, 4)
        tabs.append(tiles.reshape(heads, qrows * GRID_W, band * GRID_W))
    return jnp.stack(tabs)


def _na_kernel(q_ref, k_ref, v_ref, kc_ref, vc_ref, bias_ref, o_ref, *, rows):
    j = pl.program_id(2)
    tq = q_ref.shape[0]
    band = NA_BAND * GRID_W
    rs = jnp.clip(j * NA_QROWS - NA_ROWS // 2, 0, rows - NA_BAND)
    start = pl.multiple_of(rs * GRID_W, NA_ROWS // 2 * GRID_W)
    q = q_ref[...]
    lo = _lane_ids(q.shape) < HEAD_DIM
    zero = jnp.zeros_like(q)
    qs = jnp.concatenate([jnp.where(lo, q, zero), jnp.where(lo, zero, q)], axis=0)
    kb = k_ref[pl.ds(start, band), :]
    vb = v_ref[pl.ds(start, band), :]
    s_win = _dot_nt(qs, kb) + jnp.concatenate([bias_ref[0], bias_ref[1]], axis=0)
    s_ctx = _dot_nt(qs, kc_ref[...])
    m = jnp.maximum(jnp.max(s_win, axis=-1, keepdims=True), jnp.max(s_ctx, axis=-1, keepdims=True))
    p_win = jnp.exp(s_win - m)
    p_ctx = jnp.exp(s_ctx - m)
    l = jnp.sum(p_win, axis=-1, keepdims=True) + jnp.sum(p_ctx, axis=-1, keepdims=True)
    o = (_dot(p_ctx.astype(BF16), vc_ref[...]) + _dot(p_win.astype(BF16), vb)) / l
    o_ref[...] = jnp.where(lo, o[0:tq, :], o[tq:2 * tq, :]).astype(BF16)


def _na_attention(q, k, v, kc, vc, rpb):
    bsz, n, width = q.shape
    rows = n // GRID_W
    tq = NA_QROWS * GRID_W
    nt = n // tq
    bias = _na_bias_tables(rpb, rows)
    m = kc.shape[1]

    def cls(j):
        return jnp.where(j == 0, 0, jnp.where(j == nt - 1, 2, 1))

    full = lambda length: pl.BlockSpec((None, length, LANES), lambda b, h, j: (b, 0, h))
    return pl.pallas_call(
        functools.partial(_na_kernel, rows=rows),
        grid=(bsz, NA_HEADS // 2, nt),
        in_specs=[pl.BlockSpec((None, tq, LANES), lambda b, h, j: (b, j, h)),
                  full(n), full(n), full(m), full(m),
                  pl.BlockSpec((None, 2, tq, NA_BAND * GRID_W), lambda b, h, j: (cls(j), h, 0, 0))],
        out_specs=pl.BlockSpec((None, tq, LANES), lambda b, h, j: (b, j, h)),
        out_shape=jax.ShapeDtypeStruct((bsz, n, width), BF16),
        compiler_params=_cparams(("parallel", "parallel", "parallel")),
        name="na_attn",
    )(q, k, v, kc, vc, bias)


def _out_proj_kernel(a_ref, b_ref, w_ref, x_ref, gate_ref, gain_ref, shift_ref, scale_ref, r_ref,
                     x1_ref, tok_ref, logit_ref):
    half = a_ref.shape[1]
    y = _dot(a_ref[...], w_ref[0:half, :]) + _dot(b_ref[...], w_ref[half:2 * half, :])
    x1 = x_ref[...] + gate_ref[...] * y
    x1_ref[...] = x1
    h = _norm_mod(x1, gain_ref[...], shift_ref[...], scale_ref[...])
    logit_ref[...] = jnp.dot(h, r_ref[...], precision=HIGHEST, preferred_element_type=F32)
    tok_ref[...] = _pack_pair(h[:, 0:HALF], h[:, HALF:2 * HALF])


def _out_proj(a, b, w, x, gate, gain, shift, scale, router_pad):
    bsz, n, d = x.shape
    tm = min(PROJ_ROWS, n)
    return pl.pallas_call(
        _out_proj_kernel,
        grid=(bsz, n // tm),
        in_specs=[_row_spec(tm, a.shape[2]), _row_spec(tm, b.shape[2]), _const_spec(w.shape), _row_spec(tm, d),
                  _bcast_spec(d), _const_spec((1, d)), _bcast_spec(d), _bcast_spec(d), _const_spec(router_pad.shape)],
        out_specs=[_row_spec(tm, d), _row_spec(tm, HALF), _row_spec(tm, LANES)],
        out_shape=[jax.ShapeDtypeStruct((bsz, n, d), F32), jax.ShapeDtypeStruct((bsz, n, HALF), I32),
                   jax.ShapeDtypeStruct((bsz, n, LANES), F32)],
        compiler_params=_cparams(("parallel", "parallel")),
        name="out_proj",
    )(a, b, w, x, gate, gain.reshape(1, d), shift, scale, router_pad)


def _route_kernel(logit_ref, bias_ref, tri_ref, etri_ref, w_ref, loc_ref, tcnt_ref, tcarry_ref, toff_ref, cnt_ref,
                  masked_ref, carry_ref):
    step = pl.program_id(0)

    @pl.when(step == 0)
    def _():
        carry_ref[...] = jnp.zeros_like(carry_ref)

    tr = logit_ref.shape[0]
    scores = jax.nn.sigmoid(logit_ref[...].T[0:N_EXPERTS, :])
    biased = scores + bias_ref[...]
    gsz = EXPERTS_PER_GROUP
    sub = lax.broadcasted_iota(I32, (gsz, tr), 0).astype(F32)
    gscore = []
    for g in range(N_GROUPS):
        bg = biased[g * gsz:(g + 1) * gsz, :]
        m1 = jnp.max(bg, axis=0, keepdims=True)
        i1 = jnp.min(jnp.where(bg == m1, sub, float(gsz)), axis=0, keepdims=True)
        m2 = jnp.max(jnp.where(sub == i1, -jnp.inf, bg), axis=0, keepdims=True)
        gscore.append(m1 + m2)
    for g in range(N_GROUPS):
        beaten = jnp.zeros((1, tr), F32)
        for o in range(N_GROUPS):
            if o == g:
                continue
            wins = (gscore[o] >= gscore[g]) if o < g else (gscore[o] > gscore[g])
            beaten = beaten + jnp.where(wins, 1.0, 0.0)
        keep = beaten < float(TOPK_GROUPS)
        masked_ref[g * gsz:(g + 1) * gsz, :] = jnp.where(keep, biased[g * gsz:(g + 1) * gsz, :], -jnp.inf)
    cur = masked_ref[...]
    eid = lax.broadcasted_iota(I32, (N_EXPERTS, tr), 0).astype(F32)
    sel = jnp.zeros((N_EXPERTS, tr), F32)
    picks, weights = [], []
    for _ in range(TOP_K):
        m = jnp.max(cur, axis=0, keepdims=True)
        ik = jnp.min(jnp.where(cur == m, eid, float(N_EXPERTS)), axis=0, keepdims=True)
        hit = eid == ik
        weights.append(jnp.sum(jnp.where(hit, scores, 0.0), axis=0, keepdims=True))
        sel = sel + jnp.where(hit, 1.0, 0.0)
        cur = jnp.where(hit, -jnp.inf, cur)
        picks.append(ik)
    wsum = weights[0]
    for wk in weights[1:]:
        wsum = wsum + wk
    tile_cnt = jnp.broadcast_to(jnp.sum(sel, axis=1, keepdims=True), (N_EXPERTS, LANES))
    tile_cnt = jnp.floor((tile_cnt + (RUN_ALIGN - 1)) * (1.0 / RUN_ALIGN)) * RUN_ALIGN
    tile_off = jnp.dot(etri_ref[...], tile_cnt, precision=HIGHEST, preferred_element_type=F32)
    row = _dot(sel.astype(BF16), tri_ref[...]) + tile_off[:, 0:1]
    for kk in range(TOP_K):
        w_ref[kk:kk + 1, :] = weights[kk] / wsum * ROUTED_SCALE
        loc_ref[kk:kk + 1, :] = jnp.sum(jnp.where(eid == picks[kk], row, 0.0), axis=0, keepdims=True).astype(I32)
    tcnt_ref[...] = tile_cnt.astype(I32)
    tcarry_ref[...] = carry_ref[...].astype(I32)
    toff_ref[...] = tile_off.astype(I32)
    carry_ref[...] = carry_ref[...] + tile_cnt
    cnt_ref[...] = carry_ref[...].astype(I32)


def _route(logits, router_bias):
    t = logits.shape[0]
    tr = MOE_TILE
    assert t % tr == 0
    nt = t // tr
    r = jnp.arange(tr)
    tri = (r[:, None] < r[None, :]).astype(BF16)
    e = jnp.arange(N_EXPERTS)
    etri = (e[:, None] > e[None, :]).astype(F32)
    kt_spec = pl.BlockSpec((TOP_K, tr), lambda i: (0, i))
    tile_spec = pl.BlockSpec((None, N_EXPERTS, LANES), lambda i: (i, 0, 0))
    tile_shape = jax.ShapeDtypeStruct((nt, N_EXPERTS, LANES), I32)
    return pl.pallas_call(
        _route_kernel,
        grid=(nt,),
        in_specs=[pl.BlockSpec((tr, LANES), lambda i: (i, 0)),
                  pl.BlockSpec((N_EXPERTS, 1), lambda i: (0, 0)),
                  pl.BlockSpec((tr, tr), lambda i: (0, 0)),
                  pl.BlockSpec((N_EXPERTS, N_EXPERTS), lambda i: (0, 0))],
        out_specs=[kt_spec, kt_spec, tile_spec, tile_spec, tile_spec,
                   pl.BlockSpec((N_EXPERTS, LANES), lambda i: (0, 0))],
        out_shape=[jax.ShapeDtypeStruct((TOP_K, t), F32), jax.ShapeDtypeStruct((TOP_K, t), I32),
                   tile_shape, tile_shape, tile_shape, jax.ShapeDtypeStruct((N_EXPERTS, LANES), I32)],
        scratch_shapes=[pltpu.VMEM((N_EXPERTS, tr), F32), pltpu.VMEM((N_EXPERTS, LANES), F32)],
        compiler_params=_cparams(("arbitrary",)),
        name="moe_route",
    )(logits, router_bias.astype(F32).reshape(N_EXPERTS, 1), tri, etri)


TAB_WORDS = 1024
TAB_FIELD = 128
N_RUNS = N_EXPERTS + 1
TILE_ROWS = -(-(MOE_TILE * TOP_K + N_EXPERTS * (RUN_ALIGN - 1)) // MOE_TILE) * MOE_TILE
FILLER_ROWS = TILE_ROWS - MOE_TILE * TOP_K
RUN_PIECE = 64


def _run_copies(tab_ref, local_ref, global_ref, sem, to_global):
    def piece(ls, gs, off, rows, priority):
        loc = local_ref.at[pl.ds(pl.multiple_of(ls + off, RUN_ALIGN), rows)]
        glo = global_ref.at[pl.ds(pl.multiple_of(gs + off, RUN_ALIGN), rows)]
        src, dst = (loc, glo) if to_global else (glo, loc)
        pltpu.make_async_copy(src, dst, sem).start(priority=priority)

    def body(e, carry):
        gs = tab_ref[e]
        c = tab_ref[TAB_FIELD + e]
        ls = tab_ref[2 * TAB_FIELD + e]
        whole = c // RUN_PIECE

        def big(j, carry2):
            piece(ls, gs, j * RUN_PIECE, RUN_PIECE, 0)
            return carry2

        lax.fori_loop(0, whole, big, 0)
        for b in range(RUN_ALIGN.bit_length() - 1, RUN_PIECE.bit_length() - 1):
            @pl.when(((c >> b) & 1) == 1)
            def _(b=b):
                piece(ls, gs, whole * RUN_PIECE + (c & ((1 << b) - 1) & (RUN_PIECE - 1)), 1 << b, 1)
        return carry

    lax.fori_loop(0, N_RUNS, body, 0)


def _run_wait(local_ref, global_ref, sem):
    pltpu.make_async_copy(global_ref.at[pl.ds(0, TILE_ROWS)], local_ref, sem).wait()


def _dispatch_kernel(tab_ref, loc_ref, tok_ref, rows_in_ref, rows_ref, srt_ref, sem, *, n_tiles):
    del rows_in_ref
    i = pl.program_id(0)
    slot = i % 2
    tt = tok_ref.shape[0]
    lo, hi = _unpack_pair(tok_ref[...])
    lo, hi = lo.astype(BF16), hi.astype(BF16)
    loc = loc_ref[...]
    for rb in range(TILE_ROWS // tt):
        r = rb * tt + lax.broadcasted_iota(I32, (tt, tt), 0)
        p = jnp.zeros((tt, tt), F32)
        for kk in range(TOP_K):
            p = jnp.where(loc[kk:kk + 1, :] == r, 1.0, p)
        pb = p.astype(BF16)
        srt_ref[slot, rb * tt:(rb + 1) * tt, :] = _pack_exact_pair(_dot(pb, lo), _dot(pb, hi))

    @pl.when(i > 0)
    def _():
        _run_wait(srt_ref.at[1 - slot], rows_ref, sem.at[1 - slot])

    _run_copies(tab_ref, srt_ref.at[slot], rows_ref, sem.at[slot], True)

    @pl.when(i == n_tiles - 1)
    def _():
        _run_wait(srt_ref.at[slot], rows_ref, sem.at[slot])


def _dispatch(tab, loc, tok, rows_buf):
    t = tok.shape[0]
    tt = MOE_TILE
    return pl.pallas_call(
        functools.partial(_dispatch_kernel, n_tiles=t // tt),
        grid=(t // tt,),
        in_specs=[pl.BlockSpec((TAB_WORDS,), lambda i: (i,), memory_space=pltpu.SMEM),
                  pl.BlockSpec((TOP_K, tt), lambda i: (0, i)),
                  pl.BlockSpec((tt, HALF), lambda i: (i, 0)),
                  pl.BlockSpec(memory_space=pl.ANY)],
        out_specs=pl.BlockSpec(memory_space=pl.ANY),
        out_shape=jax.ShapeDtypeStruct(rows_buf.shape, rows_buf.dtype),
        scratch_shapes=[pltpu.VMEM((2, TILE_ROWS, HALF), I32), pltpu.SemaphoreType.DMA((2,))],
        input_output_aliases={3: 0},
        compiler_params=_cparams(("arbitrary",)),
        name="moe_dispatch",
    )(tab, loc, tok, rows_buf)


def _expert_kernel(be_ref, nused_ref, x_ref, wgu_ref, wd_ref, y_ref):
    @pl.when(pl.program_id(0) < nused_ref[0])
    def _():
        lo, hi = _unpack_pair(x_ref[...])
        gu = _dot(lo.astype(BF16), wgu_ref[0:HALF, :]) + _dot(hi.astype(BF16), wgu_ref[HALF:2 * HALF, :])
        h = (_silu(gu[:, 0:EXPERT_DIM]) * gu[:, EXPERT_DIM:2 * EXPERT_DIM]).astype(BF16)
        y = _dot(h, wd_ref[...])
        y_ref[...] = _pack_pair(y[:, 0:HALF], y[:, HALF:2 * HALF])

    @pl.when(pl.program_id(0) >= nused_ref[0])
    def _():
        y_ref[...] = jnp.zeros_like(y_ref)


def _experts(block_expert, nused, rows, wgu, wd):
    n_rows = rows.shape[0]
    nb = n_rows // MOE_BLOCK

    def row_map(i, be, nu):
        return (jnp.minimum(i, nu[0] - 1), 0)

    def w_map(i, be, nu):
        return (be[jnp.minimum(i, nu[0] - 1)], 0, 0)

    grid_spec = pltpu.PrefetchScalarGridSpec(
        num_scalar_prefetch=2,
        grid=(nb,),
        in_specs=[pl.BlockSpec((MOE_BLOCK, HALF), row_map),
                  pl.BlockSpec((None, D_MODEL, 2 * EXPERT_DIM), w_map),
                  pl.BlockSpec((None, EXPERT_DIM, D_MODEL), w_map)],
        out_specs=pl.BlockSpec((MOE_BLOCK, HALF), lambda i, be, nu: (i, 0)),
    )
    return pl.pallas_call(
        _expert_kernel,
        grid_spec=grid_spec,
        out_shape=jax.ShapeDtypeStruct((n_rows, HALF), I32),
        compiler_params=_cparams(("arbitrary",)),
        name="moe_experts",
    )(block_expert, nused, rows, wgu, wd)


def _combine_kernel(tab_ref, tabn_ref, x1_ref, tok_ref, loc_ref, w_ref, gate_ref, wgu_ref, wd_ref, y_ref, o_ref,
                    buf_ref, sem, *, n_tiles):
    i = pl.program_id(0)
    slot = i % 2
    tt = tok_ref.shape[0]

    @pl.when(i == 0)
    def _():
        _run_copies(tab_ref, buf_ref.at[slot], y_ref, sem.at[slot], False)

    @pl.when(i + 1 < n_tiles)
    def _():
        _run_copies(tabn_ref, buf_ref.at[1 - slot], y_ref, sem.at[1 - slot], False)

    lo, hi = _unpack_pair(tok_ref[...])
    gu = _dot(lo.astype(BF16), wgu_ref[0:HALF, :]) + _dot(hi.astype(BF16), wgu_ref[HALF:2 * HALF, :])
    h = (_silu(gu[:, 0:EXPERT_DIM]) * gu[:, EXPERT_DIM:2 * EXPERT_DIM]).astype(BF16)
    shared = _dot(h, wd_ref[...])
    _run_wait(buf_ref.at[slot], y_ref, sem.at[slot])
    acc_lo = shared[:, 0:HALF]
    acc_hi = shared[:, HALF:2 * HALF]
    loc = loc_ref[...]
    w = w_ref[...]
    for rb in range(TILE_ROWS // tt):
        r = rb * tt + lax.broadcasted_iota(I32, (tt, tt), 1)
        wm = jnp.zeros((tt, tt), F32)
        for kk in range(TOP_K):
            wm = jnp.where(loc[:, kk:kk + 1] == r, w[:, kk:kk + 1], wm)
        wb = wm.astype(BF16)
        ylo, yhi = _unpack_pair(buf_ref[slot, rb * tt:(rb + 1) * tt, :])
        acc_lo = acc_lo + _dot(wb, ylo.astype(BF16))
        acc_hi = acc_hi + _dot(wb, yhi.astype(BF16))
    gate = gate_ref[...]
    o_ref[:, 0:HALF] = x1_ref[:, 0:HALF] + gate[:, 0:HALF] * acc_lo
    o_ref[:, HALF:2 * HALF] = x1_ref[:, HALF:2 * HALF] + gate[:, HALF:2 * HALF] * acc_hi


def _combine(tab, x1, tok, loc_tok, w_tok, gate, wgu, wd, y_rows, tokens_per_gate):
    t, d = x1.shape
    tt = MOE_TILE
    nt = t // tt
    per = tokens_per_gate // tt
    return pl.pallas_call(
        functools.partial(_combine_kernel, n_tiles=nt),
        grid=(nt,),
        in_specs=[pl.BlockSpec((TAB_WORDS,), lambda i: (i,), memory_space=pltpu.SMEM),
                  pl.BlockSpec((TAB_WORDS,), lambda i: (jnp.minimum(i + 1, nt - 1),), memory_space=pltpu.SMEM),
                  pl.BlockSpec((tt, d), lambda i: (i, 0)),
                  pl.BlockSpec((tt, HALF), lambda i: (i, 0)),
                  pl.BlockSpec((tt, TOP_K), lambda i: (i, 0)),
                  pl.BlockSpec((tt, TOP_K), lambda i: (i, 0)),
                  pl.BlockSpec((None, 1, d), lambda i: (i // per, 0, 0)),
                  pl.BlockSpec(wgu.shape, lambda i: (0, 0)),
                  pl.BlockSpec(wd.shape, lambda i: (0, 0)),
                  pl.BlockSpec(memory_space=pl.ANY)],
        out_specs=pl.BlockSpec((tt, d), lambda i: (i, 0)),
        out_shape=jax.ShapeDtypeStruct((t, d), F32),
        scratch_shapes=[pltpu.VMEM((2, TILE_ROWS, HALF), I32), pltpu.SemaphoreType.DMA((2,))],
        compiler_params=_cparams(("arbitrary",)),
        name="moe_combine",
    )(tab, tab, x1, tok, loc_tok, w_tok, gate, wgu, wd, y_rows)


def _moe(parts, router_bias, w_gate, w_up, w_down, ws_gate, ws_up, ws_down):
    logits = jnp.concatenate([p[2].reshape(-1, LANES) for p in parts], axis=0)
    t = logits.shape[0]
    w, loc, tile_cnt, tile_carry, tile_off, cnt = _route(logits, router_bias)
    counts = cnt[:, 0]
    padded = (counts + MOE_BLOCK - 1) // MOE_BLOCK * MOE_BLOCK
    pad_end = jnp.cumsum(padded)
    pad_start = pad_end - padded
    nt = t // MOE_TILE
    max_aligned = t * TOP_K + nt * N_EXPERTS * (RUN_ALIGN - 1)
    area_rows = -(-max_aligned // MOE_BLOCK) * MOE_BLOCK + N_EXPERTS * MOE_BLOCK
    n_rows = -(-(area_rows + nt * FILLER_ROWS) // MOE_BLOCK) * MOE_BLOCK
    tile_rows_used = tile_off[:, N_EXPERTS - 1, 0] + tile_cnt[:, N_EXPERTS - 1, 0]

    def field(per_expert, filler):
        vals = jnp.concatenate([per_expert.astype(I32), filler.astype(I32)[:, None]], axis=1)
        return jnp.pad(vals, ((0, 0), (0, TAB_FIELD - N_RUNS)))

    tab = jnp.concatenate(
        [field(pad_start[None, :] + tile_carry[:, :, 0], area_rows + jnp.arange(nt) * FILLER_ROWS),
         field(tile_cnt[:, :, 0], TILE_ROWS - tile_rows_used),
         field(tile_off[:, :, 0], tile_rows_used),
         jnp.zeros((nt, TAB_WORDS - 3 * TAB_FIELD), I32)], axis=1).reshape(-1)
    w_tok = w.T
    loc_tok = loc.T
    nb = n_rows // MOE_BLOCK
    block_start = jnp.arange(nb, dtype=I32) * MOE_BLOCK
    block_expert = jnp.minimum(jnp.sum(block_start[:, None] >= pad_end[None, :], axis=1), N_EXPERTS - 1).astype(I32)
    nused = (pad_end[-1] // MOE_BLOCK).astype(I32).reshape(1)
    rows = jnp.zeros((n_rows, HALF), I32)
    off = 0
    for x1, tok, _, _, _ in parts:
        cnt_tok = tok.shape[0] * tok.shape[1]
        part_tab = tab[off // MOE_TILE * TAB_WORDS:(off + cnt_tok) // MOE_TILE * TAB_WORDS]
        rows = _dispatch(part_tab, loc[:, off:off + cnt_tok], tok.reshape(cnt_tok, HALF), rows)
        off += cnt_tok
    wgu = jnp.concatenate([w_gate, w_up], axis=-1).astype(BF16)
    y_rows = _experts(block_expert, nused, rows, wgu, w_down.astype(BF16))
    wsgu = jnp.concatenate([ws_gate, ws_up], axis=-1).astype(BF16)
    wsd = ws_down.astype(BF16)
    outs = []
    off = 0
    for x1, tok, _, gate, per in parts:
        cnt_tok = tok.shape[0] * tok.shape[1]
        part_tab = tab[off // MOE_TILE * TAB_WORDS:(off + cnt_tok) // MOE_TILE * TAB_WORDS]
        o = _combine(part_tab, x1.reshape(cnt_tok, D_MODEL), tok.reshape(cnt_tok, HALF),
                     loc_tok[off:off + cnt_tok], w_tok[off:off + cnt_tok], gate, wsgu, wsd, y_rows, per)
        outs.append(o.reshape(x1.shape))
        off += cnt_tok
    return outs


def kernel(x, c, ctx, c_ctx, ada_w, ada_b, norm_mix, norm_ffn, ev_w_in, ev_w_out, diff_qk_gain, diff_lambda,
           diff_out_gain, hgrn_lb, hgrn_out_gain, od_w_in, od_w_out, gqa_qk_gain, na_qk_gain, na_rpb, moe_router,
           moe_router_bias, moe_w_gate, moe_w_up, moe_w_down, shared_w_gate, shared_w_up, shared_w_down):
    bsz, n, d = x.shape
    m = ctx.shape[1]
    depth = ada_w.shape[0]
    cond_rows = -(-(bsz + 1) // 8) * 8
    cond = jnp.zeros((cond_rows, d), F32).at[0:bsz].set(c).at[bsz].set(c_ctx)
    mods = _ada_mod(cond, ada_w, ada_b)

    xc = ctx
    for layer in range(depth):
        need_ctx = layer < depth - 1
        j = layer // 2
        mod = mods[layer].reshape(cond_rows, 6, d)
        mx = [mod[0:bsz, i][:, None, :] for i in range(6)]
        mc = [jnp.broadcast_to(mod[bsz:bsz + 1, i][:, None, :], (bsz, 1, d)) for i in range(6)]
        if layer % 2 == 0:
            w_in = ev_w_in[j].astype(BF16)
            px = _even_proj(x, mx[0], mx[1], norm_mix[layer], w_in, diff_qk_gain[j], hgrn_lb[0], hgrn_lb[1], j, True)
            pc = _even_proj(xc, mc[0], mc[1], norm_mix[layer], w_in, diff_qk_gain[j], hgrn_lb[0], hgrn_lb[1], j, False)
            lam_init = 0.8 - 0.6 * math.exp(-0.3 * layer)
            a_x = _diff_attention(px[0], [(pc[1], pc[2]), (px[1], px[2])], diff_lambda[j], diff_out_gain[j], lam_init)
            a_c = _diff_attention(pc[0], [(pc[1], pc[2])], diff_lambda[j], diff_out_gain[j], lam_init)
            b_x, b_c = _hgrn(px[3:], pc[3:], hgrn_out_gain[j])
            w_out = ev_w_out[j].astype(BF16)
        else:
            w_in = od_w_in[j].astype(BF16)
            px = _odd_proj(x, mx[0], mx[1], norm_mix[layer], w_in, gqa_qk_gain[j], na_qk_gain[j], True)
            pc = _odd_proj(xc, mc[0], mc[1], norm_mix[layer], w_in, gqa_qk_gain[j], na_qk_gain[j], False)
            a_x = _gqa_attention(px[0], pc[1], px[1])
            b_x = _na_attention(px[2], px[3], px[4], pc[3], pc[4], na_rpb[j])
            a_c = b_c = None
            w_out = od_w_out[j].astype(BF16)
        router_pad = jnp.zeros((d, LANES), F32).at[:, 0:N_EXPERTS].set(moe_router[layer].astype(F32))
        x1, tok_x, logit_x = _out_proj(a_x, b_x, w_out, x, mx[2], norm_ffn[layer], mx[3], mx[4], router_pad)
        parts = [(x1, tok_x, logit_x, mx[5], n)]
        if need_ctx:
            xc1, tok_c, logit_c = _out_proj(a_c, b_c, w_out, xc, mc[2], norm_ffn[layer], mc[3], mc[4], router_pad)
            parts.append((xc1, tok_c, logit_c, mc[5][0:1], bsz * m))
        outs = _moe(parts, moe_router_bias[layer], moe_w_gate[layer], moe_w_up[layer], moe_w_down[layer],
                    shared_w_gate[layer], shared_w_up[layer], shared_w_down[layer])
        x = outs[0]
        if need_ctx:
            xc = outs[1]
    return x
```

```python
import functools
import math

import jax
import jax.numpy as jnp
import numpy as np
from jax import lax
from jax.experimental import pallas as pl
from jax.experimental.pallas import tpu as pltpu

F32 = jnp.float32
BF16 = jnp.bfloat16
I32 = jnp.int32
HIGHEST = lax.Precision.HIGHEST

D_MODEL = 1024
GRID_W = 64
HEAD_DIM = 64
ATTN_SCALE = HEAD_DIM ** -0.5
ROPE_THETA = 10000.0
EPS = 1e-6
DIFF_HEADS = D_MODEL // 256
HGRN_HEADS = D_MODEL // 256
HGRN_CHUNK = 64
HGRN_SUB = 16
GQA_HEADS = D_MODEL // 128
GQA_KV_HEADS = GQA_HEADS // 4
NA_HEADS = D_MODEL // 128
NA_ROWS = 8
NA_COLS = 16
N_EXPERTS = 64
N_GROUPS = 8
EXPERTS_PER_GROUP = N_EXPERTS // N_GROUPS
TOPK_GROUPS = 4
TOP_K = 8
EXPERT_DIM = D_MODEL // 4
ROUTED_SCALE = 2.5
HALF = D_MODEL // 2

LANES = 128
VMEM_LIMIT_BYTES = 56 * 1024 * 1024
PROJ_ROWS = 512
FLASH_TK = 512
FLASH_SB = 64
DIFF_TQ = 512
GQA_TQ = 256
NA_QROWS = 8
NA_BAND = 16
HGRN_HEADS_PER_STEP = 2
MOE_BLOCK = 512
MOE_TILE = 256
RUN_ALIGN = 8
NEG_BIG = -1e30


def _cparams(sem):
    return pltpu.CompilerParams(dimension_semantics=sem, vmem_limit_bytes=VMEM_LIMIT_BYTES)


def _silu(x):
    return x * jax.nn.sigmoid(x)


def _dot(a, b):
    return jnp.dot(a, b, preferred_element_type=F32)


def _dot_nt(a, b):
    return lax.dot_general(a, b, (((1,), (1,)), ((), ())), preferred_element_type=F32)


def _dot_tn(a, b):
    return lax.dot_general(a, b, (((0,), (0,)), ((), ())), preferred_element_type=F32)


def _pack_pair(lo, hi):
    lo_bits = lax.bitcast_convert_type(lo.astype(BF16).astype(F32), I32)
    hi_bits = lax.bitcast_convert_type(hi.astype(BF16).astype(F32), I32)
    return lax.shift_right_logical(lo_bits, 16) | (hi_bits & jnp.int32(-65536))


def _pack_exact_pair(lo, hi):
    lo_bits = lax.bitcast_convert_type(lo, I32)
    hi_bits = lax.bitcast_convert_type(hi, I32)
    return lax.shift_right_logical(lo_bits, 16) | (hi_bits & jnp.int32(-65536))


def _unpack_pair(w):
    lo = lax.bitcast_convert_type(lax.shift_left(w, 16), F32)
    hi = lax.bitcast_convert_type(w & jnp.int32(-65536), F32)
    return lo, hi


def _ada_kernel(cond_ref, w_ref, b_ref, o_ref):
    s = _silu(cond_ref[...])
    o_ref[...] = jnp.dot(s, w_ref[...], precision=HIGHEST, preferred_element_type=F32) + b_ref[...]


def _ada_mod(cond, ada_w, ada_b):
    depth = ada_w.shape[0]
    rows = cond.shape[0]
    nblk = ada_w.shape[2] // D_MODEL
    return pl.pallas_call(
        _ada_kernel,
        grid=(depth, nblk),
        in_specs=[
            pl.BlockSpec((rows, D_MODEL), lambda l, j: (0, 0)),
            pl.BlockSpec((None, D_MODEL, D_MODEL), lambda l, j: (l, 0, j)),
            pl.BlockSpec((None, 1, D_MODEL), lambda l, j: (l, 0, j)),
        ],
        out_specs=pl.BlockSpec((None, rows, D_MODEL), lambda l, j: (l, 0, j)),
        out_shape=jax.ShapeDtypeStruct((depth, rows, nblk * D_MODEL), F32),
        compiler_params=_cparams(("parallel", "parallel")),
        name="ada_mod",
    )(cond, ada_w, ada_b.reshape(depth, 1, -1))


def _norm_mod(x, gain, shift, scale):
    ms = jnp.mean(x * x, axis=-1, keepdims=True)
    h = x * lax.rsqrt(ms + EPS) * gain
    return h * (1.0 + scale) + shift


def _seg_rms(acc, gain, bd):
    sq = acc * acc
    hi = sq.astype(BF16)
    lo = (sq - hi.astype(F32)).astype(BF16)
    ms = _dot(hi, bd) + _dot(lo, bd)
    return acc * lax.rsqrt(ms + EPS) * gain


def _rope(y, c, sm, sp):
    return y * c + pltpu.roll(y, LANES - 16, 1) * sm + pltpu.roll(y, 16, 1) * sp


def _rope_tables(n):
    pos = jnp.arange(n, dtype=I32)
    row = (pos // GRID_W).astype(F32)
    col = (pos % GRID_W).astype(F32)
    axis_dim = HEAD_DIM // 2
    inv_freq = ROPE_THETA ** (-jnp.arange(0, axis_dim, 2, dtype=F32) / axis_dim)
    ang_row = row[:, None] * inv_freq
    ang_col = col[:, None] * inv_freq
    lane = jnp.arange(LANES)
    p = lane % axis_dim
    f = p % (axis_dim // 2)
    on_row = ((lane % HEAD_DIM) // axis_dim) == 0
    ang = jnp.where(on_row[None, :], ang_row[:, f], ang_col[:, f])
    c = jnp.cos(ang)
    s = jnp.sin(ang)
    first = (p < axis_dim // 2)[None, :]
    return c, jnp.where(first, -s, 0.0), jnp.where(first, 0.0, s)


def _seg_mean_matrix():
    r = jnp.arange(LANES)
    return jnp.where((r[:, None] // HEAD_DIM) == (r[None, :] // HEAD_DIM), 1.0 / HEAD_DIM, 0.0).astype(BF16)


def _even_proj_kernel(*refs, rope, layer_slot):
    if rope:
        (x_ref, shift_ref, scale_ref, gain_ref, w_ref, qkg_ref, lbf_ref, lbb_ref, bd_ref, rc_ref, rm_ref, rp_ref,
         dq_ref, dk_ref, dv_ref, hq_ref, kf_ref, gf_ref, kb_ref, gb_ref, hv_ref, hg_ref) = refs
        tables = (rc_ref[...], rm_ref[...], rp_ref[...])
    else:
        (x_ref, shift_ref, scale_ref, gain_ref, w_ref, qkg_ref, lbf_ref, lbb_ref, bd_ref,
         dq_ref, dk_ref, dv_ref, hq_ref, kf_ref, gf_ref, kb_ref, gb_ref, hv_ref, hg_ref) = refs
        tables = None
    hb = _norm_mod(x_ref[...], gain_ref[...], shift_ref[...], scale_ref[...]).astype(BF16)
    bd = bd_ref[...]
    width = 4 * LANES

    def proj(group):
        return _dot(hb, w_ref[:, group * width:(group + 1) * width])

    def qk(group, gain, out_ref, mult):
        acc = proj(group)
        for s in range(4):
            y = _seg_rms(acc[:, s * LANES:(s + 1) * LANES], gain, bd)
            if tables is not None:
                y = _rope(y, *tables)
            out_ref[:, s * LANES:(s + 1) * LANES] = (y * mult).astype(BF16)

    qk(0, qkg_ref[0:1, :], dq_ref, ATTN_SCALE)
    qk(1, qkg_ref[1:2, :], dk_ref, 1.0)
    dv_ref[...] = proj(2).astype(BF16)
    hq_ref[...] = _silu(proj(3)).astype(BF16)

    def forget(group, lb_ref, k_ref, g_ref):
        raw = lb_ref[...]
        e = jnp.exp(raw - jnp.max(raw, axis=0, keepdims=True))
        lb = jnp.sum(e[0:layer_slot + 1, :], axis=0, keepdims=True) / jnp.sum(e, axis=0, keepdims=True)
        f = lb + (1.0 - lb) * jax.nn.sigmoid(proj(group))
        k_ref[...] = (1.0 - f).astype(BF16)
        g_ref[...] = jnp.log(f)

    forget(4, lbf_ref, kf_ref, gf_ref)
    forget(5, lbb_ref, kb_ref, gb_ref)
    hv_ref[...] = proj(6).astype(BF16)
    hg_ref[...] = _silu(proj(7)).astype(BF16)


def _row_spec(tm, width):
    return pl.BlockSpec((None, tm, width), lambda b, i: (b, i, 0))


def _bcast_spec(width):
    return pl.BlockSpec((None, 1, width), lambda b, i: (b, 0, 0))


def _const_spec(shape):
    nd = len(shape)
    return pl.BlockSpec(shape, lambda b, i: (0,) * nd)


def _even_proj(x, shift, scale, gain, w, qk_gain, lb_fwd, lb_bwd, layer_slot, rope):
    bsz, n, d = x.shape
    tm = min(PROJ_ROWS, n)
    width = 4 * LANES
    qkg = jnp.tile(qk_gain.astype(F32), (1, 2))
    in_specs = [
        _row_spec(tm, d), _bcast_spec(d), _bcast_spec(d), _const_spec((1, d)), _const_spec(w.shape),
        _const_spec((2, LANES)), _const_spec(lb_fwd.shape), _const_spec(lb_bwd.shape), _const_spec((LANES, LANES)),
    ]
    args = [x, shift, scale, gain.reshape(1, d), w, qkg, lb_fwd, lb_bwd, _seg_mean_matrix()]
    if rope:
        tab_spec = pl.BlockSpec((tm, LANES), lambda b, i: (i, 0))
        in_specs += [tab_spec] * 3
        args += list(_rope_tables(n))
    out_dtypes = [BF16, BF16, BF16, BF16, BF16, F32, BF16, F32, BF16, BF16]
    return pl.pallas_call(
        functools.partial(_even_proj_kernel, rope=rope, layer_slot=layer_slot),
        grid=(bsz, n // tm),
        in_specs=in_specs,
        out_specs=[_row_spec(tm, width)] * len(out_dtypes),
        out_shape=[jax.ShapeDtypeStruct((bsz, n, width), dt) for dt in out_dtypes],
        compiler_params=_cparams(("parallel", "parallel")),
        name="even_proj_x" if rope else "even_proj_ctx",
    )(*args)


def _odd_proj_kernel(*refs, rope):
    if rope:
        (x_ref, shift_ref, scale_ref, gain_ref, w_ref, gqg_ref, nag_ref, bd_ref, rc_ref, rm_ref, rp_ref,
         gq_ref, gkv_ref, nq_ref, nk_ref, nv_ref) = refs
        tables = (rc_ref[...], rm_ref[...], rp_ref[...])
    else:
        (x_ref, shift_ref, scale_ref, gain_ref, w_ref, gqg_ref, nag_ref, bd_ref,
         gq_ref, gkv_ref, nq_ref, nk_ref, nv_ref) = refs
        tables = None
    hb = _norm_mod(x_ref[...], gain_ref[...], shift_ref[...], scale_ref[...]).astype(BF16)
    bd = bd_ref[...]

    def slab(acc, s, gain, use_rope, mult):
        y = _seg_rms(acc[:, s * LANES:(s + 1) * LANES], gain, bd)
        if use_rope and tables is not None:
            y = _rope(y, *tables)
        return (y * mult).astype(BF16)

    q_w = GQA_HEADS * HEAD_DIM
    acc = _dot(hb, w_ref[:, 0:q_w])
    for s in range(q_w // LANES):
        gq_ref[:, s * LANES:(s + 1) * LANES] = slab(acc, s, gqg_ref[0:1, :], True, ATTN_SCALE)
    acc = _dot(hb, w_ref[:, q_w:q_w + 2 * LANES])
    gkv_ref[:, 0:LANES] = slab(acc, 0, gqg_ref[1:2, :], True, 1.0)
    gkv_ref[:, LANES:2 * LANES] = acc[:, LANES:2 * LANES].astype(BF16)
    base = q_w + 2 * LANES
    na_w = NA_HEADS * HEAD_DIM
    acc = _dot(hb, w_ref[:, base:base + na_w])
    for s in range(na_w // LANES):
        nq_ref[:, s * LANES:(s + 1) * LANES] = slab(acc, s, nag_ref[0:1, :], False, ATTN_SCALE)
    acc = _dot(hb, w_ref[:, base + na_w:base + 2 * na_w])
    for s in range(na_w // LANES):
        nk_ref[:, s * LANES:(s + 1) * LANES] = slab(acc, s, nag_ref[1:2, :], False, 1.0)
    nv_ref[...] = _dot(hb, w_ref[:, base + 2 * na_w:base + 3 * na_w]).astype(BF16)


def _odd_proj(x, shift, scale, gain, w, gqa_gain, na_gain, rope):
    bsz, n, d = x.shape
    tm = min(PROJ_ROWS, n)
    gqg = jnp.tile(gqa_gain.astype(F32), (1, 2))
    nag = jnp.tile(na_gain.astype(F32), (1, 2))
    in_specs = [
        _row_spec(tm, d), _bcast_spec(d), _bcast_spec(d), _const_spec((1, d)), _const_spec(w.shape),
        _const_spec((2, LANES)), _const_spec((2, LANES)), _const_spec((LANES, LANES)),
    ]
    args = [x, shift, scale, gain.reshape(1, d), w, gqg, nag, _seg_mean_matrix()]
    if rope:
        tab_spec = pl.BlockSpec((tm, LANES), lambda b, i: (i, 0))
        in_specs += [tab_spec] * 3
        args += list(_rope_tables(n))
    widths = [GQA_HEADS * HEAD_DIM, 2 * LANES, NA_HEADS * HEAD_DIM, NA_HEADS * HEAD_DIM, NA_HEADS * HEAD_DIM]
    return pl.pallas_call(
        functools.partial(_odd_proj_kernel, rope=rope),
        grid=(bsz, n // tm),
        in_specs=in_specs,
        out_specs=[_row_spec(tm, wd) for wd in widths],
        out_shape=[jax.ShapeDtypeStruct((bsz, n, wd), BF16) for wd in widths],
        compiler_params=_cparams(("parallel", "parallel")),
        name="odd_proj_x" if rope else "odd_proj_ctx",
    )(*args)


def _flash_scratch(rows):
    return [pltpu.VMEM((rows, FLASH_TK), F32),
            pltpu.VMEM((rows, FLASH_TK), F32),
            pltpu.VMEM((rows, 1), F32),
            pltpu.VMEM((rows, 2 * LANES), F32)]


def _flash(qs, sources, scratch):
    s0_ref, s1_ref, m_ref, acc_ref = scratch
    s_refs = (s0_ref, s1_ref)
    rows = qs.shape[0]
    tk = s0_ref.shape[1]
    m_ref[...] = jnp.full(m_ref.shape, -jnp.inf, F32)
    acc_ref[...] = jnp.zeros(acc_ref.shape, F32)

    def issue(slot, width, k):
        s_refs[slot][:, 0:width] = _dot_nt(qs, k)

    def consume(slot, width, v):
        s = s_refs[slot][:, 0:width]
        m_old = m_ref[...]
        m_new = jnp.maximum(m_old, jnp.max(s, axis=-1, keepdims=True))
        alpha = jnp.exp(m_old - m_new)
        p = jnp.exp((s - m_new).astype(BF16))
        ones = (lax.broadcasted_iota(I32, (width, LANES), 1) == 0).astype(BF16)
        m_ref[...] = m_new
        acc_ref[...] = alpha * acc_ref[...] + _dot(p, jnp.concatenate([v, ones], axis=1))

    issued = 0
    prev = None
    for k_ref, v_ref, length in sorted(sources, key=lambda src: src[2]):
        chunk = min(tk, length)
        steps = length // chunk
        if steps == 1:
            slot = issued % 2
            issue(slot, chunk, k_ref[...])
            if prev is not None:
                consume(prev[0], prev[1], prev[2]())
            prev = (slot, chunk, lambda v_ref=v_ref: v_ref[...])
            issued += 1
            continue
        assert steps % 2 == 0 and chunk == tk
        base = issued % 2

        def kv(ref, c):
            return ref[pl.ds(pl.multiple_of(c * tk, tk), tk), :]

        issue(base, tk, kv(k_ref, 0))
        if prev is not None:
            consume(prev[0], prev[1], prev[2]())

        def body(j, carry, k_ref=k_ref, v_ref=v_ref, base=base):
            issue(1 - base, tk, kv(k_ref, 2 * j + 1))
            consume(base, tk, kv(v_ref, 2 * j))
            issue(base, tk, kv(k_ref, 2 * j + 2))
            consume(1 - base, tk, kv(v_ref, 2 * j + 1))
            return carry

        lax.fori_loop(0, steps // 2 - 1, body, 0)
        issue(1 - base, tk, kv(k_ref, steps - 1))
        consume(base, tk, kv(v_ref, steps - 2))
        prev = (1 - base, tk, lambda v_ref=v_ref, steps=steps: kv(v_ref, steps - 1))
        issued += steps
    consume(prev[0], prev[1], prev[2]())
    return acc_ref[:, 0:LANES] / acc_ref[:, LANES:LANES + 1]


def _lane_ids(shape):
    return lax.broadcasted_iota(I32, shape, len(shape) - 1)


def _diff_attn_kernel(*refs, n_src, lens, lam_init):
    q_ref = refs[0]
    kv_refs = refs[1:1 + 2 * n_src]
    lam_ref, gain_ref, o_ref = refs[1 + 2 * n_src:4 + 2 * n_src]
    scratch = refs[4 + 2 * n_src:]
    q = q_ref[...]
    tq = q.shape[0]
    lo = _lane_ids(q.shape) < HEAD_DIM
    zero = jnp.zeros_like(q)
    qs = jnp.concatenate([jnp.where(lo, q, zero), jnp.where(lo, zero, q)], axis=0)
    sources = [(kv_refs[2 * i], kv_refs[2 * i + 1], lens[i]) for i in range(n_src)]
    a = _flash(qs, sources, scratch)
    lp = lam_ref[...]
    lam = (jnp.exp(jnp.sum(lp[0:1, :] * lp[1:2, :], axis=-1, keepdims=True))
           - jnp.exp(jnp.sum(lp[2:3, :] * lp[3:4, :], axis=-1, keepdims=True)) + lam_init)
    o = a[0:tq, :] - lam * a[tq:2 * tq, :]
    ms = jnp.mean(o * o, axis=-1, keepdims=True)
    o = o * lax.rsqrt(ms + EPS) * gain_ref[...] * (1.0 - lam_init)
    o_ref[...] = o.astype(BF16)


def _diff_attention(q, kv_list, lam_params, out_gain, lam_init):
    bsz, n, width = q.shape
    tq = min(DIFF_TQ, n)
    lens = tuple(k.shape[1] for k, _ in kv_list)
    in_specs = [pl.BlockSpec((None, tq, LANES), lambda b, h, i: (b, i, h))]
    args = [q]
    for (k, v), length in zip(kv_list, lens):
        spec = pl.BlockSpec((None, length, LANES), lambda b, h, i: (b, 0, h))
        in_specs += [spec, spec]
        args += [k, v]
    in_specs += [pl.BlockSpec(lam_params.shape, lambda b, h, i: (0, 0)),
                 pl.BlockSpec((1, LANES), lambda b, h, i: (0, 0))]
    args += [lam_params.astype(F32), out_gain.reshape(1, LANES).astype(F32)]
    return pl.pallas_call(
        functools.partial(_diff_attn_kernel, n_src=len(kv_list), lens=lens, lam_init=lam_init),
        grid=(bsz, DIFF_HEADS, n // tq),
        in_specs=in_specs,
        out_specs=pl.BlockSpec((None, tq, LANES), lambda b, h, i: (b, i, h)),
        out_shape=jax.ShapeDtypeStruct((bsz, n, width), BF16),
        scratch_shapes=_flash_scratch(2 * tq),
        compiler_params=_cparams(("parallel", "parallel", "parallel")),
        name="diff_attn",
    )(*args)


def _gqa_kernel(q_ref, kc_ref, vc_ref, kx_ref, vx_ref, o_ref, *scratch, lens):
    tq = q_ref.shape[0]
    sources = [(kc_ref, vc_ref, lens[0]), (kx_ref, vx_ref, lens[1])]
    lanes = _lane_ids((tq, LANES))
    for kv in range(GQA_KV_HEADS):
        mine = (lanes // HEAD_DIM) == kv
        rows = []
        for half in range(2):
            hh = q_ref[:, (2 * kv + half) * LANES:(2 * kv + half + 1) * LANES]
            sw = pltpu.roll(hh.astype(F32), HEAD_DIM, 1).astype(BF16)
            zero = jnp.zeros_like(hh)
            a_here, b_here = (hh, sw) if kv == 0 else (sw, hh)
            rows += [jnp.where(mine, a_here, zero), jnp.where(mine, b_here, zero)]
        qs = jnp.concatenate(rows, axis=0)
        o = _flash(qs, sources, scratch)
        for half in range(2):
            oa = o[(2 * half) * tq:(2 * half + 1) * tq, :]
            ob = o[(2 * half + 1) * tq:(2 * half + 2) * tq, :]
            oa_sw = pltpu.roll(oa, HEAD_DIM, 1)
            ob_sw = pltpu.roll(ob, HEAD_DIM, 1)
            if kv == 0:
                res = jnp.where(lanes < HEAD_DIM, oa, ob_sw)
            else:
                res = jnp.where(lanes < HEAD_DIM, oa_sw, ob)
            o_ref[:, (2 * kv + half) * LANES:(2 * kv + half + 1) * LANES] = res.astype(BF16)


def _gqa_attention(q, kv_c, kv_x):
    bsz, n, width = q.shape
    tq = min(GQA_TQ, n)
    lens = (kv_c.shape[1], kv_x.shape[1])

    def kspec(length, col):
        return pl.BlockSpec((None, length, LANES), lambda b, i, col=col: (b, 0, col))

    return pl.pallas_call(
        functools.partial(_gqa_kernel, lens=lens),
        grid=(bsz, n // tq),
        in_specs=[pl.BlockSpec((None, tq, width), lambda b, i: (b, i, 0)),
                  kspec(lens[0], 0), kspec(lens[0], 1), kspec(lens[1], 0), kspec(lens[1], 1)],
        out_specs=pl.BlockSpec((None, tq, width), lambda b, i: (b, i, 0)),
        out_shape=jax.ShapeDtypeStruct((bsz, n, width), BF16),
        scratch_shapes=_flash_scratch(4 * tq),
        compiler_params=_cparams(("parallel", "parallel")),
        name="gqa_attn",
    )(q, kv_c, kv_c, kv_x, kv_x)


def _gla_chunks(chains, lt, ut):
    c, s16 = HGRN_CHUNK, HGRN_SUB
    nsub = c // s16
    ridx = lax.broadcasted_iota(I32, (c, 1), 0)
    sidx = lax.broadcasted_iota(I32, (s16, 1), 0)
    cums = [jnp.dot(ut if rev else lt, g, precision=HIGHEST, preferred_element_type=F32)
            for _, _, g, _, _, rev in chains]
    tots = [cum[0:1, :] if ch[5] else cum[c - 1:c, :] for ch, cum in zip(chains, cums)]
    vbs = [ch[3].astype(BF16) for ch in chains]
    out_states = [_dot_nt((ch[0] * jnp.exp(cum)).astype(BF16), ch[4].astype(BF16))
                  for ch, cum in zip(chains, cums)]
    new_sts = [ch[4] * jnp.exp(tot) + _dot_tn(vb, (ch[1] * jnp.exp(tot - cum)).astype(BF16))
               for ch, cum, tot, vb in zip(chains, cums, tots, vbs)]
    blocks = [[None] * nsub for _ in chains]
    for i in range(nsub):
        starts = [c - s16 * (i + 1) if ch[5] else s16 * i for ch in chains]
        accs = [jnp.zeros((s16, ch[3].shape[1]), F32) for ch in chains]
        if i > 0:
            scores = []
            for (q, k, _, _, _, rev), cum, r0 in zip(chains, cums, starts):
                if rev:
                    bnd = cum[r0 + s16:r0 + s16 + 1, :]
                    prev = ridx >= r0 + s16
                else:
                    bnd = cum[r0 - 1:r0, :]
                    prev = ridx < r0
                qt = (q[r0:r0 + s16] * jnp.exp(cum[r0:r0 + s16] - bnd)).astype(BF16)
                kt = (k * jnp.exp(jnp.where(prev, bnd - cum, -jnp.inf))).astype(BF16)
                scores.append(_dot_nt(qt, kt).astype(BF16))
            accs = [_dot(a, vb) for a, vb in zip(scores, vbs)]
        for s in range(s16):
            for ci, ((q, k, _, v, _, rev), cum, r0) in enumerate(zip(chains, cums, starts)):
                cum_i = cum[r0:r0 + s16]
                valid = (sidx <= s) if rev else (sidx >= s)
                d = jnp.where(valid, cum_i - cum_i[s:s + 1, :], -jnp.inf)
                a = jnp.sum(q[r0:r0 + s16] * k[r0 + s:r0 + s + 1, :] * jnp.exp(d), axis=-1, keepdims=True)
                accs[ci] = accs[ci] + a * v[r0 + s:r0 + s + 1, :]
        for ci, ch in enumerate(chains):
            blocks[ci][nsub - 1 - i if ch[5] else i] = accs[ci]
    return [(o + jnp.concatenate(b, axis=0), st) for o, b, st in zip(out_states, blocks, new_sts)]


def _hgrn_kernel(qx_ref, kfx_ref, gfx_ref, kbx_ref, gbx_ref, vx_ref, sgx_ref,
                 qc_ref, kfc_ref, gfc_ref, kbc_ref, gbc_ref, vc_ref, sgc_ref,
                 gain_ref, lt_ref, ut_ref, ox_ref, oc_ref, accx_ref, accc_ref):
    c = HGRN_CHUNK
    lt, ut = lt_ref[...], ut_ref[...]
    heads = qx_ref.shape[1] // LANES

    def sweep(q_ref, kf_ref, gf_ref, kb_ref, gb_ref, v_ref, acc_ref, states):
        nchunks = q_ref.shape[0] // c
        acc_ref[...] = jnp.zeros(acc_ref.shape, F32)

        def body(i, carry):
            rf = pl.multiple_of(i * c, c)
            rb = pl.multiple_of((nchunks - 1 - i) * c, c)
            chains = []
            for h in range(heads):
                st_f, st_b = carry[h]
                cols = slice(h * LANES, (h + 1) * LANES)

                def rows(ref, r0, cols=cols):
                    return ref[pl.ds(r0, c), cols].astype(F32)

                chains.append((rows(q_ref, rf), rows(kf_ref, rf), rows(gf_ref, rf), rows(v_ref, rf), st_f, False))
                chains.append((rows(q_ref, rb), rows(kb_ref, rb), rows(gb_ref, rb), rows(v_ref, rb), st_b, True))
            res = _gla_chunks(chains, lt, ut)
            for h in range(heads):
                cols = slice(h * LANES, (h + 1) * LANES)
                acc_ref[pl.ds(rf, c), cols] += res[2 * h][0]
                acc_ref[pl.ds(rb, c), cols] += res[2 * h + 1][0]
            return tuple((res[2 * h][1], res[2 * h + 1][1]) for h in range(heads))

        return lax.fori_loop(0, nchunks, body, states)

    zero = jnp.zeros((LANES, LANES), F32)
    states = sweep(qc_ref, kfc_ref, gfc_ref, kbc_ref, gbc_ref, vc_ref, accc_ref,
                   tuple((zero, zero) for _ in range(heads)))
    sweep(qx_ref, kfx_ref, gfx_ref, kbx_ref, gbx_ref, vx_ref, accx_ref, states)

    def finish(acc_ref, sg_ref, o_ref):
        rows = acc_ref.shape[0]
        tile = min(rows, 512)

        def body(i, _):
            r = pl.multiple_of(i * tile, tile)
            for h in range(heads):
                cols = slice(h * LANES, (h + 1) * LANES)
                o = acc_ref[pl.ds(r, tile), cols]
                ms = jnp.mean(o * o, axis=-1, keepdims=True)
                o = o * lax.rsqrt(ms + EPS) * gain_ref[...]
                o_ref[pl.ds(r, tile), cols] = (o * sg_ref[pl.ds(r, tile), cols].astype(F32)).astype(BF16)
            return 0

        lax.fori_loop(0, rows // tile, body, 0)

    finish(accx_ref, sgx_ref, ox_ref)
    finish(accc_ref, sgc_ref, oc_ref)


def _hgrn(px, pc, out_gain):
    bsz, n, width = px[0].shape
    m = pc[0].shape[1]
    c = HGRN_CHUNK
    r = jnp.arange(c)
    lt = (r[:, None] >= r[None, :]).astype(F32)
    ut = (r[:, None] <= r[None, :]).astype(F32)

    bw = HGRN_HEADS_PER_STEP * LANES

    def spec(length):
        return pl.BlockSpec((None, length, bw), lambda b, h: (b, 0, h))

    const = lambda shape: pl.BlockSpec(shape, lambda b, h: (0, 0))
    return pl.pallas_call(
        _hgrn_kernel,
        grid=(bsz, width // bw),
        in_specs=[spec(n)] * 7 + [spec(m)] * 7 + [const((1, LANES)), const((c, c)), const((c, c))],
        out_specs=[spec(n), spec(m)],
        out_shape=[jax.ShapeDtypeStruct((bsz, n, width), BF16), jax.ShapeDtypeStruct((bsz, m, width), BF16)],
        scratch_shapes=[pltpu.VMEM((n, bw), F32), pltpu.VMEM((m, bw), F32)],
        compiler_params=_cparams(("parallel", "parallel")),
        name="hgrn2",
    )(*px, *pc, out_gain.reshape(1, LANES).astype(F32), lt, ut)


def _na_bias_tables(rpb, rows):
    qrows, band, heads = NA_QROWS, NA_BAND, rpb.shape[0]
    pad = GRID_W - NA_COLS
    wide = jnp.pad(rpb.astype(F32), ((0, 0), (0, 0), (pad, pad)))
    toeplitz = jnp.stack([wide[:, :, GRID_W - 1 - qc:2 * GRID_W - 1 - qc] for qc in range(GRID_W)], axis=2)
    qc = np.arange(GRID_W)[:, None]
    kc = np.arange(GRID_W)[None, :]
    cstart = np.clip(qc - NA_COLS // 2, 0, GRID_W - NA_COLS)
    col_ok = (kc >= cstart) & (kc < cstart + NA_COLS)
    toeplitz = jnp.where(col_ok[None, None], toeplitz, NEG_BIG)
    tabs = []
    for r0 in (0, qrows, rows - qrows):
        rs = min(max(r0 - NA_ROWS // 2, 0), rows - band)
        qr = r0 + np.arange(qrows)[:, None]
        kr = rs + np.arange(band)[None, :]
        rstart = np.clip(qr - NA_ROWS // 2, 0, rows - NA_ROWS)
        row_ok = (kr >= rstart) & (kr < rstart + NA_ROWS)
        dr = np.clip(kr - qr + NA_ROWS - 1, 0, 2 * NA_ROWS - 2)
        tiles = jnp.take(toeplitz, jnp.asarray(dr.reshape(-1), I32), axis=1)
        tiles = jnp.where(row_ok.reshape(1, -1, 1, 1), tiles, NEG_BIG)
        tiles = tiles.reshape(heads, qrows, band, GRID_W, GRID_W).transpose(0, 1, 3, 2, 4)
        tabs.append(tiles.reshape(heads, qrows * GRID_W, band * GRID_W))
    return jnp.stack(tabs)


def _na_kernel(q_ref, k_ref, v_ref, kc_ref, vc_ref, bias_ref, o_ref, *, rows):
    j = pl.program_id(2)
    tq = q_ref.shape[0]
    band = NA_BAND * GRID_W
    rs = jnp.clip(j * NA_QROWS - NA_ROWS // 2, 0, rows - NA_BAND)
    start = pl.multiple_of(rs * GRID_W, NA_ROWS // 2 * GRID_W)
    q = q_ref[...]
    lo = _lane_ids(q.shape) < HEAD_DIM
    zero = jnp.zeros_like(q)
    qh = [jnp.where(lo, q, zero), jnp.where(lo, zero, q)]
    kb = k_ref[pl.ds(start, band), :]
    kc = kc_ref[...]

    def with_ones(v):
        ones = (lax.broadcasted_iota(I32, v.shape, 1) == 0).astype(BF16)
        return jnp.concatenate([v, ones], axis=1)

    vb = with_ones(v_ref[pl.ds(start, band), :])
    vc = with_ones(vc_ref[...])
    s_win = [_dot_nt(qh[h], kb) + bias_ref[h] for h in range(2)]
    s_ctx = [_dot_nt(qh[h], kc) for h in range(2)]
    m = [jnp.maximum(jnp.max(s_win[h], axis=-1, keepdims=True), jnp.max(s_ctx[h], axis=-1, keepdims=True))
         for h in range(2)]
    p_win = [jnp.exp((s_win[h] - m[h]).astype(BF16)) for h in range(2)]
    p_ctx = [jnp.exp((s_ctx[h] - m[h]).astype(BF16)) for h in range(2)]
    pv = [_dot(p_ctx[h], vc) + _dot(p_win[h], vb) for h in range(2)]
    o = [pv[h][:, 0:LANES] / pv[h][:, LANES:LANES + 1] for h in range(2)]
    o_ref[...] = jnp.where(lo, o[0], o[1]).astype(BF16)


def _na_attention(q, k, v, kc, vc, rpb):
    bsz, n, width = q.shape
    rows = n // GRID_W
    tq = NA_QROWS * GRID_W
    nt = n // tq
    bias = _na_bias_tables(rpb, rows)
    m = kc.shape[1]

    def cls(j):
        return jnp.where(j == 0, 0, jnp.where(j == nt - 1, 2, 1))

    full = lambda length: pl.BlockSpec((None, length, LANES), lambda b, h, j: (b, 0, h))
    return pl.pallas_call(
        functools.partial(_na_kernel, rows=rows),
        grid=(bsz, NA_HEADS // 2, nt),
        in_specs=[pl.BlockSpec((None, tq, LANES), lambda b, h, j: (b, j, h)),
                  full(n), full(n), full(m), full(m),
                  pl.BlockSpec((None, 2, tq, NA_BAND * GRID_W), lambda b, h, j: (cls(j), h, 0, 0))],
        out_specs=pl.BlockSpec((None, tq, LANES), lambda b, h, j: (b, j, h)),
        out_shape=jax.ShapeDtypeStruct((bsz, n, width), BF16),
        compiler_params=_cparams(("parallel", "parallel", "parallel")),
        name="na_attn",
    )(q, k, v, kc, vc, bias)


def _out_proj_kernel(a_ref, b_ref, w_ref, x_ref, gate_ref, gain_ref, shift_ref, scale_ref, r_ref,
                     x1_ref, tok_ref, logit_ref):
    half = a_ref.shape[1]
    y = _dot(a_ref[...], w_ref[0:half, :]) + _dot(b_ref[...], w_ref[half:2 * half, :])
    x1 = x_ref[...] + gate_ref[...] * y
    x1_ref[...] = x1
    h = _norm_mod(x1, gain_ref[...], shift_ref[...], scale_ref[...])
    logit_ref[...] = jnp.dot(h, r_ref[...], precision=HIGHEST, preferred_element_type=F32)
    tok_ref[...] = _pack_pair(h[:, 0:HALF], h[:, HALF:2 * HALF])


def _out_proj(a, b, w, x, gate, gain, shift, scale, router_pad):
    bsz, n, d = x.shape
    tm = min(PROJ_ROWS, n)
    return pl.pallas_call(
        _out_proj_kernel,
        grid=(bsz, n // tm),
        in_specs=[_row_spec(tm, a.shape[2]), _row_spec(tm, b.shape[2]), _const_spec(w.shape), _row_spec(tm, d),
                  _bcast_spec(d), _const_spec((1, d)), _bcast_spec(d), _bcast_spec(d), _const_spec(router_pad.shape)],
        out_specs=[_row_spec(tm, d), _row_spec(tm, HALF), _row_spec(tm, LANES)],
        out_shape=[jax.ShapeDtypeStruct((bsz, n, d), F32), jax.ShapeDtypeStruct((bsz, n, HALF), I32),
                   jax.ShapeDtypeStruct((bsz, n, LANES), F32)],
        compiler_params=_cparams(("parallel", "parallel")),
        name="out_proj",
    )(a, b, w, x, gate, gain.reshape(1, d), shift, scale, router_pad)


def _route_kernel(logit_ref, bias_ref, tri_ref, etri_ref, w_ref, loc_ref, tcnt_ref, tcarry_ref, toff_ref, cnt_ref,
                  masked_ref, carry_ref):
    step = pl.program_id(0)

    @pl.when(step == 0)
    def _():
        carry_ref[...] = jnp.zeros_like(carry_ref)

    tr = logit_ref.shape[0]
    scores = jax.nn.sigmoid(logit_ref[...].T[0:N_EXPERTS, :])
    biased = scores + bias_ref[...]
    gsz = EXPERTS_PER_GROUP
    sub = lax.broadcasted_iota(I32, (gsz, tr), 0).astype(F32)
    gscore = []
    for g in range(N_GROUPS):
        bg = biased[g * gsz:(g + 1) * gsz, :]
        m1 = jnp.max(bg, axis=0, keepdims=True)
        i1 = jnp.min(jnp.where(bg == m1, sub, float(gsz)), axis=0, keepdims=True)
        m2 = jnp.max(jnp.where(sub == i1, -jnp.inf, bg), axis=0, keepdims=True)
        gscore.append(m1 + m2)
    for g in range(N_GROUPS):
        beaten = jnp.zeros((1, tr), F32)
        for o in range(N_GROUPS):
            if o == g:
                continue
            wins = (gscore[o] >= gscore[g]) if o < g else (gscore[o] > gscore[g])
            beaten = beaten + jnp.where(wins, 1.0, 0.0)
        keep = beaten < float(TOPK_GROUPS)
        masked_ref[g * gsz:(g + 1) * gsz, :] = jnp.where(keep, biased[g * gsz:(g + 1) * gsz, :], -jnp.inf)
    cur = masked_ref[...]
    eid = lax.broadcasted_iota(I32, (N_EXPERTS, tr), 0).astype(F32)
    sel = jnp.zeros((N_EXPERTS, tr), F32)
    picks, weights = [], []
    for _ in range(TOP_K):
        m = jnp.max(cur, axis=0, keepdims=True)
        ik = jnp.min(jnp.where(cur == m, eid, float(N_EXPERTS)), axis=0, keepdims=True)
        hit = eid == ik
        weights.append(jnp.sum(jnp.where(hit, scores, 0.0), axis=0, keepdims=True))
        sel = sel + jnp.where(hit, 1.0, 0.0)
        cur = jnp.where(hit, -jnp.inf, cur)
        picks.append(ik)
    wsum = weights[0]
    for wk in weights[1:]:
        wsum = wsum + wk
    tile_cnt = jnp.broadcast_to(jnp.sum(sel, axis=1, keepdims=True), (N_EXPERTS, LANES))
    tile_cnt = jnp.floor((tile_cnt + (RUN_ALIGN - 1)) * (1.0 / RUN_ALIGN)) * RUN_ALIGN
    tile_off = jnp.dot(etri_ref[...], tile_cnt, precision=HIGHEST, preferred_element_type=F32)
    row = _dot(sel.astype(BF16), tri_ref[...]) + tile_off[:, 0:1]
    for kk in range(TOP_K):
        w_ref[kk:kk + 1, :] = weights[kk] / wsum * ROUTED_SCALE
        loc_ref[kk:kk + 1, :] = jnp.sum(jnp.where(eid == picks[kk], row, 0.0), axis=0, keepdims=True).astype(I32)
    tcnt_ref[...] = tile_cnt.astype(I32)
    tcarry_ref[...] = carry_ref[...].astype(I32)
    toff_ref[...] = tile_off.astype(I32)
    carry_ref[...] = carry_ref[...] + tile_cnt
    cnt_ref[...] = carry_ref[...].astype(I32)


def _route(logits, router_bias):
    t = logits.shape[0]
    tr = MOE_TILE
    assert t % tr == 0
    nt = t // tr
    r = jnp.arange(tr)
    tri = (r[:, None] < r[None, :]).astype(BF16)
    e = jnp.arange(N_EXPERTS)
    etri = (e[:, None] > e[None, :]).astype(F32)
    kt_spec = pl.BlockSpec((TOP_K, tr), lambda i: (0, i))
    tile_spec = pl.BlockSpec((None, N_EXPERTS, LANES), lambda i: (i, 0, 0))
    tile_shape = jax.ShapeDtypeStruct((nt, N_EXPERTS, LANES), I32)
    return pl.pallas_call(
        _route_kernel,
        grid=(nt,),
        in_specs=[pl.BlockSpec((tr, LANES), lambda i: (i, 0)),
                  pl.BlockSpec((N_EXPERTS, 1), lambda i: (0, 0)),
                  pl.BlockSpec((tr, tr), lambda i: (0, 0)),
                  pl.BlockSpec((N_EXPERTS, N_EXPERTS), lambda i: (0, 0))],
        out_specs=[kt_spec, kt_spec, tile_spec, tile_spec, tile_spec,
                   pl.BlockSpec((N_EXPERTS, LANES), lambda i: (0, 0))],
        out_shape=[jax.ShapeDtypeStruct((TOP_K, t), F32), jax.ShapeDtypeStruct((TOP_K, t), I32),
                   tile_shape, tile_shape, tile_shape, jax.ShapeDtypeStruct((N_EXPERTS, LANES), I32)],
        scratch_shapes=[pltpu.VMEM((N_EXPERTS, tr), F32), pltpu.VMEM((N_EXPERTS, LANES), F32)],
        compiler_params=_cparams(("arbitrary",)),
        name="moe_route",
    )(logits, router_bias.astype(F32).reshape(N_EXPERTS, 1), tri, etri)


TAB_WORDS = 1024
TAB_FIELD = 128
N_RUNS = N_EXPERTS + 1
TILE_ROWS = -(-(MOE_TILE * TOP_K + N_EXPERTS * (RUN_ALIGN - 1)) // MOE_TILE) * MOE_TILE
FILLER_ROWS = TILE_ROWS - MOE_TILE * TOP_K
RUN_PIECE = 64


def _run_copies(tab_ref, local_ref, global_ref, sem, to_global):
    def piece(ls, gs, off, rows, priority):
        loc = local_ref.at[pl.ds(pl.multiple_of(ls + off, RUN_ALIGN), rows)]
        glo = global_ref.at[pl.ds(pl.multiple_of(gs + off, RUN_ALIGN), rows)]
        src, dst = (loc, glo) if to_global else (glo, loc)
        pltpu.make_async_copy(src, dst, sem).start(priority=priority)

    def body(e, carry):
        gs = tab_ref[e]
        c = tab_ref[TAB_FIELD + e]
        ls = tab_ref[2 * TAB_FIELD + e]
        whole = c >> (RUN_PIECE.bit_length() - 1)

        def big(j, carry2):
            piece(ls, gs, j * RUN_PIECE, RUN_PIECE, 0)
            return carry2

        lax.fori_loop(0, whole, big, 0)
        for b in range(RUN_ALIGN.bit_length() - 1, RUN_PIECE.bit_length() - 1):
            @pl.when(((c >> b) & 1) == 1)
            def _(b=b):
                piece(ls, gs, whole * RUN_PIECE + (c & ((1 << b) - 1) & (RUN_PIECE - 1)), 1 << b, 1)
        return carry

    lax.fori_loop(0, N_RUNS, body, 0)


def _run_wait(local_ref, global_ref, sem):
    pltpu.make_async_copy(global_ref.at[pl.ds(0, TILE_ROWS)], local_ref, sem).wait()


def _dispatch_kernel(tab_ref, loc_ref, tok_ref, rows_in_ref, rows_ref, srt_ref, sem, *, n_tiles):
    del rows_in_ref
    i = pl.program_id(0)
    slot = i % 2
    tt = tok_ref.shape[0]
    lo, hi = _unpack_pair(tok_ref[...])
    lo, hi = lo.astype(BF16), hi.astype(BF16)
    loc = loc_ref[...]
    for rb in range(TILE_ROWS // tt):
        r = rb * tt + lax.broadcasted_iota(I32, (tt, tt), 0)
        p = jnp.zeros((tt, tt), F32)
        for kk in range(TOP_K):
            p = jnp.where(loc[kk:kk + 1, :] == r, 1.0, p)
        pb = p.astype(BF16)
        srt_ref[slot, rb * tt:(rb + 1) * tt, :] = _pack_exact_pair(_dot(pb, lo), _dot(pb, hi))

    @pl.when(i > 0)
    def _():
        _run_wait(srt_ref.at[1 - slot], rows_ref, sem.at[1 - slot])

    _run_copies(tab_ref, srt_ref.at[slot], rows_ref, sem.at[slot], True)

    @pl.when(i == n_tiles - 1)
    def _():
        _run_wait(srt_ref.at[slot], rows_ref, sem.at[slot])


def _dispatch(tab, loc, tok, rows_buf):
    t = tok.shape[0]
    tt = MOE_TILE
    return pl.pallas_call(
        functools.partial(_dispatch_kernel, n_tiles=t // tt),
        grid=(t // tt,),
        in_specs=[pl.BlockSpec((TAB_WORDS,), lambda i: (i,), memory_space=pltpu.SMEM),
                  pl.BlockSpec((TOP_K, tt), lambda i: (0, i)),
                  pl.BlockSpec((tt, HALF), lambda i: (i, 0)),
                  pl.BlockSpec(memory_space=pl.ANY)],
        out_specs=pl.BlockSpec(memory_space=pl.ANY),
        out_shape=jax.ShapeDtypeStruct(rows_buf.shape, rows_buf.dtype),
        scratch_shapes=[pltpu.VMEM((2, TILE_ROWS, HALF), I32), pltpu.SemaphoreType.DMA((2,))],
        input_output_aliases={3: 0},
        compiler_params=_cparams(("arbitrary",)),
        name="moe_dispatch",
    )(tab, loc, tok, rows_buf)


def _expert_kernel(be_ref, nused_ref, x_ref, wgu_ref, wd_ref, y_ref):
    @pl.when(pl.program_id(0) < nused_ref[0])
    def _():
        lo, hi = _unpack_pair(x_ref[...])
        gu = _dot(lo.astype(BF16), wgu_ref[0:HALF, :]) + _dot(hi.astype(BF16), wgu_ref[HALF:2 * HALF, :])
        h = (_silu(gu[:, 0:EXPERT_DIM]) * gu[:, EXPERT_DIM:2 * EXPERT_DIM]).astype(BF16)
        y = _dot(h, wd_ref[...])
        y_ref[...] = _pack_pair(y[:, 0:HALF], y[:, HALF:2 * HALF])

    @pl.when(pl.program_id(0) >= nused_ref[0])
    def _():
        y_ref[...] = jnp.zeros_like(y_ref)


def _experts(block_expert, nused, rows, wgu, wd):
    n_rows = rows.shape[0]
    nb = n_rows // MOE_BLOCK

    def row_map(i, be, nu):
        return (jnp.minimum(i, nu[0] - 1), 0)

    def w_map(i, be, nu):
        return (be[jnp.minimum(i, nu[0] - 1)], 0, 0)

    grid_spec = pltpu.PrefetchScalarGridSpec(
        num_scalar_prefetch=2,
        grid=(nb,),
        in_specs=[pl.BlockSpec((MOE_BLOCK, HALF), row_map),
                  pl.BlockSpec((None, D_MODEL, 2 * EXPERT_DIM), w_map),
                  pl.BlockSpec((None, EXPERT_DIM, D_MODEL), w_map)],
        out_specs=pl.BlockSpec((MOE_BLOCK, HALF), lambda i, be, nu: (i, 0)),
    )
    return pl.pallas_call(
        _expert_kernel,
        grid_spec=grid_spec,
        out_shape=jax.ShapeDtypeStruct((n_rows, HALF), I32),
        compiler_params=_cparams(("arbitrary",)),
        name="moe_experts",
    )(block_expert, nused, rows, wgu, wd)


def _combine_kernel(tab_ref, tabn_ref, x1_ref, tok_ref, loc_ref, w_ref, gate_ref, wgu_ref, wd_ref, y_ref, o_ref,
                    buf_ref, sem, *, n_tiles):
    i = pl.program_id(0)
    slot = i % 2
    tt = tok_ref.shape[0]

    @pl.when(i == 0)
    def _():
        _run_copies(tab_ref, buf_ref.at[slot], y_ref, sem.at[slot], False)

    @pl.when(i + 1 < n_tiles)
    def _():
        _run_copies(tabn_ref, buf_ref.at[1 - slot], y_ref, sem.at[1 - slot], False)

    lo, hi = _unpack_pair(tok_ref[...])
    gu = _dot(lo.astype(BF16), wgu_ref[0:HALF, :]) + _dot(hi.astype(BF16), wgu_ref[HALF:2 * HALF, :])
    h = (_silu(gu[:, 0:EXPERT_DIM]) * gu[:, EXPERT_DIM:2 * EXPERT_DIM]).astype(BF16)
    shared = _dot(h, wd_ref[...])
    _run_wait(buf_ref.at[slot], y_ref, sem.at[slot])
    acc_lo = shared[:, 0:HALF]
    acc_hi = shared[:, HALF:2 * HALF]
    loc = loc_ref[...]
    w = w_ref[...]
    for rb in range(TILE_ROWS // tt):
        r = rb * tt + lax.broadcasted_iota(I32, (tt, tt), 1)
        wm = jnp.zeros((tt, tt), F32)
        for kk in range(TOP_K):
            wm = jnp.where(loc[:, kk:kk + 1] == r, w[:, kk:kk + 1], wm)
        wb = wm.astype(BF16)
        ylo, yhi = _unpack_pair(buf_ref[slot, rb * tt:(rb + 1) * tt, :])
        acc_lo = acc_lo + _dot(wb, ylo.astype(BF16))
        acc_hi = acc_hi + _dot(wb, yhi.astype(BF16))
    gate = gate_ref[...]
    o_ref[:, 0:HALF] = x1_ref[:, 0:HALF] + gate[:, 0:HALF] * acc_lo
    o_ref[:, HALF:2 * HALF] = x1_ref[:, HALF:2 * HALF] + gate[:, HALF:2 * HALF] * acc_hi


def _combine(tab, x1, tok, loc_tok, w_tok, gate, wgu, wd, y_rows, tokens_per_gate):
    t, d = x1.shape
    tt = MOE_TILE
    nt = t // tt
    per = tokens_per_gate // tt
    return pl.pallas_call(
        functools.partial(_combine_kernel, n_tiles=nt),
        grid=(nt,),
        in_specs=[pl.BlockSpec((TAB_WORDS,), lambda i: (i,), memory_space=pltpu.SMEM),
                  pl.BlockSpec((TAB_WORDS,), lambda i: (jnp.minimum(i + 1, nt - 1),), memory_space=pltpu.SMEM),
                  pl.BlockSpec((tt, d), lambda i: (i, 0)),
                  pl.BlockSpec((tt, HALF), lambda i: (i, 0)),
                  pl.BlockSpec((tt, TOP_K), lambda i: (i, 0)),
                  pl.BlockSpec((tt, TOP_K), lambda i: (i, 0)),
                  pl.BlockSpec((None, 1, d), lambda i: (i // per, 0, 0)),
                  pl.BlockSpec(wgu.shape, lambda i: (0, 0)),
                  pl.BlockSpec(wd.shape, lambda i: (0, 0)),
                  pl.BlockSpec(memory_space=pl.ANY)],
        out_specs=pl.BlockSpec((tt, d), lambda i: (i, 0)),
        out_shape=jax.ShapeDtypeStruct((t, d), F32),
        scratch_shapes=[pltpu.VMEM((2, TILE_ROWS, HALF), I32), pltpu.SemaphoreType.DMA((2,))],
        compiler_params=_cparams(("arbitrary",)),
        name="moe_combine",
    )(tab, tab, x1, tok, loc_tok, w_tok, gate, wgu, wd, y_rows)


def _moe(parts, router_bias, w_gate, w_up, w_down, ws_gate, ws_up, ws_down):
    logits = jnp.concatenate([p[2].reshape(-1, LANES) for p in parts], axis=0)
    t = logits.shape[0]
    w, loc, tile_cnt, tile_carry, tile_off, cnt = _route(logits, router_bias)
    counts = cnt[:, 0]
    padded = (counts + MOE_BLOCK - 1) // MOE_BLOCK * MOE_BLOCK
    pad_end = jnp.cumsum(padded)
    pad_start = pad_end - padded
    nt = t // MOE_TILE
    max_aligned = t * TOP_K + nt * N_EXPERTS * (RUN_ALIGN - 1)
    area_rows = -(-max_aligned // MOE_BLOCK) * MOE_BLOCK + N_EXPERTS * MOE_BLOCK
    n_rows = -(-(area_rows + FILLER_ROWS) // MOE_BLOCK) * MOE_BLOCK
    tile_rows_used = tile_off[:, N_EXPERTS - 1, 0] + tile_cnt[:, N_EXPERTS - 1, 0]

    def field(per_expert, filler):
        vals = jnp.concatenate([per_expert.astype(I32), filler.astype(I32)[:, None]], axis=1)
        return jnp.pad(vals, ((0, 0), (0, TAB_FIELD - N_RUNS)))

    tab = jnp.concatenate(
        [field(pad_start[None, :] + tile_carry[:, :, 0], jnp.full((nt,), area_rows, I32)),
         field(tile_cnt[:, :, 0], TILE_ROWS - tile_rows_used),
         field(tile_off[:, :, 0], tile_rows_used),
         jnp.zeros((nt, TAB_WORDS - 3 * TAB_FIELD), I32)], axis=1).reshape(-1)
    w_tok = w.T
    loc_tok = loc.T
    nb = n_rows // MOE_BLOCK
    block_start = jnp.arange(nb, dtype=I32) * MOE_BLOCK
    block_expert = jnp.minimum(jnp.sum(block_start[:, None] >= pad_end[None, :], axis=1), N_EXPERTS - 1).astype(I32)
    nused = (pad_end[-1] // MOE_BLOCK).astype(I32).reshape(1)
    rows = jnp.zeros((n_rows, HALF), I32)
    off = 0
    for x1, tok, _, _, _ in parts:
        cnt_tok = tok.shape[0] * tok.shape[1]
        part_tab = tab[off // MOE_TILE * TAB_WORDS:(off + cnt_tok) // MOE_TILE * TAB_WORDS]
        rows = _dispatch(part_tab, loc[:, off:off + cnt_tok], tok.reshape(cnt_tok, HALF), rows)
        off += cnt_tok
    wgu = jnp.concatenate([w_gate, w_up], axis=-1).astype(BF16)
    y_rows = _experts(block_expert, nused, rows, wgu, w_down.astype(BF16))
    wsgu = jnp.concatenate([ws_gate, ws_up], axis=-1).astype(BF16)
    wsd = ws_down.astype(BF16)
    outs = []
    off = 0
    for x1, tok, _, gate, per in parts:
        cnt_tok = tok.shape[0] * tok.shape[1]
        part_tab = tab[off // MOE_TILE * TAB_WORDS:(off + cnt_tok) // MOE_TILE * TAB_WORDS]
        o = _combine(part_tab, x1.reshape(cnt_tok, D_MODEL), tok.reshape(cnt_tok, HALF),
                     loc_tok[off:off + cnt_tok], w_tok[off:off + cnt_tok], gate, wsgu, wsd, y_rows, per)
        outs.append(o.reshape(x1.shape))
        off += cnt_tok
    return outs


def kernel(x, c, ctx, c_ctx, ada_w, ada_b, norm_mix, norm_ffn, ev_w_in, ev_w_out, diff_qk_gain, diff_lambda,
           diff_out_gain, hgrn_lb, hgrn_out_gain, od_w_in, od_w_out, gqa_qk_gain, na_qk_gain, na_rpb, moe_router,
           moe_router_bias, moe_w_gate, moe_w_up, moe_w_down, shared_w_gate, shared_w_up, shared_w_down):
    bsz, n, d = x.shape
    m = ctx.shape[1]
    depth = ada_w.shape[0]
    cond_rows = -(-(bsz + 1) // 8) * 8
    cond = jnp.zeros((cond_rows, d), F32).at[0:bsz].set(c).at[bsz].set(c_ctx)
    mods = _ada_mod(cond, ada_w, ada_b)

    xc = ctx
    for layer in range(depth):
        need_ctx = layer < depth - 1
        j = layer // 2
        mod = mods[layer].reshape(cond_rows, 6, d)
        mx = [mod[0:bsz, i][:, None, :] for i in range(6)]
        mc = [jnp.broadcast_to(mod[bsz:bsz + 1, i][:, None, :], (bsz, 1, d)) for i in range(6)]
        if layer % 2 == 0:
            w_in = ev_w_in[j].astype(BF16)
            px = _even_proj(x, mx[0], mx[1], norm_mix[layer], w_in, diff_qk_gain[j], hgrn_lb[0], hgrn_lb[1], j, True)
            pc = _even_proj(xc, mc[0], mc[1], norm_mix[layer], w_in, diff_qk_gain[j], hgrn_lb[0], hgrn_lb[1], j, False)
            lam_init = 0.8 - 0.6 * math.exp(-0.3 * layer)
            a_x = _diff_attention(px[0], [(pc[1], pc[2]), (px[1], px[2])], diff_lambda[j], diff_out_gain[j], lam_init)
            a_c = _diff_attention(pc[0], [(pc[1], pc[2])], diff_lambda[j], diff_out_gain[j], lam_init)
            b_x, b_c = _hgrn(px[3:], pc[3:], hgrn_out_gain[j])
            w_out = ev_w_out[j].astype(BF16)
        else:
            w_in = od_w_in[j].astype(BF16)
            px = _odd_proj(x, mx[0], mx[1], norm_mix[layer], w_in, gqa_qk_gain[j], na_qk_gain[j], True)
            pc = _odd_proj(xc, mc[0], mc[1], norm_mix[layer], w_in, gqa_qk_gain[j], na_qk_gain[j], False)
            a_x = _gqa_attention(px[0], pc[1], px[1])
            b_x = _na_attention(px[2], px[3], px[4], pc[3], pc[4], na_rpb[j])
            a_c = b_c = None
            w_out = od_w_out[j].astype(BF16)
        router_pad = jnp.zeros((d, LANES), F32).at[:, 0:N_EXPERTS].set(moe_router[layer].astype(F32))
        x1, tok_x, logit_x = _out_proj(a_x, b_x, w_out, x, mx[2], norm_ffn[layer], mx[3], mx[4], router_pad)
        parts = [(x1, tok_x, logit_x, mx[5], n)]
        if need_ctx:
            xc1, tok_c, logit_c = _out_proj(a_c, b_c, w_out, xc, mc[2], norm_ffn[layer], mc[3], mc[4], router_pad)
            parts.append((xc1, tok_c, logit_c, mc[5][0:1], bsz * m))
        outs = _moe(parts, moe_router_bias[layer], moe_w_gate[layer], moe_w_up[layer], moe_w_down[layer],
                    shared_w_gate[layer], shared_w_up[layer], shared_w_down[layer])
        x = outs[0]
        if need_ctx:
            xc = outs[1]
    return x
```

```python
import functools
import math

import jax
import jax.numpy as jnp
import numpy as np
from jax import lax
from jax.experimental import pallas as pl
from jax.experimental.pallas import tpu as pltpu

F32 = jnp.float32
BF16 = jnp.bfloat16
I32 = jnp.int32
HIGHEST = lax.Precision.HIGHEST

D_MODEL = 1024
GRID_W = 64
HEAD_DIM = 64
ATTN_SCALE = HEAD_DIM ** -0.5
ROPE_THETA = 10000.0
EPS = 1e-6
DIFF_HEADS = D_MODEL // 256
HGRN_HEADS = D_MODEL // 256
HGRN_CHUNK = 64
HGRN_SUB = 16
GQA_HEADS = D_MODEL // 128
GQA_KV_HEADS = GQA_HEADS // 4
NA_HEADS = D_MODEL // 128
NA_ROWS = 8
NA_COLS = 16
N_EXPERTS = 64
N_GROUPS = 8
EXPERTS_PER_GROUP = N_EXPERTS // N_GROUPS
TOPK_GROUPS = 4
TOP_K = 8
EXPERT_DIM = D_MODEL // 4
ROUTED_SCALE = 2.5
HALF = D_MODEL // 2

LANES = 128
VMEM_LIMIT_BYTES = 56 * 1024 * 1024
PROJ_ROWS = 512
FLASH_TK = 512
FLASH_SB = 64
DIFF_TQ = 512
GQA_TQ = 256
NA_QROWS = 8
NA_BAND = 16
HGRN_HEADS_PER_STEP = 2
MOE_BLOCK = 512
MOE_TILE = 256
RUN_ALIGN = 8
NEG_BIG = -1e30


def _cparams(sem):
    return pltpu.CompilerParams(dimension_semantics=sem, vmem_limit_bytes=VMEM_LIMIT_BYTES)


def _silu(x):
    return x * jax.nn.sigmoid(x)


def _dot(a, b):
    return jnp.dot(a, b, preferred_element_type=F32)


def _dot_nt(a, b):
    return lax.dot_general(a, b, (((1,), (1,)), ((), ())), preferred_element_type=F32)


def _dot_tn(a, b):
    return lax.dot_general(a, b, (((0,), (0,)), ((), ())), preferred_element_type=F32)


def _pack_pair(lo, hi):
    lo_bits = lax.bitcast_convert_type(lo.astype(BF16).astype(F32), I32)
    hi_bits = lax.bitcast_convert_type(hi.astype(BF16).astype(F32), I32)
    return lax.shift_right_logical(lo_bits, 16) | (hi_bits & jnp.int32(-65536))


def _pack_exact_pair(lo, hi):
    lo_bits = lax.bitcast_convert_type(lo, I32)
    hi_bits = lax.bitcast_convert_type(hi, I32)
    return lax.shift_right_logical(lo_bits, 16) | (hi_bits & jnp.int32(-65536))


def _unpack_pair(w):
    lo = lax.bitcast_convert_type(lax.shift_left(w, 16), F32)
    hi = lax.bitcast_convert_type(w & jnp.int32(-65536), F32)
    return lo, hi


def _ada_kernel(cond_ref, w_ref, b_ref, o_ref):
    s = _silu(cond_ref[...])
    o_ref[...] = jnp.dot(s, w_ref[...], precision=HIGHEST, preferred_element_type=F32) + b_ref[...]


def _ada_mod(cond, ada_w, ada_b):
    depth = ada_w.shape[0]
    rows = cond.shape[0]
    nblk = ada_w.shape[2] // D_MODEL
    return pl.pallas_call(
        _ada_kernel,
        grid=(depth, nblk),
        in_specs=[
            pl.BlockSpec((rows, D_MODEL), lambda l, j: (0, 0)),
            pl.BlockSpec((None, D_MODEL, D_MODEL), lambda l, j: (l, 0, j)),
            pl.BlockSpec((None, 1, D_MODEL), lambda l, j: (l, 0, j)),
        ],
        out_specs=pl.BlockSpec((None, rows, D_MODEL), lambda l, j: (l, 0, j)),
        out_shape=jax.ShapeDtypeStruct((depth, rows, nblk * D_MODEL), F32),
        compiler_params=_cparams(("parallel", "parallel")),
        name="ada_mod",
    )(cond, ada_w, ada_b.reshape(depth, 1, -1))


def _norm_mod(x, gain, shift, scale):
    ms = jnp.mean(x * x, axis=-1, keepdims=True)
    h = x * lax.rsqrt(ms + EPS) * gain
    return h * (1.0 + scale) + shift


def _seg_rms(acc, gain, bd):
    sq = acc * acc
    hi = sq.astype(BF16)
    lo = (sq - hi.astype(F32)).astype(BF16)
    ms = _dot(hi, bd) + _dot(lo, bd)
    return acc * lax.rsqrt(ms + EPS) * gain


def _rope(y, c, sm, sp):
    return y * c + pltpu.roll(y, LANES - 16, 1) * sm + pltpu.roll(y, 16, 1) * sp


def _rope_tables(n):
    pos = jnp.arange(n, dtype=I32)
    row = (pos // GRID_W).astype(F32)
    col = (pos % GRID_W).astype(F32)
    axis_dim = HEAD_DIM // 2
    inv_freq = ROPE_THETA ** (-jnp.arange(0, axis_dim, 2, dtype=F32) / axis_dim)
    ang_row = row[:, None] * inv_freq
    ang_col = col[:, None] * inv_freq
    lane = jnp.arange(LANES)
    p = lane % axis_dim
    f = p % (axis_dim // 2)
    on_row = ((lane % HEAD_DIM) // axis_dim) == 0
    ang = jnp.where(on_row[None, :], ang_row[:, f], ang_col[:, f])
    c = jnp.cos(ang)
    s = jnp.sin(ang)
    first = (p < axis_dim // 2)[None, :]
    return c, jnp.where(first, -s, 0.0), jnp.where(first, 0.0, s)


def _seg_mean_matrix():
    r = jnp.arange(LANES)
    return jnp.where((r[:, None] // HEAD_DIM) == (r[None, :] // HEAD_DIM), 1.0 / HEAD_DIM, 0.0).astype(BF16)


def _even_proj_kernel(*refs, rope, layer_slot):
    if rope:
        (x_ref, shift_ref, scale_ref, gain_ref, w_ref, qkg_ref, lbf_ref, lbb_ref, bd_ref, rc_ref, rm_ref, rp_ref,
         dq_ref, dk_ref, dv_ref, hq_ref, kf_ref, gf_ref, kb_ref, gb_ref, hv_ref, hg_ref) = refs
        tables = (rc_ref[...], rm_ref[...], rp_ref[...])
    else:
        (x_ref, shift_ref, scale_ref, gain_ref, w_ref, qkg_ref, lbf_ref, lbb_ref, bd_ref,
         dq_ref, dk_ref, dv_ref, hq_ref, kf_ref, gf_ref, kb_ref, gb_ref, hv_ref, hg_ref) = refs
        tables = None
    hb = _norm_mod(x_ref[...], gain_ref[...], shift_ref[...], scale_ref[...]).astype(BF16)
    bd = bd_ref[...]
    width = 4 * LANES

    def proj(group):
        return _dot(hb, w_ref[:, group * width:(group + 1) * width])

    def qk(group, gain, out_ref, mult):
        acc = proj(group)
        for s in range(4):
            y = _seg_rms(acc[:, s * LANES:(s + 1) * LANES], gain, bd)
            if tables is not None:
                y = _rope(y, *tables)
            out_ref[:, s * LANES:(s + 1) * LANES] = (y * mult).astype(BF16)

    qk(0, qkg_ref[0:1, :], dq_ref, ATTN_SCALE)
    qk(1, qkg_ref[1:2, :], dk_ref, 1.0)
    dv_ref[...] = proj(2).astype(BF16)
    hq_ref[...] = _silu(proj(3)).astype(BF16)

    def forget(group, lb_ref, k_ref, g_ref):
        raw = lb_ref[...]
        e = jnp.exp(raw - jnp.max(raw, axis=0, keepdims=True))
        lb = jnp.sum(e[0:layer_slot + 1, :], axis=0, keepdims=True) / jnp.sum(e, axis=0, keepdims=True)
        f = lb + (1.0 - lb) * jax.nn.sigmoid(proj(group))
        k_ref[...] = (1.0 - f).astype(BF16)
        g_ref[...] = jnp.log(f)

    forget(4, lbf_ref, kf_ref, gf_ref)
    forget(5, lbb_ref, kb_ref, gb_ref)
    hv_ref[...] = proj(6).astype(BF16)
    hg_ref[...] = _silu(proj(7)).astype(BF16)


def _row_spec(tm, width):
    return pl.BlockSpec((None, tm, width), lambda b, i: (b, i, 0))


def _bcast_spec(width):
    return pl.BlockSpec((None, 1, width), lambda b, i: (b, 0, 0))


def _const_spec(shape):
    nd = len(shape)
    return pl.BlockSpec(shape, lambda b, i: (0,) * nd)


def _even_proj(x, shift, scale, gain, w, qk_gain, lb_fwd, lb_bwd, layer_slot, rope):
    bsz, n, d = x.shape
    tm = min(PROJ_ROWS, n)
    width = 4 * LANES
    qkg = jnp.tile(qk_gain.astype(F32), (1, 2))
    in_specs = [
        _row_spec(tm, d), _bcast_spec(d), _bcast_spec(d), _const_spec((1, d)), _const_spec(w.shape),
        _const_spec((2, LANES)), _const_spec(lb_fwd.shape), _const_spec(lb_bwd.shape), _const_spec((LANES, LANES)),
    ]
    args = [x, shift, scale, gain.reshape(1, d), w, qkg, lb_fwd, lb_bwd, _seg_mean_matrix()]
    if rope:
        tab_spec = pl.BlockSpec((tm, LANES), lambda b, i: (i, 0))
        in_specs += [tab_spec] * 3
        args += list(_rope_tables(n))
    out_dtypes = [BF16, BF16, BF16, BF16, BF16, F32, BF16, F32, BF16, BF16]
    return pl.pallas_call(
        functools.partial(_even_proj_kernel, rope=rope, layer_slot=layer_slot),
        grid=(bsz, n // tm),
        in_specs=in_specs,
        out_specs=[_row_spec(tm, width)] * len(out_dtypes),
        out_shape=[jax.ShapeDtypeStruct((bsz, n, width), dt) for dt in out_dtypes],
        compiler_params=_cparams(("parallel", "parallel")),
        name="even_proj_x" if rope else "even_proj_ctx",
    )(*args)


def _odd_proj_kernel(*refs, rope):
    if rope:
        (x_ref, shift_ref, scale_ref, gain_ref, w_ref, gqg_ref, nag_ref, bd_ref, rc_ref, rm_ref, rp_ref,
         gq_ref, gkv_ref, nq_ref, nk_ref, nv_ref) = refs
        tables = (rc_ref[...], rm_ref[...], rp_ref[...])
    else:
        (x_ref, shift_ref, scale_ref, gain_ref, w_ref, gqg_ref, nag_ref, bd_ref,
         gq_ref, gkv_ref, nq_ref, nk_ref, nv_ref) = refs
        tables = None
    hb = _norm_mod(x_ref[...], gain_ref[...], shift_ref[...], scale_ref[...]).astype(BF16)
    bd = bd_ref[...]

    def slab(acc, s, gain, use_rope, mult):
        y = _seg_rms(acc[:, s * LANES:(s + 1) * LANES], gain, bd)
        if use_rope and tables is not None:
            y = _rope(y, *tables)
        return (y * mult).astype(BF16)

    q_w = GQA_HEADS * HEAD_DIM
    acc = _dot(hb, w_ref[:, 0:q_w])
    for s in range(q_w // LANES):
        gq_ref[:, s * LANES:(s + 1) * LANES] = slab(acc, s, gqg_ref[0:1, :], True, ATTN_SCALE)
    acc = _dot(hb, w_ref[:, q_w:q_w + 2 * LANES])
    gkv_ref[:, 0:LANES] = slab(acc, 0, gqg_ref[1:2, :], True, 1.0)
    gkv_ref[:, LANES:2 * LANES] = acc[:, LANES:2 * LANES].astype(BF16)
    base = q_w + 2 * LANES
    na_w = NA_HEADS * HEAD_DIM
    acc = _dot(hb, w_ref[:, base:base + na_w])
    for s in range(na_w // LANES):
        nq_ref[:, s * LANES:(s + 1) * LANES] = slab(acc, s, nag_ref[0:1, :], False, ATTN_SCALE)
    acc = _dot(hb, w_ref[:, base + na_w:base + 2 * na_w])
    for s in range(na_w // LANES):
        nk_ref[:, s * LANES:(s + 1) * LANES] = slab(acc, s, nag_ref[1:2, :], False, 1.0)
    nv_ref[...] = _dot(hb, w_ref[:, base + 2 * na_w:base + 3 * na_w]).astype(BF16)


def _odd_proj(x, shift, scale, gain, w, gqa_gain, na_gain, rope):
    bsz, n, d = x.shape
    tm = min(PROJ_ROWS, n)
    gqg = jnp.tile(gqa_gain.astype(F32), (1, 2))
    nag = jnp.tile(na_gain.astype(F32), (1, 2))
    in_specs = [
        _row_spec(tm, d), _bcast_spec(d), _bcast_spec(d), _const_spec((1, d)), _const_spec(w.shape),
        _const_spec((2, LANES)), _const_spec((2, LANES)), _const_spec((LANES, LANES)),
    ]
    args = [x, shift, scale, gain.reshape(1, d), w, gqg, nag, _seg_mean_matrix()]
    if rope:
        tab_spec = pl.BlockSpec((tm, LANES), lambda b, i: (i, 0))
        in_specs += [tab_spec] * 3
        args += list(_rope_tables(n))
    widths = [GQA_HEADS * HEAD_DIM, 2 * LANES, NA_HEADS * HEAD_DIM, NA_HEADS * HEAD_DIM, NA_HEADS * HEAD_DIM]
    return pl.pallas_call(
        functools.partial(_odd_proj_kernel, rope=rope),
        grid=(bsz, n // tm),
        in_specs=in_specs,
        out_specs=[_row_spec(tm, wd) for wd in widths],
        out_shape=[jax.ShapeDtypeStruct((bsz, n, wd), BF16) for wd in widths],
        compiler_params=_cparams(("parallel", "parallel")),
        name="odd_proj_x" if rope else "odd_proj_ctx",
    )(*args)


def _flash_scratch(rows):
    return [pltpu.VMEM((rows, FLASH_TK), F32),
            pltpu.VMEM((rows, FLASH_TK), F32),
            pltpu.VMEM((rows, 1), F32),
            pltpu.VMEM((rows, 2 * LANES), F32)]


def _flash(qs, sources, scratch):
    s0_ref, s1_ref, m_ref, acc_ref = scratch
    s_refs = (s0_ref, s1_ref)
    rows = qs.shape[0]
    tk = s0_ref.shape[1]
    m_ref[...] = jnp.full(m_ref.shape, -jnp.inf, F32)
    acc_ref[...] = jnp.zeros(acc_ref.shape, F32)

    def issue(slot, width, k):
        s_refs[slot][:, 0:width] = _dot_nt(qs, k)

    def consume(slot, width, v):
        s = s_refs[slot][:, 0:width]
        m_old = m_ref[...]
        m_new = jnp.maximum(m_old, jnp.max(s, axis=-1, keepdims=True))
        alpha = jnp.exp(m_old - m_new)
        p = jnp.exp((s - m_new).astype(BF16))
        ones = (lax.broadcasted_iota(I32, (width, LANES), 1) == 0).astype(BF16)
        m_ref[...] = m_new
        acc_ref[...] = alpha * acc_ref[...] + _dot(p, jnp.concatenate([v, ones], axis=1))

    issued = 0
    prev = None
    for k_ref, v_ref, length in sorted(sources, key=lambda src: src[2]):
        chunk = min(tk, length)
        steps = length // chunk
        if steps == 1:
            slot = issued % 2
            issue(slot, chunk, k_ref[...])
            if prev is not None:
                consume(prev[0], prev[1], prev[2]())
            prev = (slot, chunk, lambda v_ref=v_ref: v_ref[...])
            issued += 1
            continue
        assert steps % 2 == 0 and chunk == tk
        base = issued % 2

        def kv(ref, c):
            return ref[pl.ds(pl.multiple_of(c * tk, tk), tk), :]

        issue(base, tk, kv(k_ref, 0))
        if prev is not None:
            consume(prev[0], prev[1], prev[2]())

        def body(j, carry, k_ref=k_ref, v_ref=v_ref, base=base):
            issue(1 - base, tk, kv(k_ref, 2 * j + 1))
            consume(base, tk, kv(v_ref, 2 * j))
            issue(base, tk, kv(k_ref, 2 * j + 2))
            consume(1 - base, tk, kv(v_ref, 2 * j + 1))
            return carry

        lax.fori_loop(0, steps // 2 - 1, body, 0)
        issue(1 - base, tk, kv(k_ref, steps - 1))
        consume(base, tk, kv(v_ref, steps - 2))
        prev = (1 - base, tk, lambda v_ref=v_ref, steps=steps: kv(v_ref, steps - 1))
        issued += steps
    consume(prev[0], prev[1], prev[2]())
    return acc_ref[:, 0:LANES] / acc_ref[:, LANES:LANES + 1]


def _lane_ids(shape):
    return lax.broadcasted_iota(I32, shape, len(shape) - 1)


def _diff_attn_kernel(*refs, n_src, lens, lam_init):
    q_ref = refs[0]
    kv_refs = refs[1:1 + 2 * n_src]
    lam_ref, gain_ref, o_ref = refs[1 + 2 * n_src:4 + 2 * n_src]
    scratch = refs[4 + 2 * n_src:]
    q = q_ref[...]
    tq = q.shape[0]
    lo = _lane_ids(q.shape) < HEAD_DIM
    zero = jnp.zeros_like(q)
    qs = jnp.concatenate([jnp.where(lo, q, zero), jnp.where(lo, zero, q)], axis=0)
    sources = [(kv_refs[2 * i], kv_refs[2 * i + 1], lens[i]) for i in range(n_src)]
    a = _flash(qs, sources, scratch)
    lp = lam_ref[...]
    lam = (jnp.exp(jnp.sum(lp[0:1, :] * lp[1:2, :], axis=-1, keepdims=True))
           - jnp.exp(jnp.sum(lp[2:3, :] * lp[3:4, :], axis=-1, keepdims=True)) + lam_init)
    o = a[0:tq, :] - lam * a[tq:2 * tq, :]
    ms = jnp.mean(o * o, axis=-1, keepdims=True)
    o = o * lax.rsqrt(ms + EPS) * gain_ref[...] * (1.0 - lam_init)
    o_ref[...] = o.astype(BF16)


def _diff_attention(q, kv_list, lam_params, out_gain, lam_init):
    bsz, n, width = q.shape
    tq = min(DIFF_TQ, n)
    lens = tuple(k.shape[1] for k, _ in kv_list)
    in_specs = [pl.BlockSpec((None, tq, LANES), lambda b, h, i: (b, i, h))]
    args = [q]
    for (k, v), length in zip(kv_list, lens):
        spec = pl.BlockSpec((None, length, LANES), lambda b, h, i: (b, 0, h))
        in_specs += [spec, spec]
        args += [k, v]
    in_specs += [pl.BlockSpec(lam_params.shape, lambda b, h, i: (0, 0)),
                 pl.BlockSpec((1, LANES), lambda b, h, i: (0, 0))]
    args += [lam_params.astype(F32), out_gain.reshape(1, LANES).astype(F32)]
    return pl.pallas_call(
        functools.partial(_diff_attn_kernel, n_src=len(kv_list), lens=lens, lam_init=lam_init),
        grid=(bsz, DIFF_HEADS, n // tq),
        in_specs=in_specs,
        out_specs=pl.BlockSpec((None, tq, LANES), lambda b, h, i: (b, i, h)),
        out_shape=jax.ShapeDtypeStruct((bsz, n, width), BF16),
        scratch_shapes=_flash_scratch(2 * tq),
        compiler_params=_cparams(("parallel", "parallel", "parallel")),
        name="diff_attn",
    )(*args)


def _gqa_kernel(q_ref, kc_ref, vc_ref, kx_ref, vx_ref, o_ref, *scratch, lens):
    tq = q_ref.shape[0]
    sources = [(kc_ref, vc_ref, lens[0]), (kx_ref, vx_ref, lens[1])]
    lanes = _lane_ids((tq, LANES))
    for kv in range(GQA_KV_HEADS):
        mine = (lanes // HEAD_DIM) == kv
        rows = []
        for half in range(2):
            hh = q_ref[:, (2 * kv + half) * LANES:(2 * kv + half + 1) * LANES]
            sw = pltpu.roll(hh.astype(F32), HEAD_DIM, 1).astype(BF16)
            zero = jnp.zeros_like(hh)
            a_here, b_here = (hh, sw) if kv == 0 else (sw, hh)
            rows += [jnp.where(mine, a_here, zero), jnp.where(mine, b_here, zero)]
        qs = jnp.concatenate(rows, axis=0)
        o = _flash(qs, sources, scratch)
        for half in range(2):
            oa = o[(2 * half) * tq:(2 * half + 1) * tq, :]
            ob = o[(2 * half + 1) * tq:(2 * half + 2) * tq, :]
            oa_sw = pltpu.roll(oa, HEAD_DIM, 1)
            ob_sw = pltpu.roll(ob, HEAD_DIM, 1)
            if kv == 0:
                res = jnp.where(lanes < HEAD_DIM, oa, ob_sw)
            else:
                res = jnp.where(lanes < HEAD_DIM, oa_sw, ob)
            o_ref[:, (2 * kv + half) * LANES:(2 * kv + half + 1) * LANES] = res.astype(BF16)


def _gqa_attention(q, kv_c, kv_x):
    bsz, n, width = q.shape
    tq = min(GQA_TQ, n)
    lens = (kv_c.shape[1], kv_x.shape[1])

    def kspec(length, col):
        return pl.BlockSpec((None, length, LANES), lambda b, i, col=col: (b, 0, col))

    return pl.pallas_call(
        functools.partial(_gqa_kernel, lens=lens),
        grid=(bsz, n // tq),
        in_specs=[pl.BlockSpec((None, tq, width), lambda b, i: (b, i, 0)),
                  kspec(lens[0], 0), kspec(lens[0], 1), kspec(lens[1], 0), kspec(lens[1], 1)],
        out_specs=pl.BlockSpec((None, tq, width), lambda b, i: (b, i, 0)),
        out_shape=jax.ShapeDtypeStruct((bsz, n, width), BF16),
        scratch_shapes=_flash_scratch(4 * tq),
        compiler_params=_cparams(("parallel", "parallel")),
        name="gqa_attn",
    )(q, kv_c, kv_c, kv_x, kv_x)


def _gla_chunks(chains, lt, ut):
    c, s16 = HGRN_CHUNK, HGRN_SUB
    nsub = c // s16
    ridx = lax.broadcasted_iota(I32, (c, 1), 0)
    sidx = lax.broadcasted_iota(I32, (s16, 1), 0)
    cums = [jnp.dot(ut if rev else lt, g, precision=HIGHEST, preferred_element_type=F32)
            for _, _, g, _, _, rev in chains]
    tots = [cum[0:1, :] if ch[5] else cum[c - 1:c, :] for ch, cum in zip(chains, cums)]
    vbs = [ch[3].astype(BF16) for ch in chains]
    out_states = [_dot_nt((ch[0] * jnp.exp(cum)).astype(BF16), ch[4].astype(BF16))
                  for ch, cum in zip(chains, cums)]
    new_sts = [ch[4] * jnp.exp(tot) + _dot_tn(vb, (ch[1] * jnp.exp(tot - cum)).astype(BF16))
               for ch, cum, tot, vb in zip(chains, cums, tots, vbs)]
    blocks = [[None] * nsub for _ in chains]
    for i in range(nsub):
        starts = [c - s16 * (i + 1) if ch[5] else s16 * i for ch in chains]
        accs = [jnp.zeros((s16, ch[3].shape[1]), F32) for ch in chains]
        if i > 0:
            scores = []
            for (q, k, _, _, _, rev), cum, r0 in zip(chains, cums, starts):
                if rev:
                    bnd = cum[r0 + s16:r0 + s16 + 1, :]
                    prev = ridx >= r0 + s16
                else:
                    bnd = cum[r0 - 1:r0, :]
                    prev = ridx < r0
                qt = (q[r0:r0 + s16] * jnp.exp(cum[r0:r0 + s16] - bnd)).astype(BF16)
                kt = (k * jnp.exp(jnp.where(prev, bnd - cum, -jnp.inf))).astype(BF16)
                scores.append(_dot_nt(qt, kt).astype(BF16))
            accs = [_dot(a, vb) for a, vb in zip(scores, vbs)]
        for s in range(s16):
            for ci, ((q, k, _, v, _, rev), cum, r0) in enumerate(zip(chains, cums, starts)):
                cum_i = cum[r0:r0 + s16]
                valid = (sidx <= s) if rev else (sidx >= s)
                d = jnp.where(valid, cum_i - cum_i[s:s + 1, :], -jnp.inf)
                a = jnp.sum(q[r0:r0 + s16] * k[r0 + s:r0 + s + 1, :] * jnp.exp(d), axis=-1, keepdims=True)
                accs[ci] = accs[ci] + a * v[r0 + s:r0 + s + 1, :]
        for ci, ch in enumerate(chains):
            blocks[ci][nsub - 1 - i if ch[5] else i] = accs[ci]
    return [(o + jnp.concatenate(b, axis=0), st) for o, b, st in zip(out_states, blocks, new_sts)]


def _hgrn_kernel(qx_ref, kfx_ref, gfx_ref, kbx_ref, gbx_ref, vx_ref, sgx_ref,
                 qc_ref, kfc_ref, gfc_ref, kbc_ref, gbc_ref, vc_ref, sgc_ref,
                 gain_ref, lt_ref, ut_ref, ox_ref, oc_ref, accx_ref, accc_ref):
    c = HGRN_CHUNK
    lt, ut = lt_ref[...], ut_ref[...]
    heads = qx_ref.shape[1] // LANES

    def sweep(q_ref, kf_ref, gf_ref, kb_ref, gb_ref, v_ref, acc_ref, states):
        nchunks = q_ref.shape[0] // c
        acc_ref[...] = jnp.zeros(acc_ref.shape, F32)

        def body(i, carry):
            rf = pl.multiple_of(i * c, c)
            rb = pl.multiple_of((nchunks - 1 - i) * c, c)
            chains = []
            for h in range(heads):
                st_f, st_b = carry[h]
                cols = slice(h * LANES, (h + 1) * LANES)

                def rows(ref, r0, cols=cols):
                    return ref[pl.ds(r0, c), cols].astype(F32)

                chains.append((rows(q_ref, rf), rows(kf_ref, rf), rows(gf_ref, rf), rows(v_ref, rf), st_f, False))
                chains.append((rows(q_ref, rb), rows(kb_ref, rb), rows(gb_ref, rb), rows(v_ref, rb), st_b, True))
            res = _gla_chunks(chains, lt, ut)
            for h in range(heads):
                cols = slice(h * LANES, (h + 1) * LANES)
                acc_ref[pl.ds(rf, c), cols] += res[2 * h][0]
                acc_ref[pl.ds(rb, c), cols] += res[2 * h + 1][0]
            return tuple((res[2 * h][1], res[2 * h + 1][1]) for h in range(heads))

        return lax.fori_loop(0, nchunks, body, states)

    zero = jnp.zeros((LANES, LANES), F32)
    states = sweep(qc_ref, kfc_ref, gfc_ref, kbc_ref, gbc_ref, vc_ref, accc_ref,
                   tuple((zero, zero) for _ in range(heads)))
    sweep(qx_ref, kfx_ref, gfx_ref, kbx_ref, gbx_ref, vx_ref, accx_ref, states)

    def finish(acc_ref, sg_ref, o_ref):
        rows = acc_ref.shape[0]
        tile = min(rows, 512)

        def body(i, _):
            r = pl.multiple_of(i * tile, tile)
            for h in range(heads):
                cols = slice(h * LANES, (h + 1) * LANES)
                o = acc_ref[pl.ds(r, tile), cols]
                ms = jnp.mean(o * o, axis=-1, keepdims=True)
                o = o * lax.rsqrt(ms + EPS) * gain_ref[...]
                o_ref[pl.ds(r, tile), cols] = (o * sg_ref[pl.ds(r, tile), cols].astype(F32)).astype(BF16)
            return 0

        lax.fori_loop(0, rows // tile, body, 0)

    finish(accx_ref, sgx_ref, ox_ref)
    finish(accc_ref, sgc_ref, oc_ref)


def _hgrn(px, pc, out_gain):
    bsz, n, width = px[0].shape
    m = pc[0].shape[1]
    c = HGRN_CHUNK
    r = jnp.arange(c)
    lt = (r[:, None] >= r[None, :]).astype(F32)
    ut = (r[:, None] <= r[None, :]).astype(F32)

    bw = HGRN_HEADS_PER_STEP * LANES

    def spec(length):
        return pl.BlockSpec((None, length, bw), lambda b, h: (b, 0, h))

    const = lambda shape: pl.BlockSpec(shape, lambda b, h: (0, 0))
    return pl.pallas_call(
        _hgrn_kernel,
        grid=(bsz, width // bw),
        in_specs=[spec(n)] * 7 + [spec(m)] * 7 + [const((1, LANES)), const((c, c)), const((c, c))],
        out_specs=[spec(n), spec(m)],
        out_shape=[jax.ShapeDtypeStruct((bsz, n, width), BF16), jax.ShapeDtypeStruct((bsz, m, width), BF16)],
        scratch_shapes=[pltpu.VMEM((n, bw), F32), pltpu.VMEM((m, bw), F32)],
        compiler_params=_cparams(("parallel", "parallel")),
        name="hgrn2",
    )(*px, *pc, out_gain.reshape(1, LANES).astype(F32), lt, ut)


def _na_bias_tables(rpb, rows):
    qrows, band, heads = NA_QROWS, NA_BAND, rpb.shape[0]
    pad = GRID_W - NA_COLS
    wide = jnp.pad(rpb.astype(F32), ((0, 0), (0, 0), (pad, pad)))
    toeplitz = jnp.stack([wide[:, :, GRID_W - 1 - qc:2 * GRID_W - 1 - qc] for qc in range(GRID_W)], axis=2)
    qc = np.arange(GRID_W)[:, None]
    kc = np.arange(GRID_W)[None, :]
    cstart = np.clip(qc - NA_COLS // 2, 0, GRID_W - NA_COLS)
    col_ok = (kc >= cstart) & (kc < cstart + NA_COLS)
    toeplitz = jnp.where(col_ok[None, None], toeplitz, NEG_BIG)
    tabs = []
    for r0 in (0, qrows, rows - qrows):
        rs = min(max(r0 - NA_ROWS // 2, 0), rows - band)
        qr = r0 + np.arange(qrows)[:, None]
        kr = rs + np.arange(band)[None, :]
        rstart = np.clip(qr - NA_ROWS // 2, 0, rows - NA_ROWS)
        row_ok = (kr >= rstart) & (kr < rstart + NA_ROWS)
        dr = np.clip(kr - qr + NA_ROWS - 1, 0, 2 * NA_ROWS - 2)
        tiles = jnp.take(toeplitz, jnp.asarray(dr.reshape(-1), I32), axis=1)
        tiles = jnp.where(row_ok.reshape(1, -1, 1, 1), tiles, NEG_BIG)
        tiles = tiles.reshape(heads, qrows, band, GRID_W, GRID_W).transpose(0, 1, 3, 2, 4)
        tabs.append(tiles.reshape(heads, qrows * GRID_W, band * GRID_W))
    return jnp.stack(tabs)


def _na_kernel(q_ref, k_ref, v_ref, kc_ref, vc_ref, bias_ref, o_ref, *, rows):
    j = pl.program_id(2)
    tq = q_ref.shape[0]
    band = NA_BAND * GRID_W
    rs = jnp.clip(j * NA_QROWS - NA_ROWS // 2, 0, rows - NA_BAND)
    start = pl.multiple_of(rs * GRID_W, NA_ROWS // 2 * GRID_W)
    q = q_ref[...]
    lo = _lane_ids(q.shape) < HEAD_DIM
    zero = jnp.zeros_like(q)
    qh = [jnp.where(lo, q, zero), jnp.where(lo, zero, q)]
    kb = k_ref[pl.ds(start, band), :]
    kc = kc_ref[...]

    def with_ones(v):
        ones = (lax.broadcasted_iota(I32, v.shape, 1) == 0).astype(BF16)
        return jnp.concatenate([v, ones], axis=1)

    vb = with_ones(v_ref[pl.ds(start, band), :])
    vc = with_ones(vc_ref[...])
    s_win = [_dot_nt(qh[h], kb) + bias_ref[h] for h in range(2)]
    s_ctx = [_dot_nt(qh[h], kc) for h in range(2)]
    m = [jnp.maximum(jnp.max(s_win[h], axis=-1, keepdims=True), jnp.max(s_ctx[h], axis=-1, keepdims=True))
         for h in range(2)]
    p_win = [jnp.exp((s_win[h] - m[h]).astype(BF16)) for h in range(2)]
    p_ctx = [jnp.exp((s_ctx[h] - m[h]).astype(BF16)) for h in range(2)]
    pv = [_dot(p_ctx[h], vc) + _dot(p_win[h], vb) for h in range(2)]
    o = [pv[h][:, 0:LANES] / pv[h][:, LANES:LANES + 1] for h in range(2)]
    o_ref[...] = jnp.where(lo, o[0], o[1]).astype(BF16)


def _na_attention(q, k, v, kc, vc, rpb):
    bsz, n, width = q.shape
    rows = n // GRID_W
    tq = NA_QROWS * GRID_W
    nt = n // tq
    bias = _na_bias_tables(rpb, rows)
    m = kc.shape[1]

    def cls(j):
        return jnp.where(j == 0, 0, jnp.where(j == nt - 1, 2, 1))

    full = lambda length: pl.BlockSpec((None, length, LANES), lambda b, h, j: (b, 0, h))
    return pl.pallas_call(
        functools.partial(_na_kernel, rows=rows),
        grid=(bsz, NA_HEADS // 2, nt),
        in_specs=[pl.BlockSpec((None, tq, LANES), lambda b, h, j: (b, j, h)),
                  full(n), full(n), full(m), full(m),
                  pl.BlockSpec((None, 2, tq, NA_BAND * GRID_W), lambda b, h, j: (cls(j), h, 0, 0))],
        out_specs=pl.BlockSpec((None, tq, LANES), lambda b, h, j: (b, j, h)),
        out_shape=jax.ShapeDtypeStruct((bsz, n, width), BF16),
        compiler_params=_cparams(("parallel", "parallel", "parallel")),
        name="na_attn",
    )(q, k, v, kc, vc, bias)


def _out_proj_kernel(a_ref, b_ref, w_ref, x_ref, gate_ref, gain_ref, shift_ref, scale_ref, r_ref,
                     x1_ref, tok_ref, logit_ref):
    half = a_ref.shape[1]
    y = _dot(a_ref[...], w_ref[0:half, :]) + _dot(b_ref[...], w_ref[half:2 * half, :])
    x1 = x_ref[...] + gate_ref[...] * y
    x1_ref[...] = x1
    h = _norm_mod(x1, gain_ref[...], shift_ref[...], scale_ref[...])
    logit_ref[...] = jnp.dot(h, r_ref[...], precision=HIGHEST, preferred_element_type=F32)
    tok_ref[...] = _pack_pair(h[:, 0:HALF], h[:, HALF:2 * HALF])


def _out_proj(a, b, w, x, gate, gain, shift, scale, router_pad):
    bsz, n, d = x.shape
    tm = min(PROJ_ROWS, n)
    return pl.pallas_call(
        _out_proj_kernel,
        grid=(bsz, n // tm),
        in_specs=[_row_spec(tm, a.shape[2]), _row_spec(tm, b.shape[2]), _const_spec(w.shape), _row_spec(tm, d),
                  _bcast_spec(d), _const_spec((1, d)), _bcast_spec(d), _bcast_spec(d), _const_spec(router_pad.shape)],
        out_specs=[_row_spec(tm, d), _row_spec(tm, HALF), _row_spec(tm, LANES)],
        out_shape=[jax.ShapeDtypeStruct((bsz, n, d), F32), jax.ShapeDtypeStruct((bsz, n, HALF), I32),
                   jax.ShapeDtypeStruct((bsz, n, LANES), F32)],
        compiler_params=_cparams(("parallel", "parallel")),
        name="out_proj",
    )(a, b, w, x, gate, gain.reshape(1, d), shift, scale, router_pad)


def _route_kernel(logit_ref, bias_ref, tri_ref, etri_ref, w_ref, loc_ref, tcnt_ref, tcarry_ref, toff_ref, cnt_ref,
                  masked_ref, carry_ref):
    step = pl.program_id(0)

    @pl.when(step == 0)
    def _():
        carry_ref[...] = jnp.zeros_like(carry_ref)

    tr = logit_ref.shape[0]
    scores = jax.nn.sigmoid(logit_ref[...].T[0:N_EXPERTS, :])
    biased = scores + bias_ref[...]
    gsz = EXPERTS_PER_GROUP
    sub = lax.broadcasted_iota(I32, (gsz, tr), 0).astype(F32)
    gscore = []
    for g in range(N_GROUPS):
        bg = biased[g * gsz:(g + 1) * gsz, :]
        m1 = jnp.max(bg, axis=0, keepdims=True)
        i1 = jnp.min(jnp.where(bg == m1, sub, float(gsz)), axis=0, keepdims=True)
        m2 = jnp.max(jnp.where(sub == i1, -jnp.inf, bg), axis=0, keepdims=True)
        gscore.append(m1 + m2)
    for g in range(N_GROUPS):
        beaten = jnp.zeros((1, tr), F32)
        for o in range(N_GROUPS):
            if o == g:
                continue
            wins = (gscore[o] >= gscore[g]) if o < g else (gscore[o] > gscore[g])
            beaten = beaten + jnp.where(wins, 1.0, 0.0)
        keep = beaten < float(TOPK_GROUPS)
        masked_ref[g * gsz:(g + 1) * gsz, :] = jnp.where(keep, biased[g * gsz:(g + 1) * gsz, :], -jnp.inf)
    cur = masked_ref[...]
    eid = lax.broadcasted_iota(I32, (N_EXPERTS, tr), 0).astype(F32)
    sel = jnp.zeros((N_EXPERTS, tr), F32)
    picks, weights = [], []
    for _ in range(TOP_K):
        m = jnp.max(cur, axis=0, keepdims=True)
        ik = jnp.min(jnp.where(cur == m, eid, float(N_EXPERTS)), axis=0, keepdims=True)
        hit = eid == ik
        weights.append(jnp.sum(jnp.where(hit, scores, 0.0), axis=0, keepdims=True))
        sel = sel + jnp.where(hit, 1.0, 0.0)
        cur = jnp.where(hit, -jnp.inf, cur)
        picks.append(ik)
    wsum = weights[0]
    for wk in weights[1:]:
        wsum = wsum + wk
    tile_cnt = jnp.broadcast_to(jnp.sum(sel, axis=1, keepdims=True), (N_EXPERTS, LANES))
    tile_cnt = jnp.floor((tile_cnt + (RUN_ALIGN - 1)) * (1.0 / RUN_ALIGN)) * RUN_ALIGN
    tile_off = jnp.dot(etri_ref[...], tile_cnt, precision=HIGHEST, preferred_element_type=F32)
    row = _dot(sel.astype(BF16), tri_ref[...]) + tile_off[:, 0:1]
    for kk in range(TOP_K):
        w_ref[kk:kk + 1, :] = weights[kk] / wsum * ROUTED_SCALE
        loc_ref[kk:kk + 1, :] = jnp.sum(jnp.where(eid == picks[kk], row, 0.0), axis=0, keepdims=True).astype(I32)
    tcnt_ref[...] = tile_cnt.astype(I32)
    tcarry_ref[...] = carry_ref[...].astype(I32)
    toff_ref[...] = tile_off.astype(I32)
    carry_ref[...] = carry_ref[...] + tile_cnt
    cnt_ref[...] = carry_ref[...].astype(I32)


def _route(logits, router_bias):
    t = logits.shape[0]
    tr = MOE_TILE
    assert t % tr == 0
    nt = t // tr
    r = jnp.arange(tr)
    tri = (r[:, None] < r[None, :]).astype(BF16)
    e = jnp.arange(N_EXPERTS)
    etri = (e[:, None] > e[None, :]).astype(F32)
    kt_spec = pl.BlockSpec((TOP_K, tr), lambda i: (0, i))
    tile_spec = pl.BlockSpec((None, N_EXPERTS, LANES), lambda i: (i, 0, 0))
    tile_shape = jax.ShapeDtypeStruct((nt, N_EXPERTS, LANES), I32)
    return pl.pallas_call(
        _route_kernel,
        grid=(nt,),
        in_specs=[pl.BlockSpec((tr, LANES), lambda i: (i, 0)),
                  pl.BlockSpec((N_EXPERTS, 1), lambda i: (0, 0)),
                  pl.BlockSpec((tr, tr), lambda i: (0, 0)),
                  pl.BlockSpec((N_EXPERTS, N_EXPERTS), lambda i: (0, 0))],
        out_specs=[kt_spec, kt_spec, tile_spec, tile_spec, tile_spec,
                   pl.BlockSpec((N_EXPERTS, LANES), lambda i: (0, 0))],
        out_shape=[jax.ShapeDtypeStruct((TOP_K, t), F32), jax.ShapeDtypeStruct((TOP_K, t), I32),
                   tile_shape, tile_shape, tile_shape, jax.ShapeDtypeStruct((N_EXPERTS, LANES), I32)],
        scratch_shapes=[pltpu.VMEM((N_EXPERTS, tr), F32), pltpu.VMEM((N_EXPERTS, LANES), F32)],
        compiler_params=_cparams(("arbitrary",)),
        name="moe_route",
    )(logits, router_bias.astype(F32).reshape(N_EXPERTS, 1), tri, etri)


TAB_WORDS = 1024
TAB_FIELD = 128
N_RUNS = N_EXPERTS + 1
TILE_ROWS = -(-(MOE_TILE * TOP_K + N_EXPERTS * (RUN_ALIGN - 1)) // MOE_TILE) * MOE_TILE
FILLER_ROWS = TILE_ROWS - MOE_TILE * TOP_K
RUN_PIECE = 64


def _run_copies(tab_ref, local_ref, global_ref, sem, to_global):
    def piece(ls, gs, off, rows, priority):
        loc = local_ref.at[pl.ds(pl.multiple_of(ls + off, RUN_ALIGN), rows)]
        glo = global_ref.at[pl.ds(pl.multiple_of(gs + off, RUN_ALIGN), rows)]
        src, dst = (loc, glo) if to_global else (glo, loc)
        pltpu.make_async_copy(src, dst, sem).start(priority=priority)

    def body(e, carry):
        gs = tab_ref[e]
        c = tab_ref[TAB_FIELD + e]
        ls = tab_ref[2 * TAB_FIELD + e]
        whole = c >> (RUN_PIECE.bit_length() - 1)

        def big(j, carry2):
            piece(ls, gs, j * RUN_PIECE, RUN_PIECE, 0)
            return carry2

        lax.fori_loop(0, whole, big, 0)
        for b in range(RUN_ALIGN.bit_length() - 1, RUN_PIECE.bit_length() - 1):
            @pl.when(((c >> b) & 1) == 1)
            def _(b=b):
                piece(ls, gs, whole * RUN_PIECE + (c & ((1 << b) - 1) & (RUN_PIECE - 1)), 1 << b, 1)
        return carry

    lax.fori_loop(0, N_RUNS, body, 0)


def _run_wait(local_ref, global_ref, sem):
    pltpu.make_async_copy(global_ref.at[pl.ds(0, TILE_ROWS)], local_ref, sem).wait()


def _dispatch_kernel(tab_ref, loc_ref, tok_ref, rows_in_ref, rows_ref, srt_ref, sem, *, n_tiles):
    del rows_in_ref
    i = pl.program_id(0)
    slot = i % 2
    tt = tok_ref.shape[0]
    lo, hi = _unpack_pair(tok_ref[...])
    lo, hi = lo.astype(BF16), hi.astype(BF16)
    loc = loc_ref[...]
    blk = loc >> (tt.bit_length() - 1)
    low = (loc & (tt - 1)).astype(F32).astype(BF16)
    r = lax.broadcasted_iota(I32, (tt, tt), 0).astype(F32).astype(BF16)
    one = jnp.ones((tt, tt), BF16)
    for rb in range(TILE_ROWS // tt):
        pb = jnp.zeros((tt, tt), BF16)
        for kk in range(TOP_K):
            lk = jnp.where(blk[kk:kk + 1, :] == rb, low[kk:kk + 1, :], jnp.asarray(-1.0, BF16))
            pb = jnp.where(lk == r, one, pb)
        srt_ref[slot, rb * tt:(rb + 1) * tt, :] = _pack_exact_pair(_dot(pb, lo), _dot(pb, hi))

    @pl.when(i > 0)
    def _():
        _run_wait(srt_ref.at[1 - slot], rows_ref, sem.at[1 - slot])

    _run_copies(tab_ref, srt_ref.at[slot], rows_ref, sem.at[slot], True)

    @pl.when(i == n_tiles - 1)
    def _():
        _run_wait(srt_ref.at[slot], rows_ref, sem.at[slot])


def _dispatch(tab, loc, tok, rows_buf):
    t = tok.shape[0]
    tt = MOE_TILE
    return pl.pallas_call(
        functools.partial(_dispatch_kernel, n_tiles=t // tt),
        grid=(t // tt,),
        in_specs=[pl.BlockSpec((TAB_WORDS,), lambda i: (i,), memory_space=pltpu.SMEM),
                  pl.BlockSpec((TOP_K, tt), lambda i: (0, i)),
                  pl.BlockSpec((tt, HALF), lambda i: (i, 0)),
                  pl.BlockSpec(memory_space=pl.ANY)],
        out_specs=pl.BlockSpec(memory_space=pl.ANY),
        out_shape=jax.ShapeDtypeStruct(rows_buf.shape, rows_buf.dtype),
        scratch_shapes=[pltpu.VMEM((2, TILE_ROWS, HALF), I32), pltpu.SemaphoreType.DMA((2,))],
        input_output_aliases={3: 0},
        compiler_params=_cparams(("arbitrary",)),
        name="moe_dispatch",
    )(tab, loc, tok, rows_buf)


def _expert_kernel(be_ref, nused_ref, x_ref, wgu_ref, wd_ref, y_ref):
    @pl.when(pl.program_id(0) < nused_ref[0])
    def _():
        lo, hi = _unpack_pair(x_ref[...])
        gu = _dot(lo.astype(BF16), wgu_ref[0:HALF, :]) + _dot(hi.astype(BF16), wgu_ref[HALF:2 * HALF, :])
        h = (_silu(gu[:, 0:EXPERT_DIM]) * gu[:, EXPERT_DIM:2 * EXPERT_DIM]).astype(BF16)
        y = _dot(h, wd_ref[...])
        y_ref[...] = _pack_pair(y[:, 0:HALF], y[:, HALF:2 * HALF])

    @pl.when(pl.program_id(0) >= nused_ref[0])
    def _():
        y_ref[...] = jnp.zeros_like(y_ref)


def _experts(block_expert, nused, rows, wgu, wd):
    n_rows = rows.shape[0]
    nb = n_rows // MOE_BLOCK

    def row_map(i, be, nu):
        return (jnp.minimum(i, nu[0] - 1), 0)

    def w_map(i, be, nu):
        return (be[jnp.minimum(i, nu[0] - 1)], 0, 0)

    grid_spec = pltpu.PrefetchScalarGridSpec(
        num_scalar_prefetch=2,
        grid=(nb,),
        in_specs=[pl.BlockSpec((MOE_BLOCK, HALF), row_map),
                  pl.BlockSpec((None, D_MODEL, 2 * EXPERT_DIM), w_map),
                  pl.BlockSpec((None, EXPERT_DIM, D_MODEL), w_map)],
        out_specs=pl.BlockSpec((MOE_BLOCK, HALF), lambda i, be, nu: (i, 0)),
    )
    return pl.pallas_call(
        _expert_kernel,
        grid_spec=grid_spec,
        out_shape=jax.ShapeDtypeStruct((n_rows, HALF), I32),
        compiler_params=_cparams(("arbitrary",)),
        name="moe_experts",
    )(block_expert, nused, rows, wgu, wd)


def _combine_kernel(tab_ref, tabn_ref, x1_ref, tok_ref, loc_ref, w_ref, gate_ref, wgu_ref, wd_ref, y_ref, o_ref,
                    buf_ref, sem, *, n_tiles):
    i = pl.program_id(0)
    slot = i % 2
    tt = tok_ref.shape[0]

    @pl.when(i == 0)
    def _():
        _run_copies(tab_ref, buf_ref.at[slot], y_ref, sem.at[slot], False)

    @pl.when(i + 1 < n_tiles)
    def _():
        _run_copies(tabn_ref, buf_ref.at[1 - slot], y_ref, sem.at[1 - slot], False)

    lo, hi = _unpack_pair(tok_ref[...])
    gu = _dot(lo.astype(BF16), wgu_ref[0:HALF, :]) + _dot(hi.astype(BF16), wgu_ref[HALF:2 * HALF, :])
    h = (_silu(gu[:, 0:EXPERT_DIM]) * gu[:, EXPERT_DIM:2 * EXPERT_DIM]).astype(BF16)
    shared = _dot(h, wd_ref[...])
    _run_wait(buf_ref.at[slot], y_ref, sem.at[slot])
    acc_lo = shared[:, 0:HALF]
    acc_hi = shared[:, HALF:2 * HALF]
    loc = loc_ref[...]
    w = w_ref[...].astype(BF16)
    blk = loc >> (tt.bit_length() - 1)
    low = (loc & (tt - 1)).astype(F32).astype(BF16)
    r = lax.broadcasted_iota(I32, (tt, tt), 1).astype(F32).astype(BF16)
    for rb in range(TILE_ROWS // tt):
        wb = jnp.zeros((tt, tt), BF16)
        for kk in range(TOP_K):
            lk = jnp.where(blk[:, kk:kk + 1] == rb, low[:, kk:kk + 1], jnp.asarray(-1.0, BF16))
            wb = jnp.where(lk == r, jnp.broadcast_to(w[:, kk:kk + 1], (tt, tt)), wb)
        ylo, yhi = _unpack_pair(buf_ref[slot, rb * tt:(rb + 1) * tt, :])
        acc_lo = acc_lo + _dot(wb, ylo.astype(BF16))
        acc_hi = acc_hi + _dot(wb, yhi.astype(BF16))
    gate = gate_ref[...]
    o_ref[:, 0:HALF] = x1_ref[:, 0:HALF] + gate[:, 0:HALF] * acc_lo
    o_ref[:, HALF:2 * HALF] = x1_ref[:, HALF:2 * HALF] + gate[:, HALF:2 * HALF] * acc_hi


def _combine(tab, x1, tok, loc_tok, w_tok, gate, wgu, wd, y_rows, tokens_per_gate):
    t, d = x1.shape
    tt = MOE_TILE
    nt = t // tt
    per = tokens_per_gate // tt
    return pl.pallas_call(
        functools.partial(_combine_kernel, n_tiles=nt),
        grid=(nt,),
        in_specs=[pl.BlockSpec((TAB_WORDS,), lambda i: (i,), memory_space=pltpu.SMEM),
                  pl.BlockSpec((TAB_WORDS,), lambda i: (jnp.minimum(i + 1, nt - 1),), memory_space=pltpu.SMEM),
                  pl.BlockSpec((tt, d), lambda i: (i, 0)),
                  pl.BlockSpec((tt, HALF), lambda i: (i, 0)),
                  pl.BlockSpec((tt, TOP_K), lambda i: (i, 0)),
                  pl.BlockSpec((tt, TOP_K), lambda i: (i, 0)),
                  pl.BlockSpec((None, 1, d), lambda i: (i // per, 0, 0)),
                  pl.BlockSpec(wgu.shape, lambda i: (0, 0)),
                  pl.BlockSpec(wd.shape, lambda i: (0, 0)),
                  pl.BlockSpec(memory_space=pl.ANY)],
        out_specs=pl.BlockSpec((tt, d), lambda i: (i, 0)),
        out_shape=jax.ShapeDtypeStruct((t, d), F32),
        scratch_shapes=[pltpu.VMEM((2, TILE_ROWS, HALF), I32), pltpu.SemaphoreType.DMA((2,))],
        compiler_params=_cparams(("arbitrary",)),
        name="moe_combine",
    )(tab, tab, x1, tok, loc_tok, w_tok, gate, wgu, wd, y_rows)


def _moe(parts, router_bias, w_gate, w_up, w_down, ws_gate, ws_up, ws_down, row_buf=None):
    logits = jnp.concatenate([p[2].reshape(-1, LANES) for p in parts], axis=0)
    t = logits.shape[0]
    w, loc, tile_cnt, tile_carry, tile_off, cnt = _route(logits, router_bias)
    counts = cnt[:, 0]
    padded = (counts + MOE_BLOCK - 1) // MOE_BLOCK * MOE_BLOCK
    pad_end = jnp.cumsum(padded)
    pad_start = pad_end - padded
    nt = t // MOE_TILE
    max_aligned = t * TOP_K + nt * N_EXPERTS * (RUN_ALIGN - 1)
    area_rows = -(-max_aligned // MOE_BLOCK) * MOE_BLOCK + N_EXPERTS * MOE_BLOCK
    n_rows = -(-(area_rows + FILLER_ROWS) // MOE_BLOCK) * MOE_BLOCK
    if row_buf is not None:
        assert row_buf.shape[0] >= n_rows
        n_rows = row_buf.shape[0]
    tile_rows_used = tile_off[:, N_EXPERTS - 1, 0] + tile_cnt[:, N_EXPERTS - 1, 0]

    def field(per_expert, filler):
        vals = jnp.concatenate([per_expert.astype(I32), filler.astype(I32)[:, None]], axis=1)
        return jnp.pad(vals, ((0, 0), (0, TAB_FIELD - N_RUNS)))

    tab = jnp.concatenate(
        [field(pad_start[None, :] + tile_carry[:, :, 0], jnp.full((nt,), area_rows, I32)),
         field(tile_cnt[:, :, 0], TILE_ROWS - tile_rows_used),
         field(tile_off[:, :, 0], tile_rows_used),
         jnp.zeros((nt, TAB_WORDS - 3 * TAB_FIELD), I32)], axis=1).reshape(-1)
    w_tok = w.T
    loc_tok = loc.T
    nb = n_rows // MOE_BLOCK
    block_start = jnp.arange(nb, dtype=I32) * MOE_BLOCK
    block_expert = jnp.minimum(jnp.sum(block_start[:, None] >= pad_end[None, :], axis=1), N_EXPERTS - 1).astype(I32)
    nused = (pad_end[-1] // MOE_BLOCK).astype(I32).reshape(1)
    rows = jnp.zeros((n_rows, HALF), I32) if row_buf is None else row_buf
    off = 0
    for x1, tok, _, _, _ in parts:
        cnt_tok = tok.shape[0] * tok.shape[1]
        part_tab = tab[off // MOE_TILE * TAB_WORDS:(off + cnt_tok) // MOE_TILE * TAB_WORDS]
        rows = _dispatch(part_tab, loc[:, off:off + cnt_tok], tok.reshape(cnt_tok, HALF), rows)
        off += cnt_tok
    wgu = jnp.concatenate([w_gate, w_up], axis=-1).astype(BF16)
    y_rows = _experts(block_expert, nused, rows, wgu, w_down.astype(BF16))
    wsgu = jnp.concatenate([ws_gate, ws_up], axis=-1).astype(BF16)
    wsd = ws_down.astype(BF16)
    outs = []
    off = 0
    for x1, tok, _, gate, per in parts:
        cnt_tok = tok.shape[0] * tok.shape[1]
        part_tab = tab[off // MOE_TILE * TAB_WORDS:(off + cnt_tok) // MOE_TILE * TAB_WORDS]
        o = _combine(part_tab, x1.reshape(cnt_tok, D_MODEL), tok.reshape(cnt_tok, HALF),
                     loc_tok[off:off + cnt_tok], w_tok[off:off + cnt_tok], gate, wsgu, wsd, y_rows, per)
        outs.append(o.reshape(x1.shape))
        off += cnt_tok
    return outs, rows


def kernel(x, c, ctx, c_ctx, ada_w, ada_b, norm_mix, norm_ffn, ev_w_in, ev_w_out, diff_qk_gain, diff_lambda,
           diff_out_gain, hgrn_lb, hgrn_out_gain, od_w_in, od_w_out, gqa_qk_gain, na_qk_gain, na_rpb, moe_router,
           moe_router_bias, moe_w_gate, moe_w_up, moe_w_down, shared_w_gate, shared_w_up, shared_w_down):
    bsz, n, d = x.shape
    m = ctx.shape[1]
    depth = ada_w.shape[0]
    cond_rows = -(-(bsz + 1) // 8) * 8
    cond = jnp.zeros((cond_rows, d), F32).at[0:bsz].set(c).at[bsz].set(c_ctx)
    mods = _ada_mod(cond, ada_w, ada_b)

    xc = ctx
    row_buf = None
    for layer in range(depth):
        need_ctx = layer < depth - 1
        j = layer // 2
        mod = mods[layer].reshape(cond_rows, 6, d)
        mx = [mod[0:bsz, i][:, None, :] for i in range(6)]
        mc = [jnp.broadcast_to(mod[bsz:bsz + 1, i][:, None, :], (bsz, 1, d)) for i in range(6)]
        if layer % 2 == 0:
            w_in = ev_w_in[j].astype(BF16)
            px = _even_proj(x, mx[0], mx[1], norm_mix[layer], w_in, diff_qk_gain[j], hgrn_lb[0], hgrn_lb[1], j, True)
            pc = _even_proj(xc, mc[0], mc[1], norm_mix[layer], w_in, diff_qk_gain[j], hgrn_lb[0], hgrn_lb[1], j, False)
            lam_init = 0.8 - 0.6 * math.exp(-0.3 * layer)
            a_x = _diff_attention(px[0], [(pc[1], pc[2]), (px[1], px[2])], diff_lambda[j], diff_out_gain[j], lam_init)
            a_c = _diff_attention(pc[0], [(pc[1], pc[2])], diff_lambda[j], diff_out_gain[j], lam_init)
            b_x, b_c = _hgrn(px[3:], pc[3:], hgrn_out_gain[j])
            w_out = ev_w_out[j].astype(BF16)
        else:
            w_in = od_w_in[j].astype(BF16)
            px = _odd_proj(x, mx[0], mx[1], norm_mix[layer], w_in, gqa_qk_gain[j], na_qk_gain[j], True)
            pc = _odd_proj(xc, mc[0], mc[1], norm_mix[layer], w_in, gqa_qk_gain[j], na_qk_gain[j], False)
            a_x = _gqa_attention(px[0], pc[1], px[1])
            b_x = _na_attention(px[2], px[3], px[4], pc[3], pc[4], na_rpb[j])
            a_c = b_c = None
            w_out = od_w_out[j].astype(BF16)
        router_pad = jnp.zeros((d, LANES), F32).at[:, 0:N_EXPERTS].set(moe_router[layer].astype(F32))
        x1, tok_x, logit_x = _out_proj(a_x, b_x, w_out, x, mx[2], norm_ffn[layer], mx[3], mx[4], router_pad)
        parts = [(x1, tok_x, logit_x, mx[5], n)]
        if need_ctx:
            xc1, tok_c, logit_c = _out_proj(a_c, b_c, w_out, xc, mc[2], norm_ffn[layer], mc[3], mc[4], router_pad)
            parts.append((xc1, tok_c, logit_c, mc[5][0:1], bsz * m))
        outs, row_buf = _moe(parts, moe_router_bias[layer], moe_w_gate[layer], moe_w_up[layer], moe_w_down[layer],
                             shared_w_gate[layer], shared_w_up[layer], shared_w_down[layer], row_buf)
        x = outs[0]
        if need_ctx:
            xc = outs[1]
    return x
```

```python
import functools
import math

import jax
import jax.numpy as jnp
import numpy as np
from jax import lax
from jax.experimental import pallas as pl
from jax.experimental.pallas import tpu as pltpu

F32 = jnp.float32
BF16 = jnp.bfloat16
I32 = jnp.int32
HIGHEST = lax.Precision.HIGHEST

D_MODEL = 1024
GRID_W = 64
HEAD_DIM = 64
ATTN_SCALE = HEAD_DIM ** -0.5
ROPE_THETA = 10000.0
EPS = 1e-6
DIFF_HEADS = D_MODEL // 256
HGRN_HEADS = D_MODEL // 256
HGRN_CHUNK = 64
HGRN_SUB = 16
GQA_HEADS = D_MODEL // 128
GQA_KV_HEADS = GQA_HEADS // 4
NA_HEADS = D_MODEL // 128
NA_ROWS = 8
NA_COLS = 16
N_EXPERTS = 64
N_GROUPS = 8
EXPERTS_PER_GROUP = N_EXPERTS // N_GROUPS
TOPK_GROUPS = 4
TOP_K = 8
EXPERT_DIM = D_MODEL // 4
ROUTED_SCALE = 2.5
HALF = D_MODEL // 2

LANES = 128
VMEM_LIMIT_BYTES = 56 * 1024 * 1024
PROJ_ROWS = 512
FLASH_TK = 2048
FLASH_SB = 64
DIFF_TQ = 512
GQA_TQ = 256
NA_QROWS = 8
NA_BAND = 16
HGRN_HEADS_PER_STEP = 2
MOE_BLOCK = 512
MOE_TILE = 256
RUN_ALIGN = 8
NEG_BIG = -1e30


def _cparams(sem):
    return pltpu.CompilerParams(dimension_semantics=sem, vmem_limit_bytes=VMEM_LIMIT_BYTES)


def _silu(x):
    return x * jax.nn.sigmoid(x)


def _dot(a, b):
    return jnp.dot(a, b, preferred_element_type=F32)


def _dot_nt(a, b):
    return lax.dot_general(a, b, (((1,), (1,)), ((), ())), preferred_element_type=F32)


def _dot_tn(a, b):
    return lax.dot_general(a, b, (((0,), (0,)), ((), ())), preferred_element_type=F32)


def _pack_pair(lo, hi):
    lo_bits = lax.bitcast_convert_type(lo.astype(BF16).astype(F32), I32)
    hi_bits = lax.bitcast_convert_type(hi.astype(BF16).astype(F32), I32)
    return lax.shift_right_logical(lo_bits, 16) | (hi_bits & jnp.int32(-65536))


def _pack_exact_pair(lo, hi):
    lo_bits = lax.bitcast_convert_type(lo, I32)
    hi_bits = lax.bitcast_convert_type(hi, I32)
    return lax.shift_right_logical(lo_bits, 16) | (hi_bits & jnp.int32(-65536))


def _unpack_pair(w):
    lo = lax.bitcast_convert_type(lax.shift_left(w, 16), F32)
    hi = lax.bitcast_convert_type(w & jnp.int32(-65536), F32)
    return lo, hi


def _ada_kernel(cond_ref, w_ref, b_ref, o_ref):
    s = _silu(cond_ref[...])
    o_ref[...] = jnp.dot(s, w_ref[...], precision=HIGHEST, preferred_element_type=F32) + b_ref[...]


def _ada_mod(cond, ada_w, ada_b):
    depth = ada_w.shape[0]
    rows = cond.shape[0]
    nblk = ada_w.shape[2] // D_MODEL
    return pl.pallas_call(
        _ada_kernel,
        grid=(depth, nblk),
        in_specs=[
            pl.BlockSpec((rows, D_MODEL), lambda l, j: (0, 0)),
            pl.BlockSpec((None, D_MODEL, D_MODEL), lambda l, j: (l, 0, j)),
            pl.BlockSpec((None, 1, D_MODEL), lambda l, j: (l, 0, j)),
        ],
        out_specs=pl.BlockSpec((None, rows, D_MODEL), lambda l, j: (l, 0, j)),
        out_shape=jax.ShapeDtypeStruct((depth, rows, nblk * D_MODEL), F32),
        compiler_params=_cparams(("parallel", "parallel")),
        name="ada_mod",
    )(cond, ada_w, ada_b.reshape(depth, 1, -1))


def _norm_mod(x, gain, shift, scale):
    ms = jnp.mean(x * x, axis=-1, keepdims=True)
    h = x * lax.rsqrt(ms + EPS) * gain
    return h * (1.0 + scale) + shift


def _seg_rms(acc, gain, bd):
    sq = acc * acc
    hi = sq.astype(BF16)
    lo = (sq - hi.astype(F32)).astype(BF16)
    ms = _dot(hi, bd) + _dot(lo, bd)
    return acc * lax.rsqrt(ms + EPS) * gain


def _rope(y, c, sm, sp):
    return y * c + pltpu.roll(y, LANES - 16, 1) * sm + pltpu.roll(y, 16, 1) * sp


def _rope_tables(n):
    pos = jnp.arange(n, dtype=I32)
    row = (pos // GRID_W).astype(F32)
    col = (pos % GRID_W).astype(F32)
    axis_dim = HEAD_DIM // 2
    inv_freq = ROPE_THETA ** (-jnp.arange(0, axis_dim, 2, dtype=F32) / axis_dim)
    ang_row = row[:, None] * inv_freq
    ang_col = col[:, None] * inv_freq
    lane = jnp.arange(LANES)
    p = lane % axis_dim
    f = p % (axis_dim // 2)
    on_row = ((lane % HEAD_DIM) // axis_dim) == 0
    ang = jnp.where(on_row[None, :], ang_row[:, f], ang_col[:, f])
    c = jnp.cos(ang)
    s = jnp.sin(ang)
    first = (p < axis_dim // 2)[None, :]
    return c, jnp.where(first, -s, 0.0), jnp.where(first, 0.0, s)


def _seg_mean_matrix():
    r = jnp.arange(LANES)
    return jnp.where((r[:, None] // HEAD_DIM) == (r[None, :] // HEAD_DIM), 1.0 / HEAD_DIM, 0.0).astype(BF16)


def _even_proj_kernel(*refs, rope, layer_slot):
    if rope:
        (x_ref, shift_ref, scale_ref, gain_ref, w_ref, qkg_ref, lbf_ref, lbb_ref, bd_ref, rc_ref, rm_ref, rp_ref,
         dq_ref, dk_ref, dv_ref, hq_ref, kf_ref, gf_ref, kb_ref, gb_ref, hv_ref, hg_ref) = refs
        tables = (rc_ref[...], rm_ref[...], rp_ref[...])
    else:
        (x_ref, shift_ref, scale_ref, gain_ref, w_ref, qkg_ref, lbf_ref, lbb_ref, bd_ref,
         dq_ref, dk_ref, dv_ref, hq_ref, kf_ref, gf_ref, kb_ref, gb_ref, hv_ref, hg_ref) = refs
        tables = None
    hb = _norm_mod(x_ref[...], gain_ref[...], shift_ref[...], scale_ref[...]).astype(BF16)
    bd = bd_ref[...]
    width = 4 * LANES

    def proj(group):
        return _dot(hb, w_ref[:, group * width:(group + 1) * width])

    def qk(group, gain, out_ref, mult):
        acc = proj(group)
        for s in range(4):
            y = _seg_rms(acc[:, s * LANES:(s + 1) * LANES], gain, bd)
            if tables is not None:
                y = _rope(y, *tables)
            out_ref[:, s * LANES:(s + 1) * LANES] = (y * mult).astype(BF16)

    qk(0, qkg_ref[0:1, :], dq_ref, ATTN_SCALE)
    qk(1, qkg_ref[1:2, :], dk_ref, 1.0)
    dv_ref[...] = proj(2).astype(BF16)
    hq_ref[...] = _silu(proj(3)).astype(BF16)

    def forget(group, lb_ref, k_ref, g_ref):
        raw = lb_ref[...]
        e = jnp.exp(raw - jnp.max(raw, axis=0, keepdims=True))
        lb = jnp.sum(e[0:layer_slot + 1, :], axis=0, keepdims=True) / jnp.sum(e, axis=0, keepdims=True)
        f = lb + (1.0 - lb) * jax.nn.sigmoid(proj(group))
        k_ref[...] = (1.0 - f).astype(BF16)
        g_ref[...] = jnp.log(f)

    forget(4, lbf_ref, kf_ref, gf_ref)
    forget(5, lbb_ref, kb_ref, gb_ref)
    hv_ref[...] = proj(6).astype(BF16)
    hg_ref[...] = _silu(proj(7)).astype(BF16)


def _row_spec(tm, width):
    return pl.BlockSpec((None, tm, width), lambda b, i: (b, i, 0))


def _bcast_spec(width):
    return pl.BlockSpec((None, 1, width), lambda b, i: (b, 0, 0))


def _const_spec(shape):
    nd = len(shape)
    return pl.BlockSpec(shape, lambda b, i: (0,) * nd)


def _even_proj(x, shift, scale, gain, w, qk_gain, lb_fwd, lb_bwd, layer_slot, rope):
    bsz, n, d = x.shape
    tm = min(PROJ_ROWS, n)
    width = 4 * LANES
    qkg = jnp.tile(qk_gain.astype(F32), (1, 2))
    in_specs = [
        _row_spec(tm, d), _bcast_spec(d), _bcast_spec(d), _const_spec((1, d)), _const_spec(w.shape),
        _const_spec((2, LANES)), _const_spec(lb_fwd.shape), _const_spec(lb_bwd.shape), _const_spec((LANES, LANES)),
    ]
    args = [x, shift, scale, gain.reshape(1, d), w, qkg, lb_fwd, lb_bwd, _seg_mean_matrix()]
    if rope:
        tab_spec = pl.BlockSpec((tm, LANES), lambda b, i: (i, 0))
        in_specs += [tab_spec] * 3
        args += list(_rope_tables(n))
    out_dtypes = [BF16, BF16, BF16, BF16, BF16, F32, BF16, F32, BF16, BF16]
    return pl.pallas_call(
        functools.partial(_even_proj_kernel, rope=rope, layer_slot=layer_slot),
        grid=(bsz, n // tm),
        in_specs=in_specs,
        out_specs=[_row_spec(tm, width)] * len(out_dtypes),
        out_shape=[jax.ShapeDtypeStruct((bsz, n, width), dt) for dt in out_dtypes],
        compiler_params=_cparams(("parallel", "parallel")),
        name="even_proj_x" if rope else "even_proj_ctx",
    )(*args)


def _odd_proj_kernel(*refs, rope):
    if rope:
        (x_ref, shift_ref, scale_ref, gain_ref, w_ref, gqg_ref, nag_ref, bd_ref, rc_ref, rm_ref, rp_ref,
         gq_ref, gkv_ref, nq_ref, nk_ref, nv_ref) = refs
        tables = (rc_ref[...], rm_ref[...], rp_ref[...])
    else:
        (x_ref, shift_ref, scale_ref, gain_ref, w_ref, gqg_ref, nag_ref, bd_ref,
         gq_ref, gkv_ref, nq_ref, nk_ref, nv_ref) = refs
        tables = None
    hb = _norm_mod(x_ref[...], gain_ref[...], shift_ref[...], scale_ref[...]).astype(BF16)
    bd = bd_ref[...]

    def slab(acc, s, gain, use_rope, mult):
        y = _seg_rms(acc[:, s * LANES:(s + 1) * LANES], gain, bd)
        if use_rope and tables is not None:
            y = _rope(y, *tables)
        return (y * mult).astype(BF16)

    q_w = GQA_HEADS * HEAD_DIM
    acc = _dot(hb, w_ref[:, 0:q_w])
    for s in range(q_w // LANES):
        gq_ref[:, s * LANES:(s + 1) * LANES] = slab(acc, s, gqg_ref[0:1, :], True, ATTN_SCALE)
    acc = _dot(hb, w_ref[:, q_w:q_w + 2 * LANES])
    gkv_ref[:, 0:LANES] = slab(acc, 0, gqg_ref[1:2, :], True, 1.0)
    gkv_ref[:, LANES:2 * LANES] = acc[:, LANES:2 * LANES].astype(BF16)
    base = q_w + 2 * LANES
    na_w = NA_HEADS * HEAD_DIM
    acc = _dot(hb, w_ref[:, base:base + na_w])
    for s in range(na_w // LANES):
        nq_ref[:, s * LANES:(s + 1) * LANES] = slab(acc, s, nag_ref[0:1, :], False, ATTN_SCALE)
    acc = _dot(hb, w_ref[:, base + na_w:base + 2 * na_w])
    for s in range(na_w // LANES):
        nk_ref[:, s * LANES:(s + 1) * LANES] = slab(acc, s, nag_ref[1:2, :], False, 1.0)
    nv_ref[...] = _dot(hb, w_ref[:, base + 2 * na_w:base + 3 * na_w]).astype(BF16)


def _odd_proj(x, shift, scale, gain, w, gqa_gain, na_gain, rope):
    bsz, n, d = x.shape
    tm = min(PROJ_ROWS, n)
    gqg = jnp.tile(gqa_gain.astype(F32), (1, 2))
    nag = jnp.tile(na_gain.astype(F32), (1, 2))
    in_specs = [
        _row_spec(tm, d), _bcast_spec(d), _bcast_spec(d), _const_spec((1, d)), _const_spec(w.shape),
        _const_spec((2, LANES)), _const_spec((2, LANES)), _const_spec((LANES, LANES)),
    ]
    args = [x, shift, scale, gain.reshape(1, d), w, gqg, nag, _seg_mean_matrix()]
    if rope:
        tab_spec = pl.BlockSpec((tm, LANES), lambda b, i: (i, 0))
        in_specs += [tab_spec] * 3
        args += list(_rope_tables(n))
    widths = [GQA_HEADS * HEAD_DIM, 2 * LANES, NA_HEADS * HEAD_DIM, NA_HEADS * HEAD_DIM, NA_HEADS * HEAD_DIM]
    return pl.pallas_call(
        functools.partial(_odd_proj_kernel, rope=rope),
        grid=(bsz, n // tm),
        in_specs=in_specs,
        out_specs=[_row_spec(tm, wd) for wd in widths],
        out_shape=[jax.ShapeDtypeStruct((bsz, n, wd), BF16) for wd in widths],
        compiler_params=_cparams(("parallel", "parallel")),
        name="odd_proj_x" if rope else "odd_proj_ctx",
    )(*args)


def _flash_scratch(rows):
    return [pltpu.VMEM((rows, FLASH_TK), F32),
            pltpu.VMEM((rows, FLASH_TK), F32),
            pltpu.VMEM((rows, 1), F32),
            pltpu.VMEM((rows, 2 * LANES), F32)]


def _flash(qs, sources, scratch):
    s0_ref, s1_ref, m_ref, acc_ref = scratch
    s_refs = (s0_ref, s1_ref)
    rows = qs.shape[0]
    tk = s0_ref.shape[1]
    m_ref[...] = jnp.full(m_ref.shape, -jnp.inf, F32)
    acc_ref[...] = jnp.zeros(acc_ref.shape, F32)

    def issue(slot, width, k):
        s_refs[slot][:, 0:width] = _dot_nt(qs, k)

    def consume(slot, width, v):
        s = s_refs[slot][:, 0:width]
        m_old = m_ref[...]
        m_new = jnp.maximum(m_old, jnp.max(s, axis=-1, keepdims=True))
        alpha = jnp.exp(m_old - m_new)
        p = jnp.exp((s - m_new).astype(BF16))
        ones = (lax.broadcasted_iota(I32, (width, LANES), 1) == 0).astype(BF16)
        m_ref[...] = m_new
        acc_ref[...] = alpha * acc_ref[...] + _dot(p, jnp.concatenate([v, ones], axis=1))

    issued = 0
    prev = None
    for k_ref, v_ref, length in sorted(sources, key=lambda src: src[2]):
        chunk = min(tk, length)
        steps = length // chunk
        if steps == 1:
            slot = issued % 2
            issue(slot, chunk, k_ref[...])
            if prev is not None:
                consume(prev[0], prev[1], prev[2]())
            prev = (slot, chunk, lambda v_ref=v_ref: v_ref[...])
            issued += 1
            continue
        assert steps % 2 == 0 and chunk == tk
        base = issued % 2

        def kv(ref, c):
            return ref[pl.ds(pl.multiple_of(c * tk, tk), tk), :]

        issue(base, tk, kv(k_ref, 0))
        if prev is not None:
            consume(prev[0], prev[1], prev[2]())

        def body(j, carry, k_ref=k_ref, v_ref=v_ref, base=base):
            issue(1 - base, tk, kv(k_ref, 2 * j + 1))
            consume(base, tk, kv(v_ref, 2 * j))
            issue(base, tk, kv(k_ref, 2 * j + 2))
            consume(1 - base, tk, kv(v_ref, 2 * j + 1))
            return carry

        lax.fori_loop(0, steps // 2 - 1, body, 0)
        issue(1 - base, tk, kv(k_ref, steps - 1))
        consume(base, tk, kv(v_ref, steps - 2))
        prev = (1 - base, tk, lambda v_ref=v_ref, steps=steps: kv(v_ref, steps - 1))
        issued += steps
    consume(prev[0], prev[1], prev[2]())
    return acc_ref[:, 0:LANES] / acc_ref[:, LANES:LANES + 1]


def _lane_ids(shape):
    return lax.broadcasted_iota(I32, shape, len(shape) - 1)


def _diff_attn_kernel(*refs, n_src, lens, lam_init):
    q_ref = refs[0]
    kv_refs = refs[1:1 + 2 * n_src]
    lam_ref, gain_ref, o_ref = refs[1 + 2 * n_src:4 + 2 * n_src]
    scratch = refs[4 + 2 * n_src:]
    q = q_ref[...]
    tq = q.shape[0]
    lo = _lane_ids(q.shape) < HEAD_DIM
    zero = jnp.zeros_like(q)
    qs = jnp.concatenate([jnp.where(lo, q, zero), jnp.where(lo, zero, q)], axis=0)
    sources = [(kv_refs[2 * i], kv_refs[2 * i + 1], lens[i]) for i in range(n_src)]
    a = _flash(qs, sources, scratch)
    lp = lam_ref[...]
    lam = (jnp.exp(jnp.sum(lp[0:1, :] * lp[1:2, :], axis=-1, keepdims=True))
           - jnp.exp(jnp.sum(lp[2:3, :] * lp[3:4, :], axis=-1, keepdims=True)) + lam_init)
    o = a[0:tq, :] - lam * a[tq:2 * tq, :]
    ms = jnp.mean(o * o, axis=-1, keepdims=True)
    o = o * lax.rsqrt(ms + EPS) * gain_ref[...] * (1.0 - lam_init)
    o_ref[...] = o.astype(BF16)


def _diff_attention(q, kv_list, lam_params, out_gain, lam_init):
    bsz, n, width = q.shape
    tq = min(DIFF_TQ, n)
    lens = tuple(k.shape[1] for k, _ in kv_list)
    in_specs = [pl.BlockSpec((None, tq, LANES), lambda b, h, i: (b, i, h))]
    args = [q]
    for (k, v), length in zip(kv_list, lens):
        spec = pl.BlockSpec((None, length, LANES), lambda b, h, i: (b, 0, h))
        in_specs += [spec, spec]
        args += [k, v]
    in_specs += [pl.BlockSpec(lam_params.shape, lambda b, h, i: (0, 0)),
                 pl.BlockSpec((1, LANES), lambda b, h, i: (0, 0))]
    args += [lam_params.astype(F32), out_gain.reshape(1, LANES).astype(F32)]
    return pl.pallas_call(
        functools.partial(_diff_attn_kernel, n_src=len(kv_list), lens=lens, lam_init=lam_init),
        grid=(bsz, DIFF_HEADS, n // tq),
        in_specs=in_specs,
        out_specs=pl.BlockSpec((None, tq, LANES), lambda b, h, i: (b, i, h)),
        out_shape=jax.ShapeDtypeStruct((bsz, n, width), BF16),
        scratch_shapes=_flash_scratch(2 * tq),
        compiler_params=_cparams(("parallel", "parallel", "parallel")),
        name="diff_attn",
    )(*args)


def _gqa_kernel(q_ref, kc_ref, vc_ref, kx_ref, vx_ref, o_ref, *scratch, lens):
    tq = q_ref.shape[0]
    sources = [(kc_ref, vc_ref, lens[0]), (kx_ref, vx_ref, lens[1])]
    lanes = _lane_ids((tq, LANES))
    for kv in range(GQA_KV_HEADS):
        mine = (lanes // HEAD_DIM) == kv
        rows = []
        for half in range(2):
            hh = q_ref[:, (2 * kv + half) * LANES:(2 * kv + half + 1) * LANES]
            sw = pltpu.roll(hh.astype(F32), HEAD_DIM, 1).astype(BF16)
            zero = jnp.zeros_like(hh)
            a_here, b_here = (hh, sw) if kv == 0 else (sw, hh)
            rows += [jnp.where(mine, a_here, zero), jnp.where(mine, b_here, zero)]
        qs = jnp.concatenate(rows, axis=0)
        o = _flash(qs, sources, scratch)
        for half in range(2):
            oa = o[(2 * half) * tq:(2 * half + 1) * tq, :]
            ob = o[(2 * half + 1) * tq:(2 * half + 2) * tq, :]
            oa_sw = pltpu.roll(oa, HEAD_DIM, 1)
            ob_sw = pltpu.roll(ob, HEAD_DIM, 1)
            if kv == 0:
                res = jnp.where(lanes < HEAD_DIM, oa, ob_sw)
            else:
                res = jnp.where(lanes < HEAD_DIM, oa_sw, ob)
            o_ref[:, (2 * kv + half) * LANES:(2 * kv + half + 1) * LANES] = res.astype(BF16)


def _gqa_attention(q, kv_c, kv_x):
    bsz, n, width = q.shape
    tq = min(GQA_TQ, n)
    lens = (kv_c.shape[1], kv_x.shape[1])

    def kspec(length, col):
        return pl.BlockSpec((None, length, LANES), lambda b, i, col=col: (b, 0, col))

    return pl.pallas_call(
        functools.partial(_gqa_kernel, lens=lens),
        grid=(bsz, n // tq),
        in_specs=[pl.BlockSpec((None, tq, width), lambda b, i: (b, i, 0)),
                  kspec(lens[0], 0), kspec(lens[0], 1), kspec(lens[1], 0), kspec(lens[1], 1)],
        out_specs=pl.BlockSpec((None, tq, width), lambda b, i: (b, i, 0)),
        out_shape=jax.ShapeDtypeStruct((bsz, n, width), BF16),
        scratch_shapes=_flash_scratch(4 * tq),
        compiler_params=_cparams(("parallel", "parallel")),
        name="gqa_attn",
    )(q, kv_c, kv_c, kv_x, kv_x)


GLA_LEVELS = tuple(HGRN_CHUNK >> (i + 1) for i in range(HGRN_CHUNK.bit_length() - 1))


def _gla_tables(reverse):
    c = HGRN_CHUNK
    p = np.arange(c)
    t, u = p[:, None], p[None, :]
    exps = [(u <= t), (u > t)]
    pair = [np.eye(c, dtype=bool)]
    target = []
    for b in GLA_LEVELS:
        later = (p // b) % 2 == 1
        bnd = (p // (2 * b)) * (2 * b) + b - 1
        e = np.where(later[:, None], (u > bnd[:, None]) & (u <= t), (u > t) & (u <= bnd[:, None]))
        exps.append(e)
        pair.append((t // (2 * b)) == (u // (2 * b)))
        target.append(later)
    flip = (lambda m: m[::-1, ::-1]) if reverse else (lambda m: m)
    emat = np.concatenate([flip(e) for e in exps], axis=0).astype(np.float32)
    pmat = np.concatenate([flip(m) for m in pair], axis=0).astype(np.float32)
    return emat, pmat


def _gla_chunks(chains, emats, pmats):
    c = HGRN_CHUNK
    ridx = lax.broadcasted_iota(I32, (c, 1), 0)
    xs = []
    for ch in chains:
        g_hi = ch[2].astype(BF16)
        g_lo = (ch[2] - g_hi.astype(F32)).astype(BF16)
        xs.append(jnp.exp(_dot(emats[ch[5]], g_hi) + _dot(emats[ch[5]], g_lo)))
    vbs = [ch[3].astype(BF16) for ch in chains]
    out_states = [_dot_nt((ch[0] * x[0:c]).astype(BF16), ch[4].astype(BF16)) for ch, x in zip(chains, xs)]
    new_sts = []
    for ch, x, vb in zip(chains, xs, vbs):
        last = 0 if ch[5] else c - 1
        new_sts.append(ch[4] * x[last:last + 1] + _dot_tn(vb, (ch[1] * x[c:2 * c]).astype(BF16)))
    scores = [pmats[ch[5]][0:c] * _dot_nt(ch[0].astype(BF16), ch[1].astype(BF16)) for ch in chains]
    for lvl in range(len(GLA_LEVELS)):
        for ci, (ch, x) in enumerate(zip(chains, xs)):
            xl = x[(2 + lvl) * c:(3 + lvl) * c]
            pos = (c - 1 - ridx) if ch[5] else ridx
            later = ((pos >> (GLA_LEVELS[lvl].bit_length() - 1)) & 1) == 1
            qt = jnp.where(later, ch[0] * xl, 0.0).astype(BF16)
            kt = jnp.where(later, 0.0, ch[1] * xl).astype(BF16)
            scores[ci] = scores[ci] + pmats[ch[5]][(1 + lvl) * c:(2 + lvl) * c] * _dot_nt(qt, kt)
    return [(o + _dot(a.astype(BF16), vb), st) for o, a, vb, st in zip(out_states, scores, vbs, new_sts)]


def _hgrn_kernel(qx_ref, kfx_ref, gfx_ref, kbx_ref, gbx_ref, vx_ref, sgx_ref,
                 qc_ref, kfc_ref, gfc_ref, kbc_ref, gbc_ref, vc_ref, sgc_ref,
                 gain_ref, ef_ref, eb_ref, pf_ref, pb_ref, ox_ref, oc_ref, accx_ref, accc_ref):
    c = HGRN_CHUNK
    emats = {False: ef_ref[...].astype(BF16), True: eb_ref[...].astype(BF16)}
    pmats = {False: pf_ref[...], True: pb_ref[...]}
    heads = qx_ref.shape[1] // LANES

    def sweep(q_ref, kf_ref, gf_ref, kb_ref, gb_ref, v_ref, acc_ref, states):
        nchunks = q_ref.shape[0] // c
        acc_ref[...] = jnp.zeros(acc_ref.shape, F32)

        def body(i, carry):
            rf = pl.multiple_of(i * c, c)
            rb = pl.multiple_of((nchunks - 1 - i) * c, c)
            chains = []
            for h in range(heads):
                st_f, st_b = carry[h]
                cols = slice(h * LANES, (h + 1) * LANES)

                def rows(ref, r0, cols=cols):
                    return ref[pl.ds(r0, c), cols].astype(F32)

                chains.append((rows(q_ref, rf), rows(kf_ref, rf), rows(gf_ref, rf), rows(v_ref, rf), st_f, False))
                chains.append((rows(q_ref, rb), rows(kb_ref, rb), rows(gb_ref, rb), rows(v_ref, rb), st_b, True))
            res = _gla_chunks(chains, emats, pmats)
            for h in range(heads):
                cols = slice(h * LANES, (h + 1) * LANES)
                acc_ref[pl.ds(rf, c), cols] += res[2 * h][0]
                acc_ref[pl.ds(rb, c), cols] += res[2 * h + 1][0]
            return tuple((res[2 * h][1], res[2 * h + 1][1]) for h in range(heads))

        return lax.fori_loop(0, nchunks, body, states)

    zero = jnp.zeros((LANES, LANES), F32)
    states = sweep(qc_ref, kfc_ref, gfc_ref, kbc_ref, gbc_ref, vc_ref, accc_ref,
                   tuple((zero, zero) for _ in range(heads)))
    sweep(qx_ref, kfx_ref, gfx_ref, kbx_ref, gbx_ref, vx_ref, accx_ref, states)

    def finish(acc_ref, sg_ref, o_ref):
        rows = acc_ref.shape[0]
        tile = min(rows, 512)

        def body(i, _):
            r = pl.multiple_of(i * tile, tile)
            for h in range(heads):
                cols = slice(h * LANES, (h + 1) * LANES)
                o = acc_ref[pl.ds(r, tile), cols]
                ms = jnp.mean(o * o, axis=-1, keepdims=True)
                o = o * lax.rsqrt(ms + EPS) * gain_ref[...]
                o_ref[pl.ds(r, tile), cols] = (o * sg_ref[pl.ds(r, tile), cols].astype(F32)).astype(BF16)
            return 0

        lax.fori_loop(0, rows // tile, body, 0)

    finish(accx_ref, sgx_ref, ox_ref)
    finish(accc_ref, sgc_ref, oc_ref)


def _hgrn(px, pc, out_gain):
    bsz, n, width = px[0].shape
    m = pc[0].shape[1]
    c = HGRN_CHUNK
    (ef, pf), (eb, pb) = _gla_tables(False), _gla_tables(True)

    bw = HGRN_HEADS_PER_STEP * LANES

    def spec(length):
        return pl.BlockSpec((None, length, bw), lambda b, h: (b, 0, h))

    const = lambda shape: pl.BlockSpec(shape, lambda b, h: (0, 0))
    return pl.pallas_call(
        _hgrn_kernel,
        grid=(bsz, width // bw),
        in_specs=[spec(n)] * 7 + [spec(m)] * 7 + [const((1, LANES)), const(ef.shape), const(eb.shape),
                                                   const(pf.shape), const(pb.shape)],
        out_specs=[spec(n), spec(m)],
        out_shape=[jax.ShapeDtypeStruct((bsz, n, width), BF16), jax.ShapeDtypeStruct((bsz, m, width), BF16)],
        scratch_shapes=[pltpu.VMEM((n, bw), F32), pltpu.VMEM((m, bw), F32)],
        compiler_params=_cparams(("parallel", "parallel")),
        name="hgrn2",
    )(*px, *pc, out_gain.reshape(1, LANES).astype(F32), ef, eb, pf, pb)


def _na_bias_tables(rpb, rows):
    qrows, band, heads = NA_QROWS, NA_BAND, rpb.shape[0]
    pad = GRID_W - NA_COLS
    wide = jnp.pad(rpb.astype(F32), ((0, 0), (0, 0), (pad, pad)))
    toeplitz = jnp.stack([wide[:, :, GRID_W - 1 - qc:2 * GRID_W - 1 - qc] for qc in range(GRID_W)], axis=2)
    qc = np.arange(GRID_W)[:, None]
    kc = np.arange(GRID_W)[None, :]
    cstart = np.clip(qc - NA_COLS // 2, 0, GRID_W - NA_COLS)
    col_ok = (kc >= cstart) & (kc < cstart + NA_COLS)
    toeplitz = jnp.where(col_ok[None, None], toeplitz, NEG_BIG)
    tabs = []
    for r0 in (0, qrows, rows - qrows):
        rs = min(max(r0 - NA_ROWS // 2, 0), rows - band)
        qr = r0 + np.arange(qrows)[:, None]
        kr = rs + np.arange(band)[None, :]
        rstart = np.clip(qr - NA_ROWS // 2, 0, rows - NA_ROWS)
        row_ok = (kr >= rstart) & (kr < rstart + NA_ROWS)
        dr = np.clip(kr - qr + NA_ROWS - 1, 0, 2 * NA_ROWS - 2)
        tiles = jnp.take(toeplitz, jnp.asarray(dr.reshape(-1), I32), axis=1)
        tiles = jnp.where(row_ok.reshape(1, -1, 1, 1), tiles, NEG_BIG)
        tiles = tiles.reshape(heads, qrows, band, GRID_W, GRID_W).transpose(0, 1, 3, 2, 4)
        tabs.append(tiles.reshape(heads, qrows * GRID_W, band * GRID_W))
    return jnp.stack(tabs)


def _na_kernel(q_ref, k_ref, v_ref, kc_ref, vc_ref, bias_ref, o_ref, *, rows):
    j = pl.program_id(2)
    tq = q_ref.shape[0]
    band = NA_BAND * GRID_W
    rs = jnp.clip(j * NA_QROWS - NA_ROWS // 2, 0, rows - NA_BAND)
    start = pl.multiple_of(rs * GRID_W, NA_ROWS // 2 * GRID_W)
    q = q_ref[...]
    lo = _lane_ids(q.shape) < HEAD_DIM
    zero = jnp.zeros_like(q)
    qh = [jnp.where(lo, q, zero), jnp.where(lo, zero, q)]
    kb = k_ref[pl.ds(start, band), :]
    kc = kc_ref[...]

    def with_ones(v):
        ones = (lax.broadcasted_iota(I32, v.shape, 1) == 0).astype(BF16)
        return jnp.concatenate([v, ones], axis=1)

    vb = with_ones(v_ref[pl.ds(start, band), :])
    vc = with_ones(vc_ref[...])
    s_win = [_dot_nt(qh[h], kb) + bias_ref[h] for h in range(2)]
    s_ctx = [_dot_nt(qh[h], kc) for h in range(2)]
    m = [jnp.maximum(jnp.max(s_win[h], axis=-1, keepdims=True), jnp.max(s_ctx[h], axis=-1, keepdims=True))
         for h in range(2)]
    p_win = [jnp.exp((s_win[h] - m[h]).astype(BF16)) for h in range(2)]
    p_ctx = [jnp.exp((s_ctx[h] - m[h]).astype(BF16)) for h in range(2)]
    pv = [_dot(p_ctx[h], vc) + _dot(p_win[h], vb) for h in range(2)]
    o = [pv[h][:, 0:LANES] / pv[h][:, LANES:LANES + 1] for h in range(2)]
    o_ref[...] = jnp.where(lo, o[0], o[1]).astype(BF16)


def _na_attention(q, k, v, kc, vc, rpb):
    bsz, n, width = q.shape
    rows = n // GRID_W
    tq = NA_QROWS * GRID_W
    nt = n // tq
    bias = _na_bias_tables(rpb, rows)
    m = kc.shape[1]

    def cls(j):
        return jnp.where(j == 0, 0, jnp.where(j == nt - 1, 2, 1))

    full = lambda length: pl.BlockSpec((None, length, LANES), lambda b, h, j: (b, 0, h))
    return pl.pallas_call(
        functools.partial(_na_kernel, rows=rows),
        grid=(bsz, NA_HEADS // 2, nt),
        in_specs=[pl.BlockSpec((None, tq, LANES), lambda b, h, j: (b, j, h)),
                  full(n), full(n), full(m), full(m),
                  pl.BlockSpec((None, 2, tq, NA_BAND * GRID_W), lambda b, h, j: (cls(j), h, 0, 0))],
        out_specs=pl.BlockSpec((None, tq, LANES), lambda b, h, j: (b, j, h)),
        out_shape=jax.ShapeDtypeStruct((bsz, n, width), BF16),
        compiler_params=_cparams(("parallel", "parallel", "parallel")),
        name="na_attn",
    )(q, k, v, kc, vc, bias)


def _out_proj_kernel(a_ref, b_ref, w_ref, x_ref, gate_ref, gain_ref, shift_ref, scale_ref, r_ref,
                     x1_ref, tok_ref, logit_ref):
    half = a_ref.shape[1]
    y = _dot(a_ref[...], w_ref[0:half, :]) + _dot(b_ref[...], w_ref[half:2 * half, :])
    x1 = x_ref[...] + gate_ref[...] * y
    x1_ref[...] = x1
    h = _norm_mod(x1, gain_ref[...], shift_ref[...], scale_ref[...])
    logit_ref[...] = jnp.dot(h, r_ref[...], precision=HIGHEST, preferred_element_type=F32)
    tok_ref[...] = _pack_pair(h[:, 0:HALF], h[:, HALF:2 * HALF])


def _out_proj(a, b, w, x, gate, gain, shift, scale, router_pad):
    bsz, n, d = x.shape
    tm = min(PROJ_ROWS, n)
    return pl.pallas_call(
        _out_proj_kernel,
        grid=(bsz, n // tm),
        in_specs=[_row_spec(tm, a.shape[2]), _row_spec(tm, b.shape[2]), _const_spec(w.shape), _row_spec(tm, d),
                  _bcast_spec(d), _const_spec((1, d)), _bcast_spec(d), _bcast_spec(d), _const_spec(router_pad.shape)],
        out_specs=[_row_spec(tm, d), _row_spec(tm, HALF), _row_spec(tm, LANES)],
        out_shape=[jax.ShapeDtypeStruct((bsz, n, d), F32), jax.ShapeDtypeStruct((bsz, n, HALF), I32),
                   jax.ShapeDtypeStruct((bsz, n, LANES), F32)],
        compiler_params=_cparams(("parallel", "parallel")),
        name="out_proj",
    )(a, b, w, x, gate, gain.reshape(1, d), shift, scale, router_pad)


def _route_kernel(logit_ref, bias_ref, tri_ref, etri_ref, w_ref, loc_ref, tcnt_ref, tcarry_ref, toff_ref, cnt_ref,
                  masked_ref, carry_ref):
    step = pl.program_id(0)

    @pl.when(step == 0)
    def _():
        carry_ref[...] = jnp.zeros_like(carry_ref)

    tr = logit_ref.shape[0]
    scores = jax.nn.sigmoid(logit_ref[...].T[0:N_EXPERTS, :])
    biased = scores + bias_ref[...]
    gsz = EXPERTS_PER_GROUP
    sub = lax.broadcasted_iota(I32, (gsz, tr), 0).astype(F32)
    gscore = []
    for g in range(N_GROUPS):
        bg = biased[g * gsz:(g + 1) * gsz, :]
        m1 = jnp.max(bg, axis=0, keepdims=True)
        i1 = jnp.min(jnp.where(bg == m1, sub, float(gsz)), axis=0, keepdims=True)
        m2 = jnp.max(jnp.where(sub == i1, -jnp.inf, bg), axis=0, keepdims=True)
        gscore.append(m1 + m2)
    for g in range(N_GROUPS):
        beaten = jnp.zeros((1, tr), F32)
        for o in range(N_GROUPS):
            if o == g:
                continue
            wins = (gscore[o] >= gscore[g]) if o < g else (gscore[o] > gscore[g])
            beaten = beaten + jnp.where(wins, 1.0, 0.0)
        keep = beaten < float(TOPK_GROUPS)
        masked_ref[g * gsz:(g + 1) * gsz, :] = jnp.where(keep, biased[g * gsz:(g + 1) * gsz, :], -jnp.inf)
    cur = masked_ref[...]
    eid = lax.broadcasted_iota(I32, (N_EXPERTS, tr), 0).astype(F32)
    sel = jnp.zeros((N_EXPERTS, tr), F32)
    picks, weights = [], []
    for _ in range(TOP_K):
        m = jnp.max(cur, axis=0, keepdims=True)
        ik = jnp.min(jnp.where(cur == m, eid, float(N_EXPERTS)), axis=0, keepdims=True)
        hit = eid == ik
        weights.append(jnp.sum(jnp.where(hit, scores, 0.0), axis=0, keepdims=True))
        sel = sel + jnp.where(hit, 1.0, 0.0)
        cur = jnp.where(hit, -jnp.inf, cur)
        picks.append(ik)
    wsum = weights[0]
    for wk in weights[1:]:
        wsum = wsum + wk
    tile_cnt = jnp.broadcast_to(jnp.sum(sel, axis=1, keepdims=True), (N_EXPERTS, LANES))
    tile_cnt = jnp.floor((tile_cnt + (RUN_ALIGN - 1)) * (1.0 / RUN_ALIGN)) * RUN_ALIGN
    tile_off = jnp.dot(etri_ref[...], tile_cnt, precision=HIGHEST, preferred_element_type=F32)
    row = _dot(sel.astype(BF16), tri_ref[...]) + tile_off[:, 0:1]
    for kk in range(TOP_K):
        w_ref[kk:kk + 1, :] = weights[kk] / wsum * ROUTED_SCALE
        loc_ref[kk:kk + 1, :] = jnp.sum(jnp.where(eid == picks[kk], row, 0.0), axis=0, keepdims=True).astype(I32)
    tcnt_ref[...] = tile_cnt.astype(I32)
    tcarry_ref[...] = carry_ref[...].astype(I32)
    toff_ref[...] = tile_off.astype(I32)
    carry_ref[...] = carry_ref[...] + tile_cnt
    cnt_ref[...] = carry_ref[...].astype(I32)


def _route(logits, router_bias):
    t = logits.shape[0]
    tr = MOE_TILE
    assert t % tr == 0
    nt = t // tr
    r = jnp.arange(tr)
    tri = (r[:, None] < r[None, :]).astype(BF16)
    e = jnp.arange(N_EXPERTS)
    etri = (e[:, None] > e[None, :]).astype(F32)
    kt_spec = pl.BlockSpec((TOP_K, tr), lambda i: (0, i))
    tile_spec = pl.BlockSpec((None, N_EXPERTS, LANES), lambda i: (i, 0, 0))
    tile_shape = jax.ShapeDtypeStruct((nt, N_EXPERTS, LANES), I32)
    return pl.pallas_call(
        _route_kernel,
        grid=(nt,),
        in_specs=[pl.BlockSpec((tr, LANES), lambda i: (i, 0)),
                  pl.BlockSpec((N_EXPERTS, 1), lambda i: (0, 0)),
                  pl.BlockSpec((tr, tr), lambda i: (0, 0)),
                  pl.BlockSpec((N_EXPERTS, N_EXPERTS), lambda i: (0, 0))],
        out_specs=[kt_spec, kt_spec, tile_spec, tile_spec, tile_spec,
                   pl.BlockSpec((N_EXPERTS, LANES), lambda i: (0, 0))],
        out_shape=[jax.ShapeDtypeStruct((TOP_K, t), F32), jax.ShapeDtypeStruct((TOP_K, t), I32),
                   tile_shape, tile_shape, tile_shape, jax.ShapeDtypeStruct((N_EXPERTS, LANES), I32)],
        scratch_shapes=[pltpu.VMEM((N_EXPERTS, tr), F32), pltpu.VMEM((N_EXPERTS, LANES), F32)],
        compiler_params=_cparams(("arbitrary",)),
        name="moe_route",
    )(logits, router_bias.astype(F32).reshape(N_EXPERTS, 1), tri, etri)


TAB_WORDS = 1024
TAB_FIELD = 128
N_RUNS = N_EXPERTS + 1
TILE_ROWS = -(-(MOE_TILE * TOP_K + N_EXPERTS * (RUN_ALIGN - 1)) // MOE_TILE) * MOE_TILE
FILLER_ROWS = TILE_ROWS - MOE_TILE * TOP_K
RUN_PIECE = 64


def _run_copies(tab_ref, local_ref, global_ref, sem, to_global):
    def piece(ls, gs, off, rows, priority):
        loc = local_ref.at[pl.ds(pl.multiple_of(ls + off, RUN_ALIGN), rows)]
        glo = global_ref.at[pl.ds(pl.multiple_of(gs + off, RUN_ALIGN), rows)]
        src, dst = (loc, glo) if to_global else (glo, loc)
        pltpu.make_async_copy(src, dst, sem).start(priority=priority)

    def body(e, carry):
        gs = tab_ref[e]
        c = tab_ref[TAB_FIELD + e]
        ls = tab_ref[2 * TAB_FIELD + e]
        whole = c >> (RUN_PIECE.bit_length() - 1)

        def big(j, carry2):
            piece(ls, gs, j * RUN_PIECE, RUN_PIECE, 0)
            return carry2

        lax.fori_loop(0, whole, big, 0)
        for b in range(RUN_ALIGN.bit_length() - 1, RUN_PIECE.bit_length() - 1):
            @pl.when(((c >> b) & 1) == 1)
            def _(b=b):
                piece(ls, gs, whole * RUN_PIECE + (c & ((1 << b) - 1) & (RUN_PIECE - 1)), 1 << b, 1)
        return carry

    lax.fori_loop(0, N_RUNS, body, 0)


def _run_wait(local_ref, global_ref, sem):
    pltpu.make_async_copy(global_ref.at[pl.ds(0, TILE_ROWS)], local_ref, sem).wait()


def _dispatch_kernel(tab_ref, loc_ref, tok_ref, rows_in_ref, rows_ref, srt_ref, sem, *, n_tiles):
    del rows_in_ref
    i = pl.program_id(0)
    slot = i % 2
    tt = tok_ref.shape[0]
    lo, hi = _unpack_pair(tok_ref[...])
    lo, hi = lo.astype(BF16), hi.astype(BF16)
    loc = loc_ref[...]
    blk = loc >> (tt.bit_length() - 1)
    low = (loc & (tt - 1)).astype(F32).astype(BF16)
    r = lax.broadcasted_iota(I32, (tt, tt), 0).astype(F32).astype(BF16)
    one = jnp.ones((tt, tt), BF16)
    for rb in range(TILE_ROWS // tt):
        pb = jnp.zeros((tt, tt), BF16)
        for kk in range(TOP_K):
            lk = jnp.where(blk[kk:kk + 1, :] == rb, low[kk:kk + 1, :], jnp.asarray(-1.0, BF16))
            pb = jnp.where(lk == r, one, pb)
        srt_ref[slot, rb * tt:(rb + 1) * tt, :] = _pack_exact_pair(_dot(pb, lo), _dot(pb, hi))

    @pl.when(i > 0)
    def _():
        _run_wait(srt_ref.at[1 - slot], rows_ref, sem.at[1 - slot])

    _run_copies(tab_ref, srt_ref.at[slot], rows_ref, sem.at[slot], True)

    @pl.when(i == n_tiles - 1)
    def _():
        _run_wait(srt_ref.at[slot], rows_ref, sem.at[slot])


def _dispatch(tab, loc, tok, rows_buf):
    t = tok.shape[0]
    tt = MOE_TILE
    return pl.pallas_call(
        functools.partial(_dispatch_kernel, n_tiles=t // tt),
        grid=(t // tt,),
        in_specs=[pl.BlockSpec((TAB_WORDS,), lambda i: (i,), memory_space=pltpu.SMEM),
                  pl.BlockSpec((TOP_K, tt), lambda i: (0, i)),
                  pl.BlockSpec((tt, HALF), lambda i: (i, 0)),
                  pl.BlockSpec(memory_space=pl.ANY)],
        out_specs=pl.BlockSpec(memory_space=pl.ANY),
        out_shape=jax.ShapeDtypeStruct(rows_buf.shape, rows_buf.dtype),
        scratch_shapes=[pltpu.VMEM((2, TILE_ROWS, HALF), I32), pltpu.SemaphoreType.DMA((2,))],
        input_output_aliases={3: 0},
        compiler_params=_cparams(("arbitrary",)),
        name="moe_dispatch",
    )(tab, loc, tok, rows_buf)


def _expert_kernel(be_ref, nused_ref, x_ref, wgu_ref, wd_ref, y_ref):
    @pl.when(pl.program_id(0) < nused_ref[0])
    def _():
        lo, hi = _unpack_pair(x_ref[...])
        gu = _dot(lo.astype(BF16), wgu_ref[0:HALF, :]) + _dot(hi.astype(BF16), wgu_ref[HALF:2 * HALF, :])
        h = (_silu(gu[:, 0:EXPERT_DIM]) * gu[:, EXPERT_DIM:2 * EXPERT_DIM]).astype(BF16)
        y = _dot(h, wd_ref[...])
        y_ref[...] = _pack_pair(y[:, 0:HALF], y[:, HALF:2 * HALF])

    @pl.when(pl.program_id(0) >= nused_ref[0])
    def _():
        y_ref[...] = jnp.zeros_like(y_ref)


def _experts(block_expert, nused, rows, wgu, wd):
    n_rows = rows.shape[0]
    nb = n_rows // MOE_BLOCK

    def row_map(i, be, nu):
        return (jnp.minimum(i, nu[0] - 1), 0)

    def w_map(i, be, nu):
        return (be[jnp.minimum(i, nu[0] - 1)], 0, 0)

    grid_spec = pltpu.PrefetchScalarGridSpec(
        num_scalar_prefetch=2,
        grid=(nb,),
        in_specs=[pl.BlockSpec((MOE_BLOCK, HALF), row_map),
                  pl.BlockSpec((None, D_MODEL, 2 * EXPERT_DIM), w_map),
                  pl.BlockSpec((None, EXPERT_DIM, D_MODEL), w_map)],
        out_specs=pl.BlockSpec((MOE_BLOCK, HALF), lambda i, be, nu: (i, 0)),
    )
    return pl.pallas_call(
        _expert_kernel,
        grid_spec=grid_spec,
        out_shape=jax.ShapeDtypeStruct((n_rows, HALF), I32),
        compiler_params=_cparams(("arbitrary",)),
        name="moe_experts",
    )(block_expert, nused, rows, wgu, wd)


def _combine_kernel(tab_ref, tabn_ref, x1_ref, tok_ref, loc_ref, w_ref, gate_ref, wgu_ref, wd_ref, y_ref, o_ref,
                    buf_ref, sem, *, n_tiles):
    i = pl.program_id(0)
    slot = i % 2
    tt = tok_ref.shape[0]

    @pl.when(i == 0)
    def _():
        _run_copies(tab_ref, buf_ref.at[slot], y_ref, sem.at[slot], False)

    @pl.when(i + 1 < n_tiles)
    def _():
        _run_copies(tabn_ref, buf_ref.at[1 - slot], y_ref, sem.at[1 - slot], False)

    lo, hi = _unpack_pair(tok_ref[...])
    gu = _dot(lo.astype(BF16), wgu_ref[0:HALF, :]) + _dot(hi.astype(BF16), wgu_ref[HALF:2 * HALF, :])
    h = (_silu(gu[:, 0:EXPERT_DIM]) * gu[:, EXPERT_DIM:2 * EXPERT_DIM]).astype(BF16)
    shared = _dot(h, wd_ref[...])
    _run_wait(buf_ref.at[slot], y_ref, sem.at[slot])
    acc_lo = shared[:, 0:HALF]
    acc_hi = shared[:, HALF:2 * HALF]
    loc = loc_ref[...]
    w = w_ref[...].astype(BF16)
    blk = loc >> (tt.bit_length() - 1)
    low = (loc & (tt - 1)).astype(F32).astype(BF16)
    r = lax.broadcasted_iota(I32, (tt, tt), 1).astype(F32).astype(BF16)
    for rb in range(TILE_ROWS // tt):
        wb = jnp.zeros((tt, tt), BF16)
        for kk in range(TOP_K):
            lk = jnp.where(blk[:, kk:kk + 1] == rb, low[:, kk:kk + 1], jnp.asarray(-1.0, BF16))
            wb = jnp.where(lk == r, jnp.broadcast_to(w[:, kk:kk + 1], (tt, tt)), wb)
        ylo, yhi = _unpack_pair(buf_ref[slot, rb * tt:(rb + 1) * tt, :])
        acc_lo = acc_lo + _dot(wb, ylo.astype(BF16))
        acc_hi = acc_hi + _dot(wb, yhi.astype(BF16))
    gate = gate_ref[...]
    o_ref[:, 0:HALF] = x1_ref[:, 0:HALF] + gate[:, 0:HALF] * acc_lo
    o_ref[:, HALF:2 * HALF] = x1_ref[:, HALF:2 * HALF] + gate[:, HALF:2 * HALF] * acc_hi


def _combine(tab, x1, tok, loc_tok, w_tok, gate, wgu, wd, y_rows, tokens_per_gate):
    t, d = x1.shape
    tt = MOE_TILE
    nt = t // tt
    per = tokens_per_gate // tt
    return pl.pallas_call(
        functools.partial(_combine_kernel, n_tiles=nt),
        grid=(nt,),
        in_specs=[pl.BlockSpec((TAB_WORDS,), lambda i: (i,), memory_space=pltpu.SMEM),
                  pl.BlockSpec((TAB_WORDS,), lambda i: (jnp.minimum(i + 1, nt - 1),), memory_space=pltpu.SMEM),
                  pl.BlockSpec((tt, d), lambda i: (i, 0)),
                  pl.BlockSpec((tt, HALF), lambda i: (i, 0)),
                  pl.BlockSpec((tt, TOP_K), lambda i: (i, 0)),
                  pl.BlockSpec((tt, TOP_K), lambda i: (i, 0)),
                  pl.BlockSpec((None, 1, d), lambda i: (i // per, 0, 0)),
                  pl.BlockSpec(wgu.shape, lambda i: (0, 0)),
                  pl.BlockSpec(wd.shape, lambda i: (0, 0)),
                  pl.BlockSpec(memory_space=pl.ANY)],
        out_specs=pl.BlockSpec((tt, d), lambda i: (i, 0)),
        out_shape=jax.ShapeDtypeStruct((t, d), F32),
        scratch_shapes=[pltpu.VMEM((2, TILE_ROWS, HALF), I32), pltpu.SemaphoreType.DMA((2,))],
        compiler_params=_cparams(("arbitrary",)),
        name="moe_combine",
    )(tab, tab, x1, tok, loc_tok, w_tok, gate, wgu, wd, y_rows)


def _moe(parts, router_bias, w_gate, w_up, w_down, ws_gate, ws_up, ws_down, row_buf=None):
    logits = jnp.concatenate([p[2].reshape(-1, LANES) for p in parts], axis=0)
    t = logits.shape[0]
    w, loc, tile_cnt, tile_carry, tile_off, cnt = _route(logits, router_bias)
    counts = cnt[:, 0]
    padded = (counts + MOE_BLOCK - 1) // MOE_BLOCK * MOE_BLOCK
    pad_end = jnp.cumsum(padded)
    pad_start = pad_end - padded
    nt = t // MOE_TILE
    max_aligned = t * TOP_K + nt * N_EXPERTS * (RUN_ALIGN - 1)
    area_rows = -(-max_aligned // MOE_BLOCK) * MOE_BLOCK + N_EXPERTS * MOE_BLOCK
    n_rows = -(-(area_rows + FILLER_ROWS) // MOE_BLOCK) * MOE_BLOCK
    if row_buf is not None:
        assert row_buf.shape[0] >= n_rows
        n_rows = row_buf.shape[0]
    tile_rows_used = tile_off[:, N_EXPERTS - 1, 0] + tile_cnt[:, N_EXPERTS - 1, 0]

    def field(per_expert, filler):
        vals = jnp.concatenate([per_expert.astype(I32), filler.astype(I32)[:, None]], axis=1)
        return jnp.pad(vals, ((0, 0), (0, TAB_FIELD - N_RUNS)))

    tab = jnp.concatenate(
        [field(pad_start[None, :] + tile_carry[:, :, 0], jnp.full((nt,), area_rows, I32)),
         field(tile_cnt[:, :, 0], TILE_ROWS - tile_rows_used),
         field(tile_off[:, :, 0], tile_rows_used),
         jnp.zeros((nt, TAB_WORDS - 3 * TAB_FIELD), I32)], axis=1).reshape(-1)
    w_tok = w.T
    loc_tok = loc.T
    nb = n_rows // MOE_BLOCK
    block_start = jnp.arange(nb, dtype=I32) * MOE_BLOCK
    block_expert = jnp.minimum(jnp.sum(block_start[:, None] >= pad_end[None, :], axis=1), N_EXPERTS - 1).astype(I32)
    nused = (pad_end[-1] // MOE_BLOCK).astype(I32).reshape(1)
    rows = jnp.zeros((n_rows, HALF), I32) if row_buf is None else row_buf
    off = 0
    for x1, tok, _, _, _ in parts:
        cnt_tok = tok.shape[0] * tok.shape[1]
        part_tab = tab[off // MOE_TILE * TAB_WORDS:(off + cnt_tok) // MOE_TILE * TAB_WORDS]
        rows = _dispatch(part_tab, loc[:, off:off + cnt_tok], tok.reshape(cnt_tok, HALF), rows)
        off += cnt_tok
    wgu = jnp.concatenate([w_gate, w_up], axis=-1).astype(BF16)
    y_rows = _experts(block_expert, nused, rows, wgu, w_down.astype(BF16))
    wsgu = jnp.concatenate([ws_gate, ws_up], axis=-1).astype(BF16)
    wsd = ws_down.astype(BF16)
    outs = []
    off = 0
    for x1, tok, _, gate, per in parts:
        cnt_tok = tok.shape[0] * tok.shape[1]
        part_tab = tab[off // MOE_TILE * TAB_WORDS:(off + cnt_tok) // MOE_TILE * TAB_WORDS]
        o = _combine(part_tab, x1.reshape(cnt_tok, D_MODEL), tok.reshape(cnt_tok, HALF),
                     loc_tok[off:off + cnt_tok], w_tok[off:off + cnt_tok], gate, wsgu, wsd, y_rows, per)
        outs.append(o.reshape(x1.shape))
        off += cnt_tok
    return outs, rows


def kernel(x, c, ctx, c_ctx, ada_w, ada_b, norm_mix, norm_ffn, ev_w_in, ev_w_out, diff_qk_gain, diff_lambda,
           diff_out_gain, hgrn_lb, hgrn_out_gain, od_w_in, od_w_out, gqa_qk_gain, na_qk_gain, na_rpb, moe_router,
           moe_router_bias, moe_w_gate, moe_w_up, moe_w_down, shared_w_gate, shared_w_up, shared_w_down):
    bsz, n, d = x.shape
    m = ctx.shape[1]
    depth = ada_w.shape[0]
    cond_rows = -(-(bsz + 1) // 8) * 8
    cond = jnp.zeros((cond_rows, d), F32).at[0:bsz].set(c).at[bsz].set(c_ctx)
    mods = _ada_mod(cond, ada_w, ada_b)

    xc = ctx
    row_buf = None
    for layer in range(depth):
        need_ctx = layer < depth - 1
        j = layer // 2
        mod = mods[layer].reshape(cond_rows, 6, d)
        mx = [mod[0:bsz, i][:, None, :] for i in range(6)]
        mc = [jnp.broadcast_to(mod[bsz:bsz + 1, i][:, None, :], (bsz, 1, d)) for i in range(6)]
        if layer % 2 == 0:
            w_in = ev_w_in[j].astype(BF16)
            px = _even_proj(x, mx[0], mx[1], norm_mix[layer], w_in, diff_qk_gain[j], hgrn_lb[0], hgrn_lb[1], j, True)
            pc = _even_proj(xc, mc[0], mc[1], norm_mix[layer], w_in, diff_qk_gain[j], hgrn_lb[0], hgrn_lb[1], j, False)
            lam_init = 0.8 - 0.6 * math.exp(-0.3 * layer)
            a_x = _diff_attention(px[0], [(pc[1], pc[2]), (px[1], px[2])], diff_lambda[j], diff_out_gain[j], lam_init)
            a_c = _diff_attention(pc[0], [(pc[1], pc[2])], diff_lambda[j], diff_out_gain[j], lam_init)
            b_x, b_c = _hgrn(px[3:], pc[3:], hgrn_out_gain[j])
            w_out = ev_w_out[j].astype(BF16)
        else:
            w_in = od_w_in[j].astype(BF16)
            px = _odd_proj(x, mx[0], mx[1], norm_mix[layer], w_in, gqa_qk_gain[j], na_qk_gain[j], True)
            pc = _odd_proj(xc, mc[0], mc[1], norm_mix[layer], w_in, gqa_qk_gain[j], na_qk_gain[j], False)
            a_x = _gqa_attention(px[0], pc[1], px[1])
            b_x = _na_attention(px[2], px[3], px[4], pc[3], pc[4], na_rpb[j])
            a_c = b_c = None
            w_out = od_w_out[j].astype(BF16)
        router_pad = jnp.zeros((d, LANES), F32).at[:, 0:N_EXPERTS].set(moe_router[layer].astype(F32))
        x1, tok_x, logit_x = _out_proj(a_x, b_x, w_out, x, mx[2], norm_ffn[layer], mx[3], mx[4], router_pad)
        parts = [(x1, tok_x, logit_x, mx[5], n)]
        if need_ctx:
            xc1, tok_c, logit_c = _out_proj(a_c, b_c, w_out, xc, mc[2], norm_ffn[layer], mc[3], mc[4], router_pad)
            parts.append((xc1, tok_c, logit_c, mc[5][0:1], bsz * m))
        outs, row_buf = _moe(parts, moe_router_bias[layer], moe_w_gate[layer], moe_w_up[layer], moe_w_down[layer],
                             shared_w_gate[layer], shared_w_up[layer], shared_w_down[layer], row_buf)
        x = outs[0]
        if need_ctx:
            xc = outs[1]
    return x
```

```python
import functools
import math

import jax
import jax.numpy as jnp
import numpy as np
from jax import lax
from jax.experimental import pallas as pl
from jax.experimental.pallas import tpu as pltpu

F32 = jnp.float32
BF16 = jnp.bfloat16
I32 = jnp.int32
HIGHEST = lax.Precision.HIGHEST

D_MODEL = 1024
GRID_W = 64
HEAD_DIM = 64
ATTN_SCALE = HEAD_DIM ** -0.5
ROPE_THETA = 10000.0
EPS = 1e-6
DIFF_HEADS = D_MODEL // 256
HGRN_HEADS = D_MODEL // 256
HGRN_CHUNK = 64
GQA_HEADS = D_MODEL // 128
GQA_KV_HEADS = GQA_HEADS // 4
NA_HEADS = D_MODEL // 128
NA_ROWS = 8
NA_COLS = 16
N_EXPERTS = 64
N_GROUPS = 8
EXPERTS_PER_GROUP = N_EXPERTS // N_GROUPS
TOPK_GROUPS = 4
TOP_K = 8
EXPERT_DIM = D_MODEL // 4
ROUTED_SCALE = 2.5
HALF = D_MODEL // 2

LANES = 128
VMEM_LIMIT_BYTES = 56 * 1024 * 1024
PROJ_ROWS = 512
FLASH_TK = 2048
FLASH_SB = 64
DIFF_TQ = 512
GQA_TQ = 256
NA_QROWS = 8
NA_BAND = 16
HGRN_HEADS_PER_STEP = 2
HGRN_CHUNKS_PER_STEP = 2
MOE_BLOCK = 512
MOE_TILE = 256
RUN_ALIGN = 8
NEG_BIG = -1e30


def _cparams(sem):
    return pltpu.CompilerParams(dimension_semantics=sem, vmem_limit_bytes=VMEM_LIMIT_BYTES)


def _silu(x):
    return x * jax.nn.sigmoid(x)


def _dot(a, b):
    return jnp.dot(a, b, preferred_element_type=F32)


def _dot_nt(a, b):
    return lax.dot_general(a, b, (((1,), (1,)), ((), ())), preferred_element_type=F32)


def _dot_tn(a, b):
    return lax.dot_general(a, b, (((0,), (0,)), ((), ())), preferred_element_type=F32)


def _pack_pair(lo, hi):
    lo_bits = lax.bitcast_convert_type(lo.astype(BF16).astype(F32), I32)
    hi_bits = lax.bitcast_convert_type(hi.astype(BF16).astype(F32), I32)
    return lax.shift_right_logical(lo_bits, 16) | (hi_bits & jnp.int32(-65536))


def _pack_exact_pair(lo, hi):
    lo_bits = lax.bitcast_convert_type(lo, I32)
    hi_bits = lax.bitcast_convert_type(hi, I32)
    return lax.shift_right_logical(lo_bits, 16) | (hi_bits & jnp.int32(-65536))


def _unpack_pair(w):
    lo = lax.bitcast_convert_type(lax.shift_left(w, 16), F32)
    hi = lax.bitcast_convert_type(w & jnp.int32(-65536), F32)
    return lo, hi


def _ada_kernel(cond_ref, w_ref, b_ref, o_ref):
    s = _silu(cond_ref[...])
    o_ref[...] = jnp.dot(s, w_ref[...], precision=HIGHEST, preferred_element_type=F32) + b_ref[...]


def _ada_mod(cond, ada_w, ada_b):
    depth = ada_w.shape[0]
    rows = cond.shape[0]
    nblk = ada_w.shape[2] // D_MODEL
    return pl.pallas_call(
        _ada_kernel,
        grid=(depth, nblk),
        in_specs=[
            pl.BlockSpec((rows, D_MODEL), lambda l, j: (0, 0)),
            pl.BlockSpec((None, D_MODEL, D_MODEL), lambda l, j: (l, 0, j)),
            pl.BlockSpec((None, 1, D_MODEL), lambda l, j: (l, 0, j)),
        ],
        out_specs=pl.BlockSpec((None, rows, D_MODEL), lambda l, j: (l, 0, j)),
        out_shape=jax.ShapeDtypeStruct((depth, rows, nblk * D_MODEL), F32),
        compiler_params=_cparams(("parallel", "parallel")),
        name="ada_mod",
    )(cond, ada_w, ada_b.reshape(depth, 1, -1))


def _norm_mod(x, gain, shift, scale):
    ms = jnp.mean(x * x, axis=-1, keepdims=True)
    h = x * lax.rsqrt(ms + EPS) * gain
    return h * (1.0 + scale) + shift


def _seg_rms(acc, gain, bd):
    sq = acc * acc
    hi = sq.astype(BF16)
    lo = (sq - hi.astype(F32)).astype(BF16)
    ms = _dot(hi, bd) + _dot(lo, bd)
    return acc * lax.rsqrt(ms + EPS) * gain


def _rope(y, c, sm, sp):
    return y * c + pltpu.roll(y, LANES - 16, 1) * sm + pltpu.roll(y, 16, 1) * sp


def _rope_tables(n):
    pos = jnp.arange(n, dtype=I32)
    row = (pos // GRID_W).astype(F32)
    col = (pos % GRID_W).astype(F32)
    axis_dim = HEAD_DIM // 2
    inv_freq = ROPE_THETA ** (-jnp.arange(0, axis_dim, 2, dtype=F32) / axis_dim)
    ang_row = row[:, None] * inv_freq
    ang_col = col[:, None] * inv_freq
    lane = jnp.arange(LANES)
    p = lane % axis_dim
    f = p % (axis_dim // 2)
    on_row = ((lane % HEAD_DIM) // axis_dim) == 0
    ang = jnp.where(on_row[None, :], ang_row[:, f], ang_col[:, f])
    c = jnp.cos(ang)
    s = jnp.sin(ang)
    first = (p < axis_dim // 2)[None, :]
    return c, jnp.where(first, -s, 0.0), jnp.where(first, 0.0, s)


def _seg_mean_matrix():
    r = jnp.arange(LANES)
    return jnp.where((r[:, None] // HEAD_DIM) == (r[None, :] // HEAD_DIM), 1.0 / HEAD_DIM, 0.0).astype(BF16)


def _even_proj_kernel(*refs, rope, layer_slot):
    if rope:
        (x_ref, shift_ref, scale_ref, gain_ref, w_ref, qkg_ref, lbf_ref, lbb_ref, bd_ref, rc_ref, rm_ref, rp_ref,
         dq_ref, dk_ref, dv_ref, hq_ref, kf_ref, gf_ref, kb_ref, gb_ref, hv_ref, hg_ref) = refs
        tables = (rc_ref[...], rm_ref[...], rp_ref[...])
    else:
        (x_ref, shift_ref, scale_ref, gain_ref, w_ref, qkg_ref, lbf_ref, lbb_ref, bd_ref,
         dq_ref, dk_ref, dv_ref, hq_ref, kf_ref, gf_ref, kb_ref, gb_ref, hv_ref, hg_ref) = refs
        tables = None
    hb = _norm_mod(x_ref[...], gain_ref[...], shift_ref[...], scale_ref[...]).astype(BF16)
    bd = bd_ref[...]
    width = 4 * LANES

    def proj(group):
        return _dot(hb, w_ref[:, group * width:(group + 1) * width])

    def qk(group, gain, out_ref, mult):
        acc = proj(group)
        for s in range(4):
            y = _seg_rms(acc[:, s * LANES:(s + 1) * LANES], gain, bd)
            if tables is not None:
                y = _rope(y, *tables)
            out_ref[:, s * LANES:(s + 1) * LANES] = (y * mult).astype(BF16)

    qk(0, qkg_ref[0:1, :], dq_ref, ATTN_SCALE)
    qk(1, qkg_ref[1:2, :], dk_ref, 1.0)
    dv_ref[...] = proj(2).astype(BF16)
    hq_ref[...] = _silu(proj(3)).astype(BF16)

    def forget(group, lb_ref, k_ref, g_ref):
        raw = lb_ref[...]
        e = jnp.exp(raw - jnp.max(raw, axis=0, keepdims=True))
        lb = jnp.sum(e[0:layer_slot + 1, :], axis=0, keepdims=True) / jnp.sum(e, axis=0, keepdims=True)
        f = lb + (1.0 - lb) * jax.nn.sigmoid(proj(group))
        k_ref[...] = (1.0 - f).astype(BF16)
        g_ref[...] = jnp.log(f)

    forget(4, lbf_ref, kf_ref, gf_ref)
    forget(5, lbb_ref, kb_ref, gb_ref)
    hv_ref[...] = proj(6).astype(BF16)
    hg_ref[...] = _silu(proj(7)).astype(BF16)


def _row_spec(tm, width):
    return pl.BlockSpec((None, tm, width), lambda b, i: (b, i, 0))


def _bcast_spec(width):
    return pl.BlockSpec((None, 1, width), lambda b, i: (b, 0, 0))


def _const_spec(shape):
    nd = len(shape)
    return pl.BlockSpec(shape, lambda b, i: (0,) * nd)


def _even_proj(x, shift, scale, gain, w, qk_gain, lb_fwd, lb_bwd, layer_slot, rope):
    bsz, n, d = x.shape
    tm = min(PROJ_ROWS, n)
    width = 4 * LANES
    qkg = jnp.tile(qk_gain.astype(F32), (1, 2))
    in_specs = [
        _row_spec(tm, d), _bcast_spec(d), _bcast_spec(d), _const_spec((1, d)), _const_spec(w.shape),
        _const_spec((2, LANES)), _const_spec(lb_fwd.shape), _const_spec(lb_bwd.shape), _const_spec((LANES, LANES)),
    ]
    args = [x, shift, scale, gain.reshape(1, d), w, qkg, lb_fwd, lb_bwd, _seg_mean_matrix()]
    if rope:
        tab_spec = pl.BlockSpec((tm, LANES), lambda b, i: (i, 0))
        in_specs += [tab_spec] * 3
        args += list(_rope_tables(n))
    out_dtypes = [BF16, BF16, BF16, BF16, BF16, F32, BF16, F32, BF16, BF16]
    return pl.pallas_call(
        functools.partial(_even_proj_kernel, rope=rope, layer_slot=layer_slot),
        grid=(bsz, n // tm),
        in_specs=in_specs,
        out_specs=[_row_spec(tm, width)] * len(out_dtypes),
        out_shape=[jax.ShapeDtypeStruct((bsz, n, width), dt) for dt in out_dtypes],
        compiler_params=_cparams(("parallel", "parallel")),
        name="even_proj_x" if rope else "even_proj_ctx",
    )(*args)


def _odd_proj_kernel(*refs, rope):
    if rope:
        (x_ref, shift_ref, scale_ref, gain_ref, w_ref, gqg_ref, nag_ref, bd_ref, rc_ref, rm_ref, rp_ref,
         gq_ref, gkv_ref, nq_ref, nk_ref, nv_ref) = refs
        tables = (rc_ref[...], rm_ref[...], rp_ref[...])
    else:
        (x_ref, shift_ref, scale_ref, gain_ref, w_ref, gqg_ref, nag_ref, bd_ref,
         gq_ref, gkv_ref, nq_ref, nk_ref, nv_ref) = refs
        tables = None
    hb = _norm_mod(x_ref[...], gain_ref[...], shift_ref[...], scale_ref[...]).astype(BF16)
    bd = bd_ref[...]

    def slab(acc, s, gain, use_rope, mult):
        y = _seg_rms(acc[:, s * LANES:(s + 1) * LANES], gain, bd)
        if use_rope and tables is not None:
            y = _rope(y, *tables)
        return (y * mult).astype(BF16)

    q_w = GQA_HEADS * HEAD_DIM
    acc = _dot(hb, w_ref[:, 0:q_w])
    for s in range(q_w // LANES):
        gq_ref[:, s * LANES:(s + 1) * LANES] = slab(acc, s, gqg_ref[0:1, :], True, ATTN_SCALE)
    acc = _dot(hb, w_ref[:, q_w:q_w + 2 * LANES])
    gkv_ref[:, 0:LANES] = slab(acc, 0, gqg_ref[1:2, :], True, 1.0)
    gkv_ref[:, LANES:2 * LANES] = acc[:, LANES:2 * LANES].astype(BF16)
    base = q_w + 2 * LANES
    na_w = NA_HEADS * HEAD_DIM
    acc = _dot(hb, w_ref[:, base:base + na_w])
    for s in range(na_w // LANES):
        nq_ref[:, s * LANES:(s + 1) * LANES] = slab(acc, s, nag_ref[0:1, :], False, ATTN_SCALE)
    acc = _dot(hb, w_ref[:, base + na_w:base + 2 * na_w])
    for s in range(na_w // LANES):
        nk_ref[:, s * LANES:(s + 1) * LANES] = slab(acc, s, nag_ref[1:2, :], False, 1.0)
    nv_ref[...] = _dot(hb, w_ref[:, base + 2 * na_w:base + 3 * na_w]).astype(BF16)


def _odd_proj(x, shift, scale, gain, w, gqa_gain, na_gain, rope):
    bsz, n, d = x.shape
    tm = min(PROJ_ROWS, n)
    gqg = jnp.tile(gqa_gain.astype(F32), (1, 2))
    nag = jnp.tile(na_gain.astype(F32), (1, 2))
    in_specs = [
        _row_spec(tm, d), _bcast_spec(d), _bcast_spec(d), _const_spec((1, d)), _const_spec(w.shape),
        _const_spec((2, LANES)), _const_spec((2, LANES)), _const_spec((LANES, LANES)),
    ]
    args = [x, shift, scale, gain.reshape(1, d), w, gqg, nag, _seg_mean_matrix()]
    if rope:
        tab_spec = pl.BlockSpec((tm, LANES), lambda b, i: (i, 0))
        in_specs += [tab_spec] * 3
        args += list(_rope_tables(n))
    widths = [GQA_HEADS * HEAD_DIM, 2 * LANES, NA_HEADS * HEAD_DIM, NA_HEADS * HEAD_DIM, NA_HEADS * HEAD_DIM]
    return pl.pallas_call(
        functools.partial(_odd_proj_kernel, rope=rope),
        grid=(bsz, n // tm),
        in_specs=in_specs,
        out_specs=[_row_spec(tm, wd) for wd in widths],
        out_shape=[jax.ShapeDtypeStruct((bsz, n, wd), BF16) for wd in widths],
        compiler_params=_cparams(("parallel", "parallel")),
        name="odd_proj_x" if rope else "odd_proj_ctx",
    )(*args)


def _flash_scratch(rows):
    return [pltpu.VMEM((rows, FLASH_TK), F32),
            pltpu.VMEM((rows, FLASH_TK), F32),
            pltpu.VMEM((rows, 1), F32),
            pltpu.VMEM((rows, 2 * LANES), F32)]


def _flash(qs, sources, scratch):
    s0_ref, s1_ref, m_ref, acc_ref = scratch
    s_refs = (s0_ref, s1_ref)
    rows = qs.shape[0]
    tk = s0_ref.shape[1]
    m_ref[...] = jnp.full(m_ref.shape, -jnp.inf, F32)
    acc_ref[...] = jnp.zeros(acc_ref.shape, F32)

    def issue(slot, width, k):
        s_refs[slot][:, 0:width] = _dot_nt(qs, k)

    def consume(slot, width, v):
        s = s_refs[slot][:, 0:width]
        m_old = m_ref[...]
        m_new = jnp.maximum(m_old, jnp.max(s, axis=-1, keepdims=True))
        alpha = jnp.exp(m_old - m_new)
        p = jnp.exp((s - m_new).astype(BF16))
        ones = (lax.broadcasted_iota(I32, (width, LANES), 1) == 0).astype(BF16)
        m_ref[...] = m_new
        acc_ref[...] = alpha * acc_ref[...] + _dot(p, jnp.concatenate([v, ones], axis=1))

    issued = 0
    prev = None
    for k_ref, v_ref, length in sorted(sources, key=lambda src: src[2]):
        chunk = min(tk, length)
        steps = length // chunk
        if steps == 1:
            slot = issued % 2
            issue(slot, chunk, k_ref[...])
            if prev is not None:
                consume(prev[0], prev[1], prev[2]())
            prev = (slot, chunk, lambda v_ref=v_ref: v_ref[...])
            issued += 1
            continue
        assert steps % 2 == 0 and chunk == tk
        base = issued % 2

        def kv(ref, c):
            return ref[pl.ds(pl.multiple_of(c * tk, tk), tk), :]

        issue(base, tk, kv(k_ref, 0))
        if prev is not None:
            consume(prev[0], prev[1], prev[2]())

        def body(j, carry, k_ref=k_ref, v_ref=v_ref, base=base):
            issue(1 - base, tk, kv(k_ref, 2 * j + 1))
            consume(base, tk, kv(v_ref, 2 * j))
            issue(base, tk, kv(k_ref, 2 * j + 2))
            consume(1 - base, tk, kv(v_ref, 2 * j + 1))
            return carry

        lax.fori_loop(0, steps // 2 - 1, body, 0)
        issue(1 - base, tk, kv(k_ref, steps - 1))
        consume(base, tk, kv(v_ref, steps - 2))
        prev = (1 - base, tk, lambda v_ref=v_ref, steps=steps: kv(v_ref, steps - 1))
        issued += steps
    consume(prev[0], prev[1], prev[2]())
    return acc_ref[:, 0:LANES] / acc_ref[:, LANES:LANES + 1]


def _lane_ids(shape):
    return lax.broadcasted_iota(I32, shape, len(shape) - 1)


def _diff_attn_kernel(*refs, n_src, lens, lam_init):
    q_ref = refs[0]
    kv_refs = refs[1:1 + 2 * n_src]
    lam_ref, gain_ref, o_ref = refs[1 + 2 * n_src:4 + 2 * n_src]
    scratch = refs[4 + 2 * n_src:]
    q = q_ref[...]
    tq = q.shape[0]
    lo = _lane_ids(q.shape) < HEAD_DIM
    zero = jnp.zeros_like(q)
    qs = jnp.concatenate([jnp.where(lo, q, zero), jnp.where(lo, zero, q)], axis=0)
    sources = [(kv_refs[2 * i], kv_refs[2 * i + 1], lens[i]) for i in range(n_src)]
    a = _flash(qs, sources, scratch)
    lp = lam_ref[...]
    lam = (jnp.exp(jnp.sum(lp[0:1, :] * lp[1:2, :], axis=-1, keepdims=True))
           - jnp.exp(jnp.sum(lp[2:3, :] * lp[3:4, :], axis=-1, keepdims=True)) + lam_init)
    o = a[0:tq, :] - lam * a[tq:2 * tq, :]
    ms = jnp.mean(o * o, axis=-1, keepdims=True)
    o = o * lax.rsqrt(ms + EPS) * gain_ref[...] * (1.0 - lam_init)
    o_ref[...] = o.astype(BF16)


def _diff_attention(q, kv_list, lam_params, out_gain, lam_init):
    bsz, n, width = q.shape
    tq = min(DIFF_TQ, n)
    lens = tuple(k.shape[1] for k, _ in kv_list)
    in_specs = [pl.BlockSpec((None, tq, LANES), lambda b, h, i: (b, i, h))]
    args = [q]
    for (k, v), length in zip(kv_list, lens):
        spec = pl.BlockSpec((None, length, LANES), lambda b, h, i: (b, 0, h))
        in_specs += [spec, spec]
        args += [k, v]
    in_specs += [pl.BlockSpec(lam_params.shape, lambda b, h, i: (0, 0)),
                 pl.BlockSpec((1, LANES), lambda b, h, i: (0, 0))]
    args += [lam_params.astype(F32), out_gain.reshape(1, LANES).astype(F32)]
    return pl.pallas_call(
        functools.partial(_diff_attn_kernel, n_src=len(kv_list), lens=lens, lam_init=lam_init),
        grid=(bsz, DIFF_HEADS, n // tq),
        in_specs=in_specs,
        out_specs=pl.BlockSpec((None, tq, LANES), lambda b, h, i: (b, i, h)),
        out_shape=jax.ShapeDtypeStruct((bsz, n, width), BF16),
        scratch_shapes=_flash_scratch(2 * tq),
        compiler_params=_cparams(("parallel", "parallel", "parallel")),
        name="diff_attn",
    )(*args)


def _gqa_kernel(q_ref, kc_ref, vc_ref, kx_ref, vx_ref, o_ref, *scratch, lens):
    tq = q_ref.shape[0]
    sources = [(kc_ref, vc_ref, lens[0]), (kx_ref, vx_ref, lens[1])]
    lanes = _lane_ids((tq, LANES))
    for kv in range(GQA_KV_HEADS):
        mine = (lanes // HEAD_DIM) == kv
        rows = []
        for half in range(2):
            hh = q_ref[:, (2 * kv + half) * LANES:(2 * kv + half + 1) * LANES]
            sw = pltpu.roll(hh.astype(F32), HEAD_DIM, 1).astype(BF16)
            zero = jnp.zeros_like(hh)
            a_here, b_here = (hh, sw) if kv == 0 else (sw, hh)
            rows += [jnp.where(mine, a_here, zero), jnp.where(mine, b_here, zero)]
        qs = jnp.concatenate(rows, axis=0)
        o = _flash(qs, sources, scratch)
        for half in range(2):
            oa = o[(2 * half) * tq:(2 * half + 1) * tq, :]
            ob = o[(2 * half + 1) * tq:(2 * half + 2) * tq, :]
            oa_sw = pltpu.roll(oa, HEAD_DIM, 1)
            ob_sw = pltpu.roll(ob, HEAD_DIM, 1)
            if kv == 0:
                res = jnp.where(lanes < HEAD_DIM, oa, ob_sw)
            else:
                res = jnp.where(lanes < HEAD_DIM, oa_sw, ob)
            o_ref[:, (2 * kv + half) * LANES:(2 * kv + half + 1) * LANES] = res.astype(BF16)


def _gqa_attention(q, kv_c, kv_x):
    bsz, n, width = q.shape
    tq = min(GQA_TQ, n)
    lens = (kv_c.shape[1], kv_x.shape[1])

    def kspec(length, col):
        return pl.BlockSpec((None, length, LANES), lambda b, i, col=col: (b, 0, col))

    return pl.pallas_call(
        functools.partial(_gqa_kernel, lens=lens),
        grid=(bsz, n // tq),
        in_specs=[pl.BlockSpec((None, tq, width), lambda b, i: (b, i, 0)),
                  kspec(lens[0], 0), kspec(lens[0], 1), kspec(lens[1], 0), kspec(lens[1], 1)],
        out_specs=pl.BlockSpec((None, tq, width), lambda b, i: (b, i, 0)),
        out_shape=jax.ShapeDtypeStruct((bsz, n, width), BF16),
        scratch_shapes=_flash_scratch(4 * tq),
        compiler_params=_cparams(("parallel", "parallel")),
        name="gqa_attn",
    )(q, kv_c, kv_c, kv_x, kv_x)


GLA_LEVELS = tuple(HGRN_CHUNK >> (i + 1) for i in range(HGRN_CHUNK.bit_length() - 1))


def _gla_tables(reverse):
    c = HGRN_CHUNK
    p = np.arange(c)
    t, u = p[:, None], p[None, :]
    exps = [(u <= t), (u > t)]
    pair = [np.eye(c, dtype=bool)]
    target = []
    for b in GLA_LEVELS:
        later = (p // b) % 2 == 1
        bnd = (p // (2 * b)) * (2 * b) + b - 1
        e = np.where(later[:, None], (u > bnd[:, None]) & (u <= t), (u > t) & (u <= bnd[:, None]))
        exps.append(e)
        pair.append((t // (2 * b)) == (u // (2 * b)))
        target.append(later)
    flip = (lambda m: m[::-1, ::-1]) if reverse else (lambda m: m)
    emat = np.concatenate([flip(e) for e in exps], axis=0).astype(np.float32)
    pmat = np.concatenate([flip(m) for m in pair], axis=0).astype(np.float32)
    return emat, pmat


def _gla_chunks(chains, emats, pmats):
    c = HGRN_CHUNK
    ridx = lax.broadcasted_iota(I32, (c, 1), 0)
    xs = []
    for ch in chains:
        g_hi = ch[2].astype(BF16)
        g_lo = (ch[2] - g_hi.astype(F32)).astype(BF16)
        xs.append(jnp.exp(_dot(emats[ch[4]], g_hi) + _dot(emats[ch[4]], g_lo)))
    vbs = [ch[3].astype(BF16) for ch in chains]
    qds = [(ch[0] * x[0:c]).astype(BF16) for ch, x in zip(chains, xs)]
    incs = [_dot_tn(vb, (ch[1] * x[c:2 * c]).astype(BF16)) for ch, x, vb in zip(chains, xs, vbs)]
    tots = [x[0:1] if ch[4] else x[c - 1:c] for ch, x in zip(chains, xs)]
    scores = [pmats[ch[4]][0:c] * _dot_nt(ch[0].astype(BF16), ch[1].astype(BF16)) for ch in chains]
    for lvl in range(len(GLA_LEVELS)):
        for ci, (ch, x) in enumerate(zip(chains, xs)):
            xl = x[(2 + lvl) * c:(3 + lvl) * c]
            pos = (c - 1 - ridx) if ch[4] else ridx
            later = ((pos >> (GLA_LEVELS[lvl].bit_length() - 1)) & 1) == 1
            qt = jnp.where(later, ch[0] * xl, 0.0).astype(BF16)
            kt = jnp.where(later, 0.0, ch[1] * xl).astype(BF16)
            scores[ci] = scores[ci] + pmats[ch[4]][(1 + lvl) * c:(2 + lvl) * c] * _dot_nt(qt, kt)
    return [(_dot(a.astype(BF16), vb), qd, inc, tot)
            for a, vb, qd, inc, tot in zip(scores, vbs, qds, incs, tots)]


def _gla_apply(part, st):
    intra, qd, inc, tot = part
    return intra + _dot_nt(qd, st.astype(BF16)), st * tot + inc


def _hgrn_kernel(qx_ref, kfx_ref, gfx_ref, kbx_ref, gbx_ref, vx_ref, sgx_ref,
                 qc_ref, kfc_ref, gfc_ref, kbc_ref, gbc_ref, vc_ref, sgc_ref,
                 gain_ref, ef_ref, eb_ref, pf_ref, pb_ref, ox_ref, oc_ref, accx_ref, accc_ref):
    c = HGRN_CHUNK
    emats = {False: ef_ref[...].astype(BF16), True: eb_ref[...].astype(BF16)}
    pmats = {False: pf_ref[...], True: pb_ref[...]}
    heads = qx_ref.shape[1] // LANES

    def sweep(q_ref, kf_ref, gf_ref, kb_ref, gb_ref, v_ref, acc_ref, states):
        nchunks = q_ref.shape[0] // c
        acc_ref[...] = jnp.zeros(acc_ref.shape, F32)

        per = HGRN_CHUNKS_PER_STEP
        assert nchunks % per == 0

        def body(i, carry):
            starts = {False: [pl.multiple_of((i * per + j) * c, c) for j in range(per)],
                      True: [pl.multiple_of((nchunks - 1 - i * per - j) * c, c) for j in range(per)]}
            chains = []
            for h in range(heads):
                cols = slice(h * LANES, (h + 1) * LANES)
                for rev, k_ref, g_ref in ((False, kf_ref, gf_ref), (True, kb_ref, gb_ref)):
                    for r0 in starts[rev]:
                        chains.append(tuple(ref[pl.ds(r0, c), cols].astype(F32)
                                            for ref in (q_ref, k_ref, g_ref, v_ref)) + (rev,))
            parts = iter(_gla_chunks(chains, emats, pmats))
            new_states = []
            for h in range(heads):
                cols = slice(h * LANES, (h + 1) * LANES)
                sts = []
                for rev in (False, True):
                    st = carry[h][rev]
                    for r0 in starts[rev]:
                        out, st = _gla_apply(next(parts), st)
                        acc_ref[pl.ds(r0, c), cols] += out
                    sts.append(st)
                new_states.append(tuple(sts))
            return tuple(new_states)

        return lax.fori_loop(0, nchunks // per, body, states)

    zero = jnp.zeros((LANES, LANES), F32)
    states = sweep(qc_ref, kfc_ref, gfc_ref, kbc_ref, gbc_ref, vc_ref, accc_ref,
                   tuple((zero, zero) for _ in range(heads)))
    sweep(qx_ref, kfx_ref, gfx_ref, kbx_ref, gbx_ref, vx_ref, accx_ref, states)

    def finish(acc_ref, sg_ref, o_ref):
        rows = acc_ref.shape[0]
        tile = min(rows, 512)

        def body(i, _):
            r = pl.multiple_of(i * tile, tile)
            for h in range(heads):
                cols = slice(h * LANES, (h + 1) * LANES)
                o = acc_ref[pl.ds(r, tile), cols]
                ms = jnp.mean(o * o, axis=-1, keepdims=True)
                o = o * lax.rsqrt(ms + EPS) * gain_ref[...]
                o_ref[pl.ds(r, tile), cols] = (o * sg_ref[pl.ds(r, tile), cols].astype(F32)).astype(BF16)
            return 0

        lax.fori_loop(0, rows // tile, body, 0)

    finish(accx_ref, sgx_ref, ox_ref)
    finish(accc_ref, sgc_ref, oc_ref)


def _hgrn(px, pc, out_gain):
    bsz, n, width = px[0].shape
    m = pc[0].shape[1]
    c = HGRN_CHUNK
    (ef, pf), (eb, pb) = _gla_tables(False), _gla_tables(True)

    bw = HGRN_HEADS_PER_STEP * LANES

    def spec(length):
        return pl.BlockSpec((None, length, bw), lambda b, h: (b, 0, h))

    const = lambda shape: pl.BlockSpec(shape, lambda b, h: (0, 0))
    return pl.pallas_call(
        _hgrn_kernel,
        grid=(bsz, width // bw),
        in_specs=[spec(n)] * 7 + [spec(m)] * 7 + [const((1, LANES)), const(ef.shape), const(eb.shape),
                                                   const(pf.shape), const(pb.shape)],
        out_specs=[spec(n), spec(m)],
        out_shape=[jax.ShapeDtypeStruct((bsz, n, width), BF16), jax.ShapeDtypeStruct((bsz, m, width), BF16)],
        scratch_shapes=[pltpu.VMEM((n, bw), F32), pltpu.VMEM((m, bw), F32)],
        compiler_params=_cparams(("parallel", "parallel")),
        name="hgrn2",
    )(*px, *pc, out_gain.reshape(1, LANES).astype(F32), ef, eb, pf, pb)


def _na_bias_tables(rpb, rows):
    qrows, band, heads = NA_QROWS, NA_BAND, rpb.shape[0]
    pad = GRID_W - NA_COLS
    wide = jnp.pad(rpb.astype(F32), ((0, 0), (0, 0), (pad, pad)))
    toeplitz = jnp.stack([wide[:, :, GRID_W - 1 - qc:2 * GRID_W - 1 - qc] for qc in range(GRID_W)], axis=2)
    qc = np.arange(GRID_W)[:, None]
    kc = np.arange(GRID_W)[None, :]
    cstart = np.clip(qc - NA_COLS // 2, 0, GRID_W - NA_COLS)
    col_ok = (kc >= cstart) & (kc < cstart + NA_COLS)
    toeplitz = jnp.where(col_ok[None, None], toeplitz, NEG_BIG)
    tabs = []
    for r0 in (0, qrows, rows - qrows):
        rs = min(max(r0 - NA_ROWS // 2, 0), rows - band)
        qr = r0 + np.arange(qrows)[:, None]
        kr = rs + np.arange(band)[None, :]
        rstart = np.clip(qr - NA_ROWS // 2, 0, rows - NA_ROWS)
        row_ok = (kr >= rstart) & (kr < rstart + NA_ROWS)
        dr = np.clip(kr - qr + NA_ROWS - 1, 0, 2 * NA_ROWS - 2)
        tiles = jnp.take(toeplitz, jnp.asarray(dr.reshape(-1), I32), axis=1)
        tiles = jnp.where(row_ok.reshape(1, -1, 1, 1), tiles, NEG_BIG)
        tiles = tiles.reshape(heads, qrows, band, GRID_W, GRID_W).transpose(0, 1, 3, 2, 4)
        tabs.append(tiles.reshape(heads, qrows * GRID_W, band * GRID_W))
    return jnp.stack(tabs)


def _na_kernel(q_ref, k_ref, v_ref, kc_ref, vc_ref, bias_ref, o_ref, *, rows):
    j = pl.program_id(2)
    tq = q_ref.shape[0]
    band = NA_BAND * GRID_W
    rs = jnp.clip(j * NA_QROWS - NA_ROWS // 2, 0, rows - NA_BAND)
    start = pl.multiple_of(rs * GRID_W, NA_ROWS // 2 * GRID_W)
    q = q_ref[...]
    lo = _lane_ids(q.shape) < HEAD_DIM
    zero = jnp.zeros_like(q)
    qh = [jnp.where(lo, q, zero), jnp.where(lo, zero, q)]
    kb = k_ref[pl.ds(start, band), :]
    kc = kc_ref[...]

    def with_ones(v):
        ones = (lax.broadcasted_iota(I32, v.shape, 1) == 0).astype(BF16)
        return jnp.concatenate([v, ones], axis=1)

    vb = with_ones(v_ref[pl.ds(start, band), :])
    vc = with_ones(vc_ref[...])
    s_win = [_dot_nt(qh[h], kb) + bias_ref[h] for h in range(2)]
    s_ctx = [_dot_nt(qh[h], kc) for h in range(2)]
    m = [jnp.maximum(jnp.max(s_win[h], axis=-1, keepdims=True), jnp.max(s_ctx[h], axis=-1, keepdims=True))
         for h in range(2)]
    p_win = [jnp.exp((s_win[h] - m[h]).astype(BF16)) for h in range(2)]
    p_ctx = [jnp.exp((s_ctx[h] - m[h]).astype(BF16)) for h in range(2)]
    pv = [_dot(p_ctx[h], vc) + _dot(p_win[h], vb) for h in range(2)]
    o = [pv[h][:, 0:LANES] / pv[h][:, LANES:LANES + 1] for h in range(2)]
    o_ref[...] = jnp.where(lo, o[0], o[1]).astype(BF16)


def _na_attention(q, k, v, kc, vc, rpb):
    bsz, n, width = q.shape
    rows = n // GRID_W
    tq = NA_QROWS * GRID_W
    nt = n // tq
    bias = _na_bias_tables(rpb, rows)
    m = kc.shape[1]

    def cls(j):
        return jnp.where(j == 0, 0, jnp.where(j == nt - 1, 2, 1))

    full = lambda length: pl.BlockSpec((None, length, LANES), lambda b, h, j: (b, 0, h))
    return pl.pallas_call(
        functools.partial(_na_kernel, rows=rows),
        grid=(bsz, NA_HEADS // 2, nt),
        in_specs=[pl.BlockSpec((None, tq, LANES), lambda b, h, j: (b, j, h)),
                  full(n), full(n), full(m), full(m),
                  pl.BlockSpec((None, 2, tq, NA_BAND * GRID_W), lambda b, h, j: (cls(j), h, 0, 0))],
        out_specs=pl.BlockSpec((None, tq, LANES), lambda b, h, j: (b, j, h)),
        out_shape=jax.ShapeDtypeStruct((bsz, n, width), BF16),
        compiler_params=_cparams(("parallel", "parallel", "parallel")),
        name="na_attn",
    )(q, k, v, kc, vc, bias)


def _out_proj_kernel(a_ref, b_ref, w_ref, x_ref, gate_ref, gain_ref, shift_ref, scale_ref, r_ref,
                     x1_ref, tok_ref, logit_ref):
    half = a_ref.shape[1]
    y = _dot(a_ref[...], w_ref[0:half, :]) + _dot(b_ref[...], w_ref[half:2 * half, :])
    x1 = x_ref[...] + gate_ref[...] * y
    x1_ref[...] = x1
    h = _norm_mod(x1, gain_ref[...], shift_ref[...], scale_ref[...])
    logit_ref[...] = jnp.dot(h, r_ref[...], precision=HIGHEST, preferred_element_type=F32)
    tok_ref[...] = _pack_pair(h[:, 0:HALF], h[:, HALF:2 * HALF])


def _out_proj(a, b, w, x, gate, gain, shift, scale, router_pad):
    bsz, n, d = x.shape
    tm = min(PROJ_ROWS, n)
    return pl.pallas_call(
        _out_proj_kernel,
        grid=(bsz, n // tm),
        in_specs=[_row_spec(tm, a.shape[2]), _row_spec(tm, b.shape[2]), _const_spec(w.shape), _row_spec(tm, d),
                  _bcast_spec(d), _const_spec((1, d)), _bcast_spec(d), _bcast_spec(d), _const_spec(router_pad.shape)],
        out_specs=[_row_spec(tm, d), _row_spec(tm, HALF), _row_spec(tm, LANES)],
        out_shape=[jax.ShapeDtypeStruct((bsz, n, d), F32), jax.ShapeDtypeStruct((bsz, n, HALF), I32),
                   jax.ShapeDtypeStruct((bsz, n, LANES), F32)],
        compiler_params=_cparams(("parallel", "parallel")),
        name="out_proj",
    )(a, b, w, x, gate, gain.reshape(1, d), shift, scale, router_pad)


def _route_kernel(logit_ref, bias_ref, tri_ref, etri_ref, w_ref, loc_ref, tcnt_ref, tcarry_ref, toff_ref, cnt_ref,
                  masked_ref, carry_ref):
    step = pl.program_id(0)

    @pl.when(step == 0)
    def _():
        carry_ref[...] = jnp.zeros_like(carry_ref)

    tr = logit_ref.shape[0]
    scores = jax.nn.sigmoid(logit_ref[...].T[0:N_EXPERTS, :])
    biased = scores + bias_ref[...]
    gsz = EXPERTS_PER_GROUP
    sub = lax.broadcasted_iota(I32, (gsz, tr), 0).astype(F32)
    gscore = []
    for g in range(N_GROUPS):
        bg = biased[g * gsz:(g + 1) * gsz, :]
        m1 = jnp.max(bg, axis=0, keepdims=True)
        i1 = jnp.min(jnp.where(bg == m1, sub, float(gsz)), axis=0, keepdims=True)
        m2 = jnp.max(jnp.where(sub == i1, -jnp.inf, bg), axis=0, keepdims=True)
        gscore.append(m1 + m2)
    for g in range(N_GROUPS):
        beaten = jnp.zeros((1, tr), F32)
        for o in range(N_GROUPS):
            if o == g:
                continue
            wins = (gscore[o] >= gscore[g]) if o < g else (gscore[o] > gscore[g])
            beaten = beaten + jnp.where(wins, 1.0, 0.0)
        keep = beaten < float(TOPK_GROUPS)
        masked_ref[g * gsz:(g + 1) * gsz, :] = jnp.where(keep, biased[g * gsz:(g + 1) * gsz, :], -jnp.inf)
    cur = masked_ref[...]
    eid = lax.broadcasted_iota(I32, (N_EXPERTS, tr), 0).astype(F32)
    sel = jnp.zeros((N_EXPERTS, tr), F32)
    picks, weights = [], []
    for _ in range(TOP_K):
        m = jnp.max(cur, axis=0, keepdims=True)
        ik = jnp.min(jnp.where(cur == m, eid, float(N_EXPERTS)), axis=0, keepdims=True)
        hit = eid == ik
        weights.append(jnp.sum(jnp.where(hit, scores, 0.0), axis=0, keepdims=True))
        sel = sel + jnp.where(hit, 1.0, 0.0)
        cur = jnp.where(hit, -jnp.inf, cur)
        picks.append(ik)
    wsum = weights[0]
    for wk in weights[1:]:
        wsum = wsum + wk
    tile_cnt = jnp.broadcast_to(jnp.sum(sel, axis=1, keepdims=True), (N_EXPERTS, LANES))
    tile_cnt = jnp.floor((tile_cnt + (RUN_ALIGN - 1)) * (1.0 / RUN_ALIGN)) * RUN_ALIGN
    tile_off = jnp.dot(etri_ref[...], tile_cnt, precision=HIGHEST, preferred_element_type=F32)
    row = _dot(sel.astype(BF16), tri_ref[...]) + tile_off[:, 0:1]
    for kk in range(TOP_K):
        w_ref[kk:kk + 1, :] = weights[kk] / wsum * ROUTED_SCALE
        loc_ref[kk:kk + 1, :] = jnp.sum(jnp.where(eid == picks[kk], row, 0.0), axis=0, keepdims=True).astype(I32)
    tcnt_ref[...] = tile_cnt.astype(I32)
    tcarry_ref[...] = carry_ref[...].astype(I32)
    toff_ref[...] = tile_off.astype(I32)
    carry_ref[...] = carry_ref[...] + tile_cnt
    cnt_ref[...] = carry_ref[...].astype(I32)


def _route(logits, router_bias):
    t = logits.shape[0]
    tr = MOE_TILE
    assert t % tr == 0
    nt = t // tr
    r = jnp.arange(tr)
    tri = (r[:, None] < r[None, :]).astype(BF16)
    e = jnp.arange(N_EXPERTS)
    etri = (e[:, None] > e[None, :]).astype(F32)
    kt_spec = pl.BlockSpec((TOP_K, tr), lambda i: (0, i))
    tile_spec = pl.BlockSpec((None, N_EXPERTS, LANES), lambda i: (i, 0, 0))
    tile_shape = jax.ShapeDtypeStruct((nt, N_EXPERTS, LANES), I32)
    return pl.pallas_call(
        _route_kernel,
        grid=(nt,),
        in_specs=[pl.BlockSpec((tr, LANES), lambda i: (i, 0)),
                  pl.BlockSpec((N_EXPERTS, 1), lambda i: (0, 0)),
                  pl.BlockSpec((tr, tr), lambda i: (0, 0)),
                  pl.BlockSpec((N_EXPERTS, N_EXPERTS), lambda i: (0, 0))],
        out_specs=[kt_spec, kt_spec, tile_spec, tile_spec, tile_spec,
                   pl.BlockSpec((N_EXPERTS, LANES), lambda i: (0, 0))],
        out_shape=[jax.ShapeDtypeStruct((TOP_K, t), F32), jax.ShapeDtypeStruct((TOP_K, t), I32),
                   tile_shape, tile_shape, tile_shape, jax.ShapeDtypeStruct((N_EXPERTS, LANES), I32)],
        scratch_shapes=[pltpu.VMEM((N_EXPERTS, tr), F32), pltpu.VMEM((N_EXPERTS, LANES), F32)],
        compiler_params=_cparams(("arbitrary",)),
        name="moe_route",
    )(logits, router_bias.astype(F32).reshape(N_EXPERTS, 1), tri, etri)


TAB_WORDS = 1024
TAB_FIELD = 128
N_RUNS = N_EXPERTS + 1
TILE_ROWS = -(-(MOE_TILE * TOP_K + N_EXPERTS * (RUN_ALIGN - 1)) // MOE_TILE) * MOE_TILE
FILLER_ROWS = TILE_ROWS - MOE_TILE * TOP_K
RUN_PIECE = 64


def _run_copies(tab_ref, local_ref, global_ref, sem, to_global):
    def piece(ls, gs, off, rows, priority):
        loc = local_ref.at[pl.ds(pl.multiple_of(ls + off, RUN_ALIGN), rows)]
        glo = global_ref.at[pl.ds(pl.multiple_of(gs + off, RUN_ALIGN), rows)]
        src, dst = (loc, glo) if to_global else (glo, loc)
        pltpu.make_async_copy(src, dst, sem).start(priority=priority)

    def body(e, carry):
        gs = tab_ref[e]
        c = tab_ref[TAB_FIELD + e]
        ls = tab_ref[2 * TAB_FIELD + e]
        whole = c >> (RUN_PIECE.bit_length() - 1)

        def big(j, carry2):
            piece(ls, gs, j * RUN_PIECE, RUN_PIECE, 0)
            return carry2

        lax.fori_loop(0, whole, big, 0)
        for b in range(RUN_ALIGN.bit_length() - 1, RUN_PIECE.bit_length() - 1):
            @pl.when(((c >> b) & 1) == 1)
            def _(b=b):
                piece(ls, gs, whole * RUN_PIECE + (c & ((1 << b) - 1) & (RUN_PIECE - 1)), 1 << b, 1)
        return carry

    lax.fori_loop(0, N_RUNS, body, 0)


def _run_wait(local_ref, global_ref, sem):
    pltpu.make_async_copy(global_ref.at[pl.ds(0, TILE_ROWS)], local_ref, sem).wait()


def _dispatch_kernel(tab_ref, loc_ref, tok_ref, rows_in_ref, rows_ref, srt_ref, sem, *, n_tiles):
    del rows_in_ref
    i = pl.program_id(0)
    slot = i % 2
    tt = tok_ref.shape[0]
    lo, hi = _unpack_pair(tok_ref[...])
    lo, hi = lo.astype(BF16), hi.astype(BF16)
    loc = loc_ref[...]
    blk = loc >> (tt.bit_length() - 1)
    low = (loc & (tt - 1)).astype(F32).astype(BF16)
    r = lax.broadcasted_iota(I32, (tt, tt), 0).astype(F32).astype(BF16)
    one = jnp.ones((tt, tt), BF16)
    for rb in range(TILE_ROWS // tt):
        pb = jnp.zeros((tt, tt), BF16)
        for kk in range(TOP_K):
            lk = jnp.where(blk[kk:kk + 1, :] == rb, low[kk:kk + 1, :], jnp.asarray(-1.0, BF16))
            pb = jnp.where(lk == r, one, pb)
        srt_ref[slot, rb * tt:(rb + 1) * tt, :] = _pack_exact_pair(_dot(pb, lo), _dot(pb, hi))

    @pl.when(i > 0)
    def _():
        _run_wait(srt_ref.at[1 - slot], rows_ref, sem.at[1 - slot])

    _run_copies(tab_ref, srt_ref.at[slot], rows_ref, sem.at[slot], True)

    @pl.when(i == n_tiles - 1)
    def _():
        _run_wait(srt_ref.at[slot], rows_ref, sem.at[slot])


def _dispatch(tab, loc, tok, rows_buf):
    t = tok.shape[0]
    tt = MOE_TILE
    return pl.pallas_call(
        functools.partial(_dispatch_kernel, n_tiles=t // tt),
        grid=(t // tt,),
        in_specs=[pl.BlockSpec((TAB_WORDS,), lambda i: (i,), memory_space=pltpu.SMEM),
                  pl.BlockSpec((TOP_K, tt), lambda i: (0, i)),
                  pl.BlockSpec((tt, HALF), lambda i: (i, 0)),
                  pl.BlockSpec(memory_space=pl.ANY)],
        out_specs=pl.BlockSpec(memory_space=pl.ANY),
        out_shape=jax.ShapeDtypeStruct(rows_buf.shape, rows_buf.dtype),
        scratch_shapes=[pltpu.VMEM((2, TILE_ROWS, HALF), I32), pltpu.SemaphoreType.DMA((2,))],
        input_output_aliases={3: 0},
        compiler_params=_cparams(("arbitrary",)),
        name="moe_dispatch",
    )(tab, loc, tok, rows_buf)


def _expert_kernel(be_ref, nused_ref, x_ref, wgu_ref, wd_ref, y_ref):
    @pl.when(pl.program_id(0) < nused_ref[0])
    def _():
        lo, hi = _unpack_pair(x_ref[...])
        gu = _dot(lo.astype(BF16), wgu_ref[0:HALF, :]) + _dot(hi.astype(BF16), wgu_ref[HALF:2 * HALF, :])
        h = (_silu(gu[:, 0:EXPERT_DIM]) * gu[:, EXPERT_DIM:2 * EXPERT_DIM]).astype(BF16)
        y = _dot(h, wd_ref[...])
        y_ref[...] = _pack_pair(y[:, 0:HALF], y[:, HALF:2 * HALF])

    @pl.when(pl.program_id(0) >= nused_ref[0])
    def _():
        y_ref[...] = jnp.zeros_like(y_ref)


def _experts(block_expert, nused, rows, wgu, wd):
    n_rows = rows.shape[0]
    nb = n_rows // MOE_BLOCK

    def row_map(i, be, nu):
        return (jnp.minimum(i, nu[0] - 1), 0)

    def w_map(i, be, nu):
        return (be[jnp.minimum(i, nu[0] - 1)], 0, 0)

    grid_spec = pltpu.PrefetchScalarGridSpec(
        num_scalar_prefetch=2,
        grid=(nb,),
        in_specs=[pl.BlockSpec((MOE_BLOCK, HALF), row_map),
                  pl.BlockSpec((None, D_MODEL, 2 * EXPERT_DIM), w_map),
                  pl.BlockSpec((None, EXPERT_DIM, D_MODEL), w_map)],
        out_specs=pl.BlockSpec((MOE_BLOCK, HALF), lambda i, be, nu: (i, 0)),
    )
    return pl.pallas_call(
        _expert_kernel,
        grid_spec=grid_spec,
        out_shape=jax.ShapeDtypeStruct((n_rows, HALF), I32),
        compiler_params=_cparams(("arbitrary",)),
        name="moe_experts",
    )(block_expert, nused, rows, wgu, wd)


def _combine_kernel(tab_ref, tabn_ref, x1_ref, tok_ref, loc_ref, w_ref, gate_ref, wgu_ref, wd_ref, y_ref, o_ref,
                    buf_ref, sem, *, n_tiles):
    i = pl.program_id(0)
    slot = i % 2
    tt = tok_ref.shape[0]

    @pl.when(i == 0)
    def _():
        _run_copies(tab_ref, buf_ref.at[slot], y_ref, sem.at[slot], False)

    @pl.when(i + 1 < n_tiles)
    def _():
        _run_copies(tabn_ref, buf_ref.at[1 - slot], y_ref, sem.at[1 - slot], False)

    lo, hi = _unpack_pair(tok_ref[...])
    gu = _dot(lo.astype(BF16), wgu_ref[0:HALF, :]) + _dot(hi.astype(BF16), wgu_ref[HALF:2 * HALF, :])
    h = (_silu(gu[:, 0:EXPERT_DIM]) * gu[:, EXPERT_DIM:2 * EXPERT_DIM]).astype(BF16)
    shared = _dot(h, wd_ref[...])
    _run_wait(buf_ref.at[slot], y_ref, sem.at[slot])
    acc_lo = shared[:, 0:HALF]
    acc_hi = shared[:, HALF:2 * HALF]
    loc = loc_ref[...]
    w = w_ref[...].astype(BF16)
    blk = loc >> (tt.bit_length() - 1)
    low = (loc & (tt - 1)).astype(F32).astype(BF16)
    r = lax.broadcasted_iota(I32, (tt, tt), 1).astype(F32).astype(BF16)
    for rb in range(TILE_ROWS // tt):
        wb = jnp.zeros((tt, tt), BF16)
        for kk in range(TOP_K):
            lk = jnp.where(blk[:, kk:kk + 1] == rb, low[:, kk:kk + 1], jnp.asarray(-1.0, BF16))
            wb = jnp.where(lk == r, jnp.broadcast_to(w[:, kk:kk + 1], (tt, tt)), wb)
        ylo, yhi = _unpack_pair(buf_ref[slot, rb * tt:(rb + 1) * tt, :])
        acc_lo = acc_lo + _dot(wb, ylo.astype(BF16))
        acc_hi = acc_hi + _dot(wb, yhi.astype(BF16))
    gate = gate_ref[...]
    o_ref[:, 0:HALF] = x1_ref[:, 0:HALF] + gate[:, 0:HALF] * acc_lo
    o_ref[:, HALF:2 * HALF] = x1_ref[:, HALF:2 * HALF] + gate[:, HALF:2 * HALF] * acc_hi


def _combine(tab, x1, tok, loc_tok, w_tok, gate, wgu, wd, y_rows, tokens_per_gate):
    t, d = x1.shape
    tt = MOE_TILE
    nt = t // tt
    per = tokens_per_gate // tt
    return pl.pallas_call(
        functools.partial(_combine_kernel, n_tiles=nt),
        grid=(nt,),
        in_specs=[pl.BlockSpec((TAB_WORDS,), lambda i: (i,), memory_space=pltpu.SMEM),
                  pl.BlockSpec((TAB_WORDS,), lambda i: (jnp.minimum(i + 1, nt - 1),), memory_space=pltpu.SMEM),
                  pl.BlockSpec((tt, d), lambda i: (i, 0)),
                  pl.BlockSpec((tt, HALF), lambda i: (i, 0)),
                  pl.BlockSpec((tt, TOP_K), lambda i: (i, 0)),
                  pl.BlockSpec((tt, TOP_K), lambda i: (i, 0)),
                  pl.BlockSpec((None, 1, d), lambda i: (i // per, 0, 0)),
                  pl.BlockSpec(wgu.shape, lambda i: (0, 0)),
                  pl.BlockSpec(wd.shape, lambda i: (0, 0)),
                  pl.BlockSpec(memory_space=pl.ANY)],
        out_specs=pl.BlockSpec((tt, d), lambda i: (i, 0)),
        out_shape=jax.ShapeDtypeStruct((t, d), F32),
        scratch_shapes=[pltpu.VMEM((2, TILE_ROWS, HALF), I32), pltpu.SemaphoreType.DMA((2,))],
        compiler_params=_cparams(("arbitrary",)),
        name="moe_combine",
    )(tab, tab, x1, tok, loc_tok, w_tok, gate, wgu, wd, y_rows)


def _moe(parts, router_bias, w_gate, w_up, w_down, ws_gate, ws_up, ws_down, row_buf=None):
    logits = jnp.concatenate([p[2].reshape(-1, LANES) for p in parts], axis=0)
    t = logits.shape[0]
    w, loc, tile_cnt, tile_carry, tile_off, cnt = _route(logits, router_bias)
    counts = cnt[:, 0]
    padded = (counts + MOE_BLOCK - 1) // MOE_BLOCK * MOE_BLOCK
    pad_end = jnp.cumsum(padded)
    pad_start = pad_end - padded
    nt = t // MOE_TILE
    max_aligned = t * TOP_K + nt * N_EXPERTS * (RUN_ALIGN - 1)
    area_rows = -(-max_aligned // MOE_BLOCK) * MOE_BLOCK + N_EXPERTS * MOE_BLOCK
    n_rows = -(-(area_rows + FILLER_ROWS) // MOE_BLOCK) * MOE_BLOCK
    if row_buf is not None:
        assert row_buf.shape[0] >= n_rows
        n_rows = row_buf.shape[0]
    tile_rows_used = tile_off[:, N_EXPERTS - 1, 0] + tile_cnt[:, N_EXPERTS - 1, 0]

    def field(per_expert, filler):
        vals = jnp.concatenate([per_expert.astype(I32), filler.astype(I32)[:, None]], axis=1)
        return jnp.pad(vals, ((0, 0), (0, TAB_FIELD - N_RUNS)))

    tab = jnp.concatenate(
        [field(pad_start[None, :] + tile_carry[:, :, 0], jnp.full((nt,), area_rows, I32)),
         field(tile_cnt[:, :, 0], TILE_ROWS - tile_rows_used),
         field(tile_off[:, :, 0], tile_rows_used),
         jnp.zeros((nt, TAB_WORDS - 3 * TAB_FIELD), I32)], axis=1).reshape(-1)
    w_tok = w.T
    loc_tok = loc.T
    nb = n_rows // MOE_BLOCK
    block_start = jnp.arange(nb, dtype=I32) * MOE_BLOCK
    block_expert = jnp.minimum(jnp.sum(block_start[:, None] >= pad_end[None, :], axis=1), N_EXPERTS - 1).astype(I32)
    nused = (pad_end[-1] // MOE_BLOCK).astype(I32).reshape(1)
    rows = jnp.zeros((n_rows, HALF), I32) if row_buf is None else row_buf
    off = 0
    for x1, tok, _, _, _ in parts:
        cnt_tok = tok.shape[0] * tok.shape[1]
        part_tab = tab[off // MOE_TILE * TAB_WORDS:(off + cnt_tok) // MOE_TILE * TAB_WORDS]
        rows = _dispatch(part_tab, loc[:, off:off + cnt_tok], tok.reshape(cnt_tok, HALF), rows)
        off += cnt_tok
    wgu = jnp.concatenate([w_gate, w_up], axis=-1).astype(BF16)
    y_rows = _experts(block_expert, nused, rows, wgu, w_down.astype(BF16))
    wsgu = jnp.concatenate([ws_gate, ws_up], axis=-1).astype(BF16)
    wsd = ws_down.astype(BF16)
    outs = []
    off = 0
    for x1, tok, _, gate, per in parts:
        cnt_tok = tok.shape[0] * tok.shape[1]
        part_tab = tab[off // MOE_TILE * TAB_WORDS:(off + cnt_tok) // MOE_TILE * TAB_WORDS]
        o = _combine(part_tab, x1.reshape(cnt_tok, D_MODEL), tok.reshape(cnt_tok, HALF),
                     loc_tok[off:off + cnt_tok], w_tok[off:off + cnt_tok], gate, wsgu, wsd, y_rows, per)
        outs.append(o.reshape(x1.shape))
        off += cnt_tok
    return outs, rows


def kernel(x, c, ctx, c_ctx, ada_w, ada_b, norm_mix, norm_ffn, ev_w_in, ev_w_out, diff_qk_gain, diff_lambda,
           diff_out_gain, hgrn_lb, hgrn_out_gain, od_w_in, od_w_out, gqa_qk_gain, na_qk_gain, na_rpb, moe_router,
           moe_router_bias, moe_w_gate, moe_w_up, moe_w_down, shared_w_gate, shared_w_up, shared_w_down):
    bsz, n, d = x.shape
    m = ctx.shape[1]
    depth = ada_w.shape[0]
    cond_rows = -(-(bsz + 1) // 8) * 8
    cond = jnp.zeros((cond_rows, d), F32).at[0:bsz].set(c).at[bsz].set(c_ctx)
    mods = _ada_mod(cond, ada_w, ada_b)

    xc = ctx
    row_buf = None
    for layer in range(depth):
        need_ctx = layer < depth - 1
        j = layer // 2
        mod = mods[layer].reshape(cond_rows, 6, d)
        mx = [mod[0:bsz, i][:, None, :] for i in range(6)]
        mc = [jnp.broadcast_to(mod[bsz:bsz + 1, i][:, None, :], (bsz, 1, d)) for i in range(6)]
        if layer % 2 == 0:
            w_in = ev_w_in[j].astype(BF16)
            px = _even_proj(x, mx[0], mx[1], norm_mix[layer], w_in, diff_qk_gain[j], hgrn_lb[0], hgrn_lb[1], j, True)
            pc = _even_proj(xc, mc[0], mc[1], norm_mix[layer], w_in, diff_qk_gain[j], hgrn_lb[0], hgrn_lb[1], j, False)
            lam_init = 0.8 - 0.6 * math.exp(-0.3 * layer)
            a_x = _diff_attention(px[0], [(pc[1], pc[2]), (px[1], px[2])], diff_lambda[j], diff_out_gain[j], lam_init)
            a_c = _diff_attention(pc[0], [(pc[1], pc[2])], diff_lambda[j], diff_out_gain[j], lam_init)
            b_x, b_c = _hgrn(px[3:], pc[3:], hgrn_out_gain[j])
            w_out = ev_w_out[j].astype(BF16)
        else:
            w_in = od_w_in[j].astype(BF16)
            px = _odd_proj(x, mx[0], mx[1], norm_mix[layer], w_in, gqa_qk_gain[j], na_qk_gain[j], True)
            pc = _odd_proj(xc, mc[0], mc[1], norm_mix[layer], w_in, gqa_qk_gain[j], na_qk_gain[j], False)
            a_x = _gqa_attention(px[0], pc[1], px[1])
            b_x = _na_attention(px[2], px[3], px[4], pc[3], pc[4], na_rpb[j])
            a_c = b_c = None
            w_out = od_w_out[j].astype(BF16)
        router_pad = jnp.zeros((d, LANES), F32).at[:, 0:N_EXPERTS].set(moe_router[layer].astype(F32))
        x1, tok_x, logit_x = _out_proj(a_x, b_x, w_out, x, mx[2], norm_ffn[layer], mx[3], mx[4], router_pad)
        parts = [(x1, tok_x, logit_x, mx[5], n)]
        if need_ctx:
            xc1, tok_c, logit_c = _out_proj(a_c, b_c, w_out, xc, mc[2], norm_ffn[layer], mc[3], mc[4], router_pad)
            parts.append((xc1, tok_c, logit_c, mc[5][0:1], bsz * m))
        outs, row_buf = _moe(parts, moe_router_bias[layer], moe_w_gate[layer], moe_w_up[layer], moe_w_down[layer],
                             shared_w_gate[layer], shared_w_up[layer], shared_w_down[layer], row_buf)
        x = outs[0]
        if need_ctx:
            xc = outs[1]
    return x
```

```python
import functools
import math

import jax
import jax.numpy as jnp
import numpy as np
from jax import lax
from jax.experimental import pallas as pl
from jax.experimental.pallas import tpu as pltpu

F32 = jnp.float32
BF16 = jnp.bfloat16
I32 = jnp.int32
HIGHEST = lax.Precision.HIGHEST

D_MODEL = 1024
GRID_W = 64
HEAD_DIM = 64
ATTN_SCALE = HEAD_DIM ** -0.5
ROPE_THETA = 10000.0
EPS = 1e-6
DIFF_HEADS = D_MODEL // 256
HGRN_HEADS = D_MODEL // 256
HGRN_CHUNK = 64
GQA_HEADS = D_MODEL // 128
GQA_KV_HEADS = GQA_HEADS // 4
NA_HEADS = D_MODEL // 128
NA_ROWS = 8
NA_COLS = 16
N_EXPERTS = 64
N_GROUPS = 8
EXPERTS_PER_GROUP = N_EXPERTS // N_GROUPS
TOPK_GROUPS = 4
TOP_K = 8
EXPERT_DIM = D_MODEL // 4
ROUTED_SCALE = 2.5
HALF = D_MODEL // 2

LANES = 128
VMEM_LIMIT_BYTES = 56 * 1024 * 1024
PROJ_ROWS = 512
FLASH_TK = 2048
FLASH_SB = 64
DIFF_TQ = 512
GQA_TQ = 256
NA_QROWS = 8
NA_BAND = 16
HGRN_HEADS_PER_STEP = 2
HGRN_CHUNKS_PER_STEP = 2
MOE_BLOCK = 1024
MOE_TILE = 256
RUN_ALIGN = 8
NEG_BIG = -1e30


def _cparams(sem):
    return pltpu.CompilerParams(dimension_semantics=sem, vmem_limit_bytes=VMEM_LIMIT_BYTES)


def _silu(x):
    return x * jax.nn.sigmoid(x)


def _dot(a, b):
    return jnp.dot(a, b, preferred_element_type=F32)


def _dot_nt(a, b):
    return lax.dot_general(a, b, (((1,), (1,)), ((), ())), preferred_element_type=F32)


def _dot_tn(a, b):
    return lax.dot_general(a, b, (((0,), (0,)), ((), ())), preferred_element_type=F32)


def _pack_pair(lo, hi):
    lo_bits = lax.bitcast_convert_type(lo.astype(BF16).astype(F32), I32)
    hi_bits = lax.bitcast_convert_type(hi.astype(BF16).astype(F32), I32)
    return lax.shift_right_logical(lo_bits, 16) | (hi_bits & jnp.int32(-65536))


def _pack_exact_pair(lo, hi):
    lo_bits = lax.bitcast_convert_type(lo, I32)
    hi_bits = lax.bitcast_convert_type(hi, I32)
    return lax.shift_right_logical(lo_bits, 16) | (hi_bits & jnp.int32(-65536))


def _unpack_pair(w):
    lo = lax.bitcast_convert_type(lax.shift_left(w, 16), F32)
    hi = lax.bitcast_convert_type(w & jnp.int32(-65536), F32)
    return lo, hi


def _ada_kernel(cond_ref, w_ref, b_ref, o_ref):
    s = _silu(cond_ref[...])
    o_ref[...] = jnp.dot(s, w_ref[...], precision=HIGHEST, preferred_element_type=F32) + b_ref[...]


def _ada_mod(cond, ada_w, ada_b):
    depth = ada_w.shape[0]
    rows = cond.shape[0]
    nblk = ada_w.shape[2] // D_MODEL
    return pl.pallas_call(
        _ada_kernel,
        grid=(depth, nblk),
        in_specs=[
            pl.BlockSpec((rows, D_MODEL), lambda l, j: (0, 0)),
            pl.BlockSpec((None, D_MODEL, D_MODEL), lambda l, j: (l, 0, j)),
            pl.BlockSpec((None, 1, D_MODEL), lambda l, j: (l, 0, j)),
        ],
        out_specs=pl.BlockSpec((None, rows, D_MODEL), lambda l, j: (l, 0, j)),
        out_shape=jax.ShapeDtypeStruct((depth, rows, nblk * D_MODEL), F32),
        compiler_params=_cparams(("parallel", "parallel")),
        name="ada_mod",
    )(cond, ada_w, ada_b.reshape(depth, 1, -1))


def _norm_mod(x, gain, shift, scale):
    ms = jnp.mean(x * x, axis=-1, keepdims=True)
    h = x * lax.rsqrt(ms + EPS) * gain
    return h * (1.0 + scale) + shift


def _seg_rms(acc, gain, bd):
    sq = acc * acc
    hi = sq.astype(BF16)
    lo = (sq - hi.astype(F32)).astype(BF16)
    ms = _dot(hi, bd) + _dot(lo, bd)
    return acc * lax.rsqrt(ms + EPS) * gain


def _rope(y, c, sm, sp):
    return y * c + pltpu.roll(y, LANES - 16, 1) * sm + pltpu.roll(y, 16, 1) * sp


def _rope_tables(n):
    pos = jnp.arange(n, dtype=I32)
    row = (pos // GRID_W).astype(F32)
    col = (pos % GRID_W).astype(F32)
    axis_dim = HEAD_DIM // 2
    inv_freq = ROPE_THETA ** (-jnp.arange(0, axis_dim, 2, dtype=F32) / axis_dim)
    ang_row = row[:, None] * inv_freq
    ang_col = col[:, None] * inv_freq
    lane = jnp.arange(LANES)
    p = lane % axis_dim
    f = p % (axis_dim // 2)
    on_row = ((lane % HEAD_DIM) // axis_dim) == 0
    ang = jnp.where(on_row[None, :], ang_row[:, f], ang_col[:, f])
    c = jnp.cos(ang)
    s = jnp.sin(ang)
    first = (p < axis_dim // 2)[None, :]
    return c, jnp.where(first, -s, 0.0), jnp.where(first, 0.0, s)


def _seg_mean_matrix():
    r = jnp.arange(LANES)
    return jnp.where((r[:, None] // HEAD_DIM) == (r[None, :] // HEAD_DIM), 1.0 / HEAD_DIM, 0.0).astype(BF16)


def _even_proj_kernel(*refs, rope, layer_slot):
    if rope:
        (x_ref, shift_ref, scale_ref, gain_ref, w_ref, qkg_ref, lbf_ref, lbb_ref, bd_ref, rc_ref, rm_ref, rp_ref,
         dq_ref, dk_ref, dv_ref, hq_ref, kf_ref, gf_ref, kb_ref, gb_ref, hv_ref, hg_ref) = refs
        tables = (rc_ref[...], rm_ref[...], rp_ref[...])
    else:
        (x_ref, shift_ref, scale_ref, gain_ref, w_ref, qkg_ref, lbf_ref, lbb_ref, bd_ref,
         dq_ref, dk_ref, dv_ref, hq_ref, kf_ref, gf_ref, kb_ref, gb_ref, hv_ref, hg_ref) = refs
        tables = None
    hb = _norm_mod(x_ref[...], gain_ref[...], shift_ref[...], scale_ref[...]).astype(BF16)
    bd = bd_ref[...]
    width = 4 * LANES

    def proj(group):
        return _dot(hb, w_ref[:, group * width:(group + 1) * width])

    def qk(group, gain, out_ref, mult):
        acc = proj(group)
        for s in range(4):
            y = _seg_rms(acc[:, s * LANES:(s + 1) * LANES], gain, bd)
            if tables is not None:
                y = _rope(y, *tables)
            out_ref[:, s * LANES:(s + 1) * LANES] = (y * mult).astype(BF16)

    qk(0, qkg_ref[0:1, :], dq_ref, ATTN_SCALE)
    qk(1, qkg_ref[1:2, :], dk_ref, 1.0)
    dv_ref[...] = proj(2).astype(BF16)
    hq_ref[...] = _silu(proj(3)).astype(BF16)

    def forget(group, lb_ref, k_ref, g_ref):
        raw = lb_ref[...]
        e = jnp.exp(raw - jnp.max(raw, axis=0, keepdims=True))
        lb = jnp.sum(e[0:layer_slot + 1, :], axis=0, keepdims=True) / jnp.sum(e, axis=0, keepdims=True)
        f = lb + (1.0 - lb) * jax.nn.sigmoid(proj(group))
        k_ref[...] = (1.0 - f).astype(BF16)
        g_ref[...] = jnp.log(f)

    forget(4, lbf_ref, kf_ref, gf_ref)
    forget(5, lbb_ref, kb_ref, gb_ref)
    hv_ref[...] = proj(6).astype(BF16)
    hg_ref[...] = _silu(proj(7)).astype(BF16)


def _row_spec(tm, width):
    return pl.BlockSpec((None, tm, width), lambda b, i: (b, i, 0))


def _bcast_spec(width):
    return pl.BlockSpec((None, 1, width), lambda b, i: (b, 0, 0))


def _const_spec(shape):
    nd = len(shape)
    return pl.BlockSpec(shape, lambda b, i: (0,) * nd)


def _even_proj(x, shift, scale, gain, w, qk_gain, lb_fwd, lb_bwd, layer_slot, rope):
    bsz, n, d = x.shape
    tm = min(PROJ_ROWS, n)
    width = 4 * LANES
    qkg = jnp.tile(qk_gain.astype(F32), (1, 2))
    in_specs = [
        _row_spec(tm, d), _bcast_spec(d), _bcast_spec(d), _const_spec((1, d)), _const_spec(w.shape),
        _const_spec((2, LANES)), _const_spec(lb_fwd.shape), _const_spec(lb_bwd.shape), _const_spec((LANES, LANES)),
    ]
    args = [x, shift, scale, gain.reshape(1, d), w, qkg, lb_fwd, lb_bwd, _seg_mean_matrix()]
    if rope:
        tab_spec = pl.BlockSpec((tm, LANES), lambda b, i: (i, 0))
        in_specs += [tab_spec] * 3
        args += list(_rope_tables(n))
    out_dtypes = [BF16, BF16, BF16, BF16, BF16, F32, BF16, F32, BF16, BF16]
    return pl.pallas_call(
        functools.partial(_even_proj_kernel, rope=rope, layer_slot=layer_slot),
        grid=(bsz, n // tm),
        in_specs=in_specs,
        out_specs=[_row_spec(tm, width)] * len(out_dtypes),
        out_shape=[jax.ShapeDtypeStruct((bsz, n, width), dt) for dt in out_dtypes],
        compiler_params=_cparams(("parallel", "parallel")),
        name="even_proj_x" if rope else "even_proj_ctx",
    )(*args)


def _odd_proj_kernel(*refs, rope):
    if rope:
        (x_ref, shift_ref, scale_ref, gain_ref, w_ref, gqg_ref, nag_ref, bd_ref, rc_ref, rm_ref, rp_ref,
         gq_ref, gkv_ref, nq_ref, nk_ref, nv_ref) = refs
        tables = (rc_ref[...], rm_ref[...], rp_ref[...])
    else:
        (x_ref, shift_ref, scale_ref, gain_ref, w_ref, gqg_ref, nag_ref, bd_ref,
         gq_ref, gkv_ref, nq_ref, nk_ref, nv_ref) = refs
        tables = None
    hb = _norm_mod(x_ref[...], gain_ref[...], shift_ref[...], scale_ref[...]).astype(BF16)
    bd = bd_ref[...]

    def slab(acc, s, gain, use_rope, mult):
        y = _seg_rms(acc[:, s * LANES:(s + 1) * LANES], gain, bd)
        if use_rope and tables is not None:
            y = _rope(y, *tables)
        return (y * mult).astype(BF16)

    q_w = GQA_HEADS * HEAD_DIM
    acc = _dot(hb, w_ref[:, 0:q_w])
    for s in range(q_w // LANES):
        gq_ref[:, s * LANES:(s + 1) * LANES] = slab(acc, s, gqg_ref[0:1, :], True, ATTN_SCALE)
    acc = _dot(hb, w_ref[:, q_w:q_w + 2 * LANES])
    gkv_ref[:, 0:LANES] = slab(acc, 0, gqg_ref[1:2, :], True, 1.0)
    gkv_ref[:, LANES:2 * LANES] = acc[:, LANES:2 * LANES].astype(BF16)
    base = q_w + 2 * LANES
    na_w = NA_HEADS * HEAD_DIM
    acc = _dot(hb, w_ref[:, base:base + na_w])
    for s in range(na_w // LANES):
        nq_ref[:, s * LANES:(s + 1) * LANES] = slab(acc, s, nag_ref[0:1, :], False, ATTN_SCALE)
    acc = _dot(hb, w_ref[:, base + na_w:base + 2 * na_w])
    for s in range(na_w // LANES):
        nk_ref[:, s * LANES:(s + 1) * LANES] = slab(acc, s, nag_ref[1:2, :], False, 1.0)
    nv_ref[...] = _dot(hb, w_ref[:, base + 2 * na_w:base + 3 * na_w]).astype(BF16)


def _odd_proj(x, shift, scale, gain, w, gqa_gain, na_gain, rope):
    bsz, n, d = x.shape
    tm = min(PROJ_ROWS, n)
    gqg = jnp.tile(gqa_gain.astype(F32), (1, 2))
    nag = jnp.tile(na_gain.astype(F32), (1, 2))
    in_specs = [
        _row_spec(tm, d), _bcast_spec(d), _bcast_spec(d), _const_spec((1, d)), _const_spec(w.shape),
        _const_spec((2, LANES)), _const_spec((2, LANES)), _const_spec((LANES, LANES)),
    ]
    args = [x, shift, scale, gain.reshape(1, d), w, gqg, nag, _seg_mean_matrix()]
    if rope:
        tab_spec = pl.BlockSpec((tm, LANES), lambda b, i: (i, 0))
        in_specs += [tab_spec] * 3
        args += list(_rope_tables(n))
    widths = [GQA_HEADS * HEAD_DIM, 2 * LANES, NA_HEADS * HEAD_DIM, NA_HEADS * HEAD_DIM, NA_HEADS * HEAD_DIM]
    return pl.pallas_call(
        functools.partial(_odd_proj_kernel, rope=rope),
        grid=(bsz, n // tm),
        in_specs=in_specs,
        out_specs=[_row_spec(tm, wd) for wd in widths],
        out_shape=[jax.ShapeDtypeStruct((bsz, n, wd), BF16) for wd in widths],
        compiler_params=_cparams(("parallel", "parallel")),
        name="odd_proj_x" if rope else "odd_proj_ctx",
    )(*args)


def _flash_scratch(rows):
    return [pltpu.VMEM((rows, FLASH_TK), F32),
            pltpu.VMEM((rows, FLASH_TK), F32),
            pltpu.VMEM((rows, 1), F32),
            pltpu.VMEM((rows, 2 * LANES), F32)]


def _flash(qs, sources, scratch):
    s0_ref, s1_ref, m_ref, acc_ref = scratch
    s_refs = (s0_ref, s1_ref)
    rows = qs.shape[0]
    tk = s0_ref.shape[1]
    m_ref[...] = jnp.full(m_ref.shape, -jnp.inf, F32)
    acc_ref[...] = jnp.zeros(acc_ref.shape, F32)

    def issue(slot, width, k):
        s_refs[slot][:, 0:width] = _dot_nt(qs, k)

    def consume(slot, width, v):
        s = s_refs[slot][:, 0:width]
        m_old = m_ref[...]
        m_new = jnp.maximum(m_old, jnp.max(s, axis=-1, keepdims=True))
        alpha = jnp.exp(m_old - m_new)
        p = jnp.exp((s - m_new).astype(BF16))
        ones = (lax.broadcasted_iota(I32, (width, LANES), 1) == 0).astype(BF16)
        m_ref[...] = m_new
        acc_ref[...] = alpha * acc_ref[...] + _dot(p, jnp.concatenate([v, ones], axis=1))

    issued = 0
    prev = None
    for k_ref, v_ref, length in sorted(sources, key=lambda src: src[2]):
        chunk = min(tk, length)
        steps = length // chunk
        if steps == 1:
            slot = issued % 2
            issue(slot, chunk, k_ref[...])
            if prev is not None:
                consume(prev[0], prev[1], prev[2]())
            prev = (slot, chunk, lambda v_ref=v_ref: v_ref[...])
            issued += 1
            continue
        assert steps % 2 == 0 and chunk == tk
        base = issued % 2

        def kv(ref, c):
            return ref[pl.ds(pl.multiple_of(c * tk, tk), tk), :]

        issue(base, tk, kv(k_ref, 0))
        if prev is not None:
            consume(prev[0], prev[1], prev[2]())

        def body(j, carry, k_ref=k_ref, v_ref=v_ref, base=base):
            issue(1 - base, tk, kv(k_ref, 2 * j + 1))
            consume(base, tk, kv(v_ref, 2 * j))
            issue(base, tk, kv(k_ref, 2 * j + 2))
            consume(1 - base, tk, kv(v_ref, 2 * j + 1))
            return carry

        lax.fori_loop(0, steps // 2 - 1, body, 0)
        issue(1 - base, tk, kv(k_ref, steps - 1))
        consume(base, tk, kv(v_ref, steps - 2))
        prev = (1 - base, tk, lambda v_ref=v_ref, steps=steps: kv(v_ref, steps - 1))
        issued += steps
    consume(prev[0], prev[1], prev[2]())
    return acc_ref[:, 0:LANES] / acc_ref[:, LANES:LANES + 1]


def _lane_ids(shape):
    return lax.broadcasted_iota(I32, shape, len(shape) - 1)


def _diff_attn_kernel(*refs, n_src, lens, lam_init):
    q_ref = refs[0]
    kv_refs = refs[1:1 + 2 * n_src]
    lam_ref, gain_ref, o_ref = refs[1 + 2 * n_src:4 + 2 * n_src]
    scratch = refs[4 + 2 * n_src:]
    q = q_ref[...]
    tq = q.shape[0]
    lo = _lane_ids(q.shape) < HEAD_DIM
    zero = jnp.zeros_like(q)
    qs = jnp.concatenate([jnp.where(lo, q, zero), jnp.where(lo, zero, q)], axis=0)
    sources = [(kv_refs[2 * i], kv_refs[2 * i + 1], lens[i]) for i in range(n_src)]
    a = _flash(qs, sources, scratch)
    lp = lam_ref[...]
    lam = (jnp.exp(jnp.sum(lp[0:1, :] * lp[1:2, :], axis=-1, keepdims=True))
           - jnp.exp(jnp.sum(lp[2:3, :] * lp[3:4, :], axis=-1, keepdims=True)) + lam_init)
    o = a[0:tq, :] - lam * a[tq:2 * tq, :]
    ms = jnp.mean(o * o, axis=-1, keepdims=True)
    o = o * lax.rsqrt(ms + EPS) * gain_ref[...] * (1.0 - lam_init)
    o_ref[...] = o.astype(BF16)


def _diff_attention(q, kv_list, lam_params, out_gain, lam_init):
    bsz, n, width = q.shape
    tq = min(DIFF_TQ, n)
    lens = tuple(k.shape[1] for k, _ in kv_list)
    in_specs = [pl.BlockSpec((None, tq, LANES), lambda b, h, i: (b, i, h))]
    args = [q]
    for (k, v), length in zip(kv_list, lens):
        spec = pl.BlockSpec((None, length, LANES), lambda b, h, i: (b, 0, h))
        in_specs += [spec, spec]
        args += [k, v]
    in_specs += [pl.BlockSpec(lam_params.shape, lambda b, h, i: (0, 0)),
                 pl.BlockSpec((1, LANES), lambda b, h, i: (0, 0))]
    args += [lam_params.astype(F32), out_gain.reshape(1, LANES).astype(F32)]
    return pl.pallas_call(
        functools.partial(_diff_attn_kernel, n_src=len(kv_list), lens=lens, lam_init=lam_init),
        grid=(bsz, DIFF_HEADS, n // tq),
        in_specs=in_specs,
        out_specs=pl.BlockSpec((None, tq, LANES), lambda b, h, i: (b, i, h)),
        out_shape=jax.ShapeDtypeStruct((bsz, n, width), BF16),
        scratch_shapes=_flash_scratch(2 * tq),
        compiler_params=_cparams(("parallel", "parallel", "parallel")),
        name="diff_attn",
    )(*args)


def _gqa_kernel(q_ref, kc_ref, vc_ref, kx_ref, vx_ref, o_ref, *scratch, lens):
    tq = q_ref.shape[0]
    sources = [(kc_ref, vc_ref, lens[0]), (kx_ref, vx_ref, lens[1])]
    lanes = _lane_ids((tq, LANES))
    for kv in range(GQA_KV_HEADS):
        mine = (lanes // HEAD_DIM) == kv
        rows = []
        for half in range(2):
            hh = q_ref[:, (2 * kv + half) * LANES:(2 * kv + half + 1) * LANES]
            sw = pltpu.roll(hh.astype(F32), HEAD_DIM, 1).astype(BF16)
            zero = jnp.zeros_like(hh)
            a_here, b_here = (hh, sw) if kv == 0 else (sw, hh)
            rows += [jnp.where(mine, a_here, zero), jnp.where(mine, b_here, zero)]
        qs = jnp.concatenate(rows, axis=0)
        o = _flash(qs, sources, scratch)
        for half in range(2):
            oa = o[(2 * half) * tq:(2 * half + 1) * tq, :]
            ob = o[(2 * half + 1) * tq:(2 * half + 2) * tq, :]
            oa_sw = pltpu.roll(oa, HEAD_DIM, 1)
            ob_sw = pltpu.roll(ob, HEAD_DIM, 1)
            if kv == 0:
                res = jnp.where(lanes < HEAD_DIM, oa, ob_sw)
            else:
                res = jnp.where(lanes < HEAD_DIM, oa_sw, ob)
            o_ref[:, (2 * kv + half) * LANES:(2 * kv + half + 1) * LANES] = res.astype(BF16)


def _gqa_attention(q, kv_c, kv_x):
    bsz, n, width = q.shape
    tq = min(GQA_TQ, n)
    lens = (kv_c.shape[1], kv_x.shape[1])

    def kspec(length, col):
        return pl.BlockSpec((None, length, LANES), lambda b, i, col=col: (b, 0, col))

    return pl.pallas_call(
        functools.partial(_gqa_kernel, lens=lens),
        grid=(bsz, n // tq),
        in_specs=[pl.BlockSpec((None, tq, width), lambda b, i: (b, i, 0)),
                  kspec(lens[0], 0), kspec(lens[0], 1), kspec(lens[1], 0), kspec(lens[1], 1)],
        out_specs=pl.BlockSpec((None, tq, width), lambda b, i: (b, i, 0)),
        out_shape=jax.ShapeDtypeStruct((bsz, n, width), BF16),
        scratch_shapes=_flash_scratch(4 * tq),
        compiler_params=_cparams(("parallel", "parallel")),
        name="gqa_attn",
    )(q, kv_c, kv_c, kv_x, kv_x)


GLA_LEVELS = tuple(HGRN_CHUNK >> (i + 1) for i in range(HGRN_CHUNK.bit_length() - 1))


def _gla_tables(reverse):
    c = HGRN_CHUNK
    p = np.arange(c)
    t, u = p[:, None], p[None, :]
    exps = [(u <= t), (u > t)]
    pair = [np.eye(c, dtype=bool)]
    target = []
    for b in GLA_LEVELS:
        later = (p // b) % 2 == 1
        bnd = (p // (2 * b)) * (2 * b) + b - 1
        e = np.where(later[:, None], (u > bnd[:, None]) & (u <= t), (u > t) & (u <= bnd[:, None]))
        exps.append(e)
        pair.append((t // (2 * b)) == (u // (2 * b)))
        target.append(later)
    flip = (lambda m: m[::-1, ::-1]) if reverse else (lambda m: m)
    emat = np.concatenate([flip(e) for e in exps], axis=0).astype(np.float32)
    pmat = np.concatenate([flip(m) for m in pair], axis=0).astype(np.float32)
    return emat, pmat


def _gla_chunks(chains, emats, pmats):
    c = HGRN_CHUNK
    ridx = lax.broadcasted_iota(I32, (c, 1), 0)
    xs = []
    for ch in chains:
        g_hi = ch[2].astype(BF16)
        g_lo = (ch[2] - g_hi.astype(F32)).astype(BF16)
        xs.append(jnp.exp(_dot(emats[ch[4]], g_hi) + _dot(emats[ch[4]], g_lo)))
    vbs = [ch[3].astype(BF16) for ch in chains]
    qds = [(ch[0] * x[0:c]).astype(BF16) for ch, x in zip(chains, xs)]
    incs = [_dot_tn(vb, (ch[1] * x[c:2 * c]).astype(BF16)) for ch, x, vb in zip(chains, xs, vbs)]
    tots = [x[0:1] if ch[4] else x[c - 1:c] for ch, x in zip(chains, xs)]
    scores = [pmats[ch[4]][0:c] * _dot_nt(ch[0].astype(BF16), ch[1].astype(BF16)) for ch in chains]
    for lvl in range(len(GLA_LEVELS)):
        for ci, (ch, x) in enumerate(zip(chains, xs)):
            xl = x[(2 + lvl) * c:(3 + lvl) * c]
            pos = (c - 1 - ridx) if ch[4] else ridx
            later = ((pos >> (GLA_LEVELS[lvl].bit_length() - 1)) & 1) == 1
            qt = jnp.where(later, ch[0] * xl, 0.0).astype(BF16)
            kt = jnp.where(later, 0.0, ch[1] * xl).astype(BF16)
            scores[ci] = scores[ci] + pmats[ch[4]][(1 + lvl) * c:(2 + lvl) * c] * _dot_nt(qt, kt)
    return [(_dot(a.astype(BF16), vb), qd, inc, tot)
            for a, vb, qd, inc, tot in zip(scores, vbs, qds, incs, tots)]


def _gla_apply(part, st):
    intra, qd, inc, tot = part
    return intra + _dot_nt(qd, st.astype(BF16)), st * tot + inc


def _hgrn_kernel(qx_ref, kfx_ref, gfx_ref, kbx_ref, gbx_ref, vx_ref, sgx_ref,
                 qc_ref, kfc_ref, gfc_ref, kbc_ref, gbc_ref, vc_ref, sgc_ref,
                 gain_ref, ef_ref, eb_ref, pf_ref, pb_ref, ox_ref, oc_ref, accx_ref, accc_ref):
    c = HGRN_CHUNK
    emats = {False: ef_ref[...].astype(BF16), True: eb_ref[...].astype(BF16)}
    pmats = {False: pf_ref[...], True: pb_ref[...]}
    heads = qx_ref.shape[1] // LANES

    def sweep(q_ref, kf_ref, gf_ref, kb_ref, gb_ref, v_ref, acc_ref, states):
        nchunks = q_ref.shape[0] // c
        acc_ref[...] = jnp.zeros(acc_ref.shape, F32)

        per = HGRN_CHUNKS_PER_STEP
        assert nchunks % per == 0

        def body(i, carry):
            starts = {False: [pl.multiple_of((i * per + j) * c, c) for j in range(per)],
                      True: [pl.multiple_of((nchunks - 1 - i * per - j) * c, c) for j in range(per)]}
            chains = []
            for h in range(heads):
                cols = slice(h * LANES, (h + 1) * LANES)
                for rev, k_ref, g_ref in ((False, kf_ref, gf_ref), (True, kb_ref, gb_ref)):
                    for r0 in starts[rev]:
                        chains.append(tuple(ref[pl.ds(r0, c), cols].astype(F32)
                                            for ref in (q_ref, k_ref, g_ref, v_ref)) + (rev,))
            parts = iter(_gla_chunks(chains, emats, pmats))
            new_states = []
            for h in range(heads):
                cols = slice(h * LANES, (h + 1) * LANES)
                sts = []
                for rev in (False, True):
                    st = carry[h][rev]
                    for r0 in starts[rev]:
                        out, st = _gla_apply(next(parts), st)
                        acc_ref[pl.ds(r0, c), cols] += out
                    sts.append(st)
                new_states.append(tuple(sts))
            return tuple(new_states)

        return lax.fori_loop(0, nchunks // per, body, states)

    zero = jnp.zeros((LANES, LANES), F32)
    states = sweep(qc_ref, kfc_ref, gfc_ref, kbc_ref, gbc_ref, vc_ref, accc_ref,
                   tuple((zero, zero) for _ in range(heads)))
    sweep(qx_ref, kfx_ref, gfx_ref, kbx_ref, gbx_ref, vx_ref, accx_ref, states)

    def finish(acc_ref, sg_ref, o_ref):
        rows = acc_ref.shape[0]
        tile = min(rows, 512)

        def body(i, _):
            r = pl.multiple_of(i * tile, tile)
            for h in range(heads):
                cols = slice(h * LANES, (h + 1) * LANES)
                o = acc_ref[pl.ds(r, tile), cols]
                ms = jnp.mean(o * o, axis=-1, keepdims=True)
                o = o * lax.rsqrt(ms + EPS) * gain_ref[...]
                o_ref[pl.ds(r, tile), cols] = (o * sg_ref[pl.ds(r, tile), cols].astype(F32)).astype(BF16)
            return 0

        lax.fori_loop(0, rows // tile, body, 0)

    finish(accx_ref, sgx_ref, ox_ref)
    finish(accc_ref, sgc_ref, oc_ref)


def _hgrn(px, pc, out_gain):
    bsz, n, width = px[0].shape
    m = pc[0].shape[1]
    c = HGRN_CHUNK
    (ef, pf), (eb, pb) = _gla_tables(False), _gla_tables(True)

    bw = HGRN_HEADS_PER_STEP * LANES

    def spec(length):
        return pl.BlockSpec((None, length, bw), lambda b, h: (b, 0, h))

    const = lambda shape: pl.BlockSpec(shape, lambda b, h: (0, 0))
    return pl.pallas_call(
        _hgrn_kernel,
        grid=(bsz, width // bw),
        in_specs=[spec(n)] * 7 + [spec(m)] * 7 + [const((1, LANES)), const(ef.shape), const(eb.shape),
                                                   const(pf.shape), const(pb.shape)],
        out_specs=[spec(n), spec(m)],
        out_shape=[jax.ShapeDtypeStruct((bsz, n, width), BF16), jax.ShapeDtypeStruct((bsz, m, width), BF16)],
        scratch_shapes=[pltpu.VMEM((n, bw), F32), pltpu.VMEM((m, bw), F32)],
        compiler_params=_cparams(("parallel", "parallel")),
        name="hgrn2",
    )(*px, *pc, out_gain.reshape(1, LANES).astype(F32), ef, eb, pf, pb)


def _na_bias_tables(rpb, rows):
    qrows, band, heads = NA_QROWS, NA_BAND, rpb.shape[0]
    pad = GRID_W - NA_COLS
    wide = jnp.pad(rpb.astype(F32), ((0, 0), (0, 0), (pad, pad)))
    toeplitz = jnp.stack([wide[:, :, GRID_W - 1 - qc:2 * GRID_W - 1 - qc] for qc in range(GRID_W)], axis=2)
    qc = np.arange(GRID_W)[:, None]
    kc = np.arange(GRID_W)[None, :]
    cstart = np.clip(qc - NA_COLS // 2, 0, GRID_W - NA_COLS)
    col_ok = (kc >= cstart) & (kc < cstart + NA_COLS)
    toeplitz = jnp.where(col_ok[None, None], toeplitz, NEG_BIG)
    tabs = []
    for r0 in (0, qrows, rows - qrows):
        rs = min(max(r0 - NA_ROWS // 2, 0), rows - band)
        qr = r0 + np.arange(qrows)[:, None]
        kr = rs + np.arange(band)[None, :]
        rstart = np.clip(qr - NA_ROWS // 2, 0, rows - NA_ROWS)
        row_ok = (kr >= rstart) & (kr < rstart + NA_ROWS)
        dr = np.clip(kr - qr + NA_ROWS - 1, 0, 2 * NA_ROWS - 2)
        tiles = jnp.take(toeplitz, jnp.asarray(dr.reshape(-1), I32), axis=1)
        tiles = jnp.where(row_ok.reshape(1, -1, 1, 1), tiles, NEG_BIG)
        tiles = tiles.reshape(heads, qrows, band, GRID_W, GRID_W).transpose(0, 1, 3, 2, 4)
        tabs.append(tiles.reshape(heads, qrows * GRID_W, band * GRID_W))
    return jnp.stack(tabs)


def _na_kernel(q_ref, k_ref, v_ref, kc_ref, vc_ref, bias_ref, o_ref, *, rows):
    j = pl.program_id(2)
    tq = q_ref.shape[0]
    band = NA_BAND * GRID_W
    rs = jnp.clip(j * NA_QROWS - NA_ROWS // 2, 0, rows - NA_BAND)
    start = pl.multiple_of(rs * GRID_W, NA_ROWS // 2 * GRID_W)
    q = q_ref[...]
    lo = _lane_ids(q.shape) < HEAD_DIM
    zero = jnp.zeros_like(q)
    qh = [jnp.where(lo, q, zero), jnp.where(lo, zero, q)]
    kb = k_ref[pl.ds(start, band), :]
    kc = kc_ref[...]

    def with_ones(v):
        ones = (lax.broadcasted_iota(I32, v.shape, 1) == 0).astype(BF16)
        return jnp.concatenate([v, ones], axis=1)

    vb = with_ones(v_ref[pl.ds(start, band), :])
    vc = with_ones(vc_ref[...])
    s_win = [_dot_nt(qh[h], kb) + bias_ref[h] for h in range(2)]
    s_ctx = [_dot_nt(qh[h], kc) for h in range(2)]
    m = [jnp.maximum(jnp.max(s_win[h], axis=-1, keepdims=True), jnp.max(s_ctx[h], axis=-1, keepdims=True))
         for h in range(2)]
    p_win = [jnp.exp((s_win[h] - m[h]).astype(BF16)) for h in range(2)]
    p_ctx = [jnp.exp((s_ctx[h] - m[h]).astype(BF16)) for h in range(2)]
    pv = [_dot(p_ctx[h], vc) + _dot(p_win[h], vb) for h in range(2)]
    o = [pv[h][:, 0:LANES] / pv[h][:, LANES:LANES + 1] for h in range(2)]
    o_ref[...] = jnp.where(lo, o[0], o[1]).astype(BF16)


def _na_attention(q, k, v, kc, vc, rpb):
    bsz, n, width = q.shape
    rows = n // GRID_W
    tq = NA_QROWS * GRID_W
    nt = n // tq
    bias = _na_bias_tables(rpb, rows)
    m = kc.shape[1]

    def cls(j):
        return jnp.where(j == 0, 0, jnp.where(j == nt - 1, 2, 1))

    full = lambda length: pl.BlockSpec((None, length, LANES), lambda b, h, j: (b, 0, h))
    return pl.pallas_call(
        functools.partial(_na_kernel, rows=rows),
        grid=(bsz, NA_HEADS // 2, nt),
        in_specs=[pl.BlockSpec((None, tq, LANES), lambda b, h, j: (b, j, h)),
                  full(n), full(n), full(m), full(m),
                  pl.BlockSpec((None, 2, tq, NA_BAND * GRID_W), lambda b, h, j: (cls(j), h, 0, 0))],
        out_specs=pl.BlockSpec((None, tq, LANES), lambda b, h, j: (b, j, h)),
        out_shape=jax.ShapeDtypeStruct((bsz, n, width), BF16),
        compiler_params=_cparams(("parallel", "parallel", "parallel")),
        name="na_attn",
    )(q, k, v, kc, vc, bias)


def _out_proj_kernel(a_ref, b_ref, w_ref, x_ref, gate_ref, gain_ref, shift_ref, scale_ref, r_ref,
                     x1_ref, tok_ref, logit_ref):
    half = a_ref.shape[1]
    y = _dot(a_ref[...], w_ref[0:half, :]) + _dot(b_ref[...], w_ref[half:2 * half, :])
    x1 = x_ref[...] + gate_ref[...] * y
    x1_ref[...] = x1
    h = _norm_mod(x1, gain_ref[...], shift_ref[...], scale_ref[...])
    r = r_ref[...]
    h_hi, r_hi = h.astype(BF16), r.astype(BF16)
    h_lo = (h - h_hi.astype(F32)).astype(BF16)
    r_lo = (r - r_hi.astype(F32)).astype(BF16)
    logit_ref[...] = _dot(h_hi, r_hi) + (_dot(h_hi, r_lo) + _dot(h_lo, r_hi))
    tok_ref[...] = _pack_pair(h[:, 0:HALF], h[:, HALF:2 * HALF])


def _out_proj(a, b, w, x, gate, gain, shift, scale, router_pad):
    bsz, n, d = x.shape
    tm = min(PROJ_ROWS, n)
    return pl.pallas_call(
        _out_proj_kernel,
        grid=(bsz, n // tm),
        in_specs=[_row_spec(tm, a.shape[2]), _row_spec(tm, b.shape[2]), _const_spec(w.shape), _row_spec(tm, d),
                  _bcast_spec(d), _const_spec((1, d)), _bcast_spec(d), _bcast_spec(d), _const_spec(router_pad.shape)],
        out_specs=[_row_spec(tm, d), _row_spec(tm, HALF), _row_spec(tm, LANES)],
        out_shape=[jax.ShapeDtypeStruct((bsz, n, d), F32), jax.ShapeDtypeStruct((bsz, n, HALF), I32),
                   jax.ShapeDtypeStruct((bsz, n, LANES), F32)],
        compiler_params=_cparams(("parallel", "parallel")),
        name="out_proj",
    )(a, b, w, x, gate, gain.reshape(1, d), shift, scale, router_pad)


def _route_kernel(logit_ref, bias_ref, tri_ref, etri_ref, w_ref, loc_ref, tcnt_ref, tcarry_ref, toff_ref, cnt_ref,
                  masked_ref, carry_ref):
    step = pl.program_id(0)

    @pl.when(step == 0)
    def _():
        carry_ref[...] = jnp.zeros_like(carry_ref)

    tr = logit_ref.shape[0]
    scores = jax.nn.sigmoid(logit_ref[...].T[0:N_EXPERTS, :])
    biased = scores + bias_ref[...]
    gsz = EXPERTS_PER_GROUP
    sub = lax.broadcasted_iota(I32, (gsz, tr), 0).astype(F32)
    gscore = []
    for g in range(N_GROUPS):
        bg = biased[g * gsz:(g + 1) * gsz, :]
        m1 = jnp.max(bg, axis=0, keepdims=True)
        i1 = jnp.min(jnp.where(bg == m1, sub, float(gsz)), axis=0, keepdims=True)
        m2 = jnp.max(jnp.where(sub == i1, -jnp.inf, bg), axis=0, keepdims=True)
        gscore.append(m1 + m2)
    for g in range(N_GROUPS):
        beaten = jnp.zeros((1, tr), F32)
        for o in range(N_GROUPS):
            if o == g:
                continue
            wins = (gscore[o] >= gscore[g]) if o < g else (gscore[o] > gscore[g])
            beaten = beaten + jnp.where(wins, 1.0, 0.0)
        keep = beaten < float(TOPK_GROUPS)
        masked_ref[g * gsz:(g + 1) * gsz, :] = jnp.where(keep, biased[g * gsz:(g + 1) * gsz, :], -jnp.inf)
    cur = masked_ref[...]
    eid = lax.broadcasted_iota(I32, (N_EXPERTS, tr), 0).astype(F32)
    sel = jnp.zeros((N_EXPERTS, tr), F32)
    picks, weights = [], []
    for _ in range(TOP_K):
        m = jnp.max(cur, axis=0, keepdims=True)
        ik = jnp.min(jnp.where(cur == m, eid, float(N_EXPERTS)), axis=0, keepdims=True)
        hit = eid == ik
        weights.append(jnp.sum(jnp.where(hit, scores, 0.0), axis=0, keepdims=True))
        sel = sel + jnp.where(hit, 1.0, 0.0)
        cur = jnp.where(hit, -jnp.inf, cur)
        picks.append(ik)
    wsum = weights[0]
    for wk in weights[1:]:
        wsum = wsum + wk
    tile_cnt = jnp.broadcast_to(jnp.sum(sel, axis=1, keepdims=True), (N_EXPERTS, LANES))
    tile_cnt = jnp.floor((tile_cnt + (RUN_ALIGN - 1)) * (1.0 / RUN_ALIGN)) * RUN_ALIGN
    tile_off = jnp.dot(etri_ref[...], tile_cnt, precision=HIGHEST, preferred_element_type=F32)
    row = _dot(sel.astype(BF16), tri_ref[...]) + tile_off[:, 0:1]
    for kk in range(TOP_K):
        w_ref[kk:kk + 1, :] = weights[kk] / wsum * ROUTED_SCALE
        loc_ref[kk:kk + 1, :] = jnp.sum(jnp.where(eid == picks[kk], row, 0.0), axis=0, keepdims=True).astype(I32)
    tcnt_ref[...] = tile_cnt.astype(I32)
    tcarry_ref[...] = carry_ref[...].astype(I32)
    toff_ref[...] = tile_off.astype(I32)
    carry_ref[...] = carry_ref[...] + tile_cnt
    cnt_ref[...] = carry_ref[...].astype(I32)


def _route(logits, router_bias):
    t = logits.shape[0]
    tr = MOE_TILE
    assert t % tr == 0
    nt = t // tr
    r = jnp.arange(tr)
    tri = (r[:, None] < r[None, :]).astype(BF16)
    e = jnp.arange(N_EXPERTS)
    etri = (e[:, None] > e[None, :]).astype(F32)
    kt_spec = pl.BlockSpec((TOP_K, tr), lambda i: (0, i))
    tile_spec = pl.BlockSpec((None, N_EXPERTS, LANES), lambda i: (i, 0, 0))
    tile_shape = jax.ShapeDtypeStruct((nt, N_EXPERTS, LANES), I32)
    return pl.pallas_call(
        _route_kernel,
        grid=(nt,),
        in_specs=[pl.BlockSpec((tr, LANES), lambda i: (i, 0)),
                  pl.BlockSpec((N_EXPERTS, 1), lambda i: (0, 0)),
                  pl.BlockSpec((tr, tr), lambda i: (0, 0)),
                  pl.BlockSpec((N_EXPERTS, N_EXPERTS), lambda i: (0, 0))],
        out_specs=[kt_spec, kt_spec, tile_spec, tile_spec, tile_spec,
                   pl.BlockSpec((N_EXPERTS, LANES), lambda i: (0, 0))],
        out_shape=[jax.ShapeDtypeStruct((TOP_K, t), F32), jax.ShapeDtypeStruct((TOP_K, t), I32),
                   tile_shape, tile_shape, tile_shape, jax.ShapeDtypeStruct((N_EXPERTS, LANES), I32)],
        scratch_shapes=[pltpu.VMEM((N_EXPERTS, tr), F32), pltpu.VMEM((N_EXPERTS, LANES), F32)],
        compiler_params=_cparams(("arbitrary",)),
        name="moe_route",
    )(logits, router_bias.astype(F32).reshape(N_EXPERTS, 1), tri, etri)


TAB_WORDS = 1024
TAB_FIELD = 128
N_RUNS = N_EXPERTS + 1
TILE_ROWS = -(-(MOE_TILE * TOP_K + N_EXPERTS * (RUN_ALIGN - 1)) // MOE_TILE) * MOE_TILE
FILLER_ROWS = TILE_ROWS - MOE_TILE * TOP_K
RUN_PIECE = 64


def _run_copies(tab_ref, local_ref, global_ref, sem, to_global):
    def piece(ls, gs, off, rows, priority):
        loc = local_ref.at[pl.ds(pl.multiple_of(ls + off, RUN_ALIGN), rows)]
        glo = global_ref.at[pl.ds(pl.multiple_of(gs + off, RUN_ALIGN), rows)]
        src, dst = (loc, glo) if to_global else (glo, loc)
        pltpu.make_async_copy(src, dst, sem).start(priority=priority)

    def body(e, carry):
        gs = tab_ref[e]
        c = tab_ref[TAB_FIELD + e]
        ls = tab_ref[2 * TAB_FIELD + e]
        whole = c >> (RUN_PIECE.bit_length() - 1)

        def big(j, carry2):
            piece(ls, gs, j * RUN_PIECE, RUN_PIECE, 0)
            return carry2

        lax.fori_loop(0, whole, big, 0)
        for b in range(RUN_ALIGN.bit_length() - 1, RUN_PIECE.bit_length() - 1):
            @pl.when(((c >> b) & 1) == 1)
            def _(b=b):
                piece(ls, gs, whole * RUN_PIECE + (c & ((1 << b) - 1) & (RUN_PIECE - 1)), 1 << b, 1)
        return carry

    lax.fori_loop(0, N_RUNS, body, 0)


def _run_wait(local_ref, global_ref, sem):
    pltpu.make_async_copy(global_ref.at[pl.ds(0, TILE_ROWS)], local_ref, sem).wait()


def _dispatch_kernel(tab_ref, loc_ref, tok_ref, rows_in_ref, rows_ref, srt_ref, sem, *, n_tiles):
    del rows_in_ref
    i = pl.program_id(0)
    slot = i % 2
    tt = tok_ref.shape[0]
    lo, hi = _unpack_pair(tok_ref[...])
    lo, hi = lo.astype(BF16), hi.astype(BF16)
    loc = loc_ref[...]
    blk = loc >> (tt.bit_length() - 1)
    low = (loc & (tt - 1)).astype(F32).astype(BF16)
    r = lax.broadcasted_iota(I32, (tt, tt), 0).astype(F32).astype(BF16)
    one = jnp.ones((tt, tt), BF16)
    for rb in range(TILE_ROWS // tt):
        pb = jnp.zeros((tt, tt), BF16)
        for kk in range(TOP_K):
            lk = jnp.where(blk[kk:kk + 1, :] == rb, low[kk:kk + 1, :], jnp.asarray(-1.0, BF16))
            pb = jnp.where(lk == r, one, pb)
        srt_ref[slot, rb * tt:(rb + 1) * tt, :] = _pack_exact_pair(_dot(pb, lo), _dot(pb, hi))

    @pl.when(i > 0)
    def _():
        _run_wait(srt_ref.at[1 - slot], rows_ref, sem.at[1 - slot])

    _run_copies(tab_ref, srt_ref.at[slot], rows_ref, sem.at[slot], True)

    @pl.when(i == n_tiles - 1)
    def _():
        _run_wait(srt_ref.at[slot], rows_ref, sem.at[slot])


def _dispatch(tab, loc, tok, rows_buf):
    t = tok.shape[0]
    tt = MOE_TILE
    return pl.pallas_call(
        functools.partial(_dispatch_kernel, n_tiles=t // tt),
        grid=(t // tt,),
        in_specs=[pl.BlockSpec((TAB_WORDS,), lambda i: (i,), memory_space=pltpu.SMEM),
                  pl.BlockSpec((TOP_K, tt), lambda i: (0, i)),
                  pl.BlockSpec((tt, HALF), lambda i: (i, 0)),
                  pl.BlockSpec(memory_space=pl.ANY)],
        out_specs=pl.BlockSpec(memory_space=pl.ANY),
        out_shape=jax.ShapeDtypeStruct(rows_buf.shape, rows_buf.dtype),
        scratch_shapes=[pltpu.VMEM((2, TILE_ROWS, HALF), I32), pltpu.SemaphoreType.DMA((2,))],
        input_output_aliases={3: 0},
        compiler_params=_cparams(("arbitrary",)),
        name="moe_dispatch",
    )(tab, loc, tok, rows_buf)


EXPERT_BUFS = 3


def _expert_kernel(be_ref, nused_ref, x_hbm, wgu_ref, wd_ref, y_ref, xbuf, sem):
    i = pl.program_id(0)
    used = nused_ref[0]

    def fetch(blk):
        slot = blk % EXPERT_BUFS
        rows = x_hbm.at[pl.ds(pl.multiple_of(blk * MOE_BLOCK, MOE_BLOCK), MOE_BLOCK)]
        return pltpu.make_async_copy(rows, xbuf.at[slot], sem.at[slot])

    @pl.when(i == 0)
    def _():
        for j in range(EXPERT_BUFS - 1):
            @pl.when(j < used)
            def _(j=j):
                fetch(j).start()

    @pl.when(i + EXPERT_BUFS - 1 < used)
    def _():
        fetch(i + EXPERT_BUFS - 1).start()

    @pl.when(i < used)
    def _():
        fetch(i).wait()
        lo, hi = _unpack_pair(xbuf[i % EXPERT_BUFS])
        gu = _dot(lo.astype(BF16), wgu_ref[0:HALF, :]) + _dot(hi.astype(BF16), wgu_ref[HALF:2 * HALF, :])
        h = (_silu(gu[:, 0:EXPERT_DIM]) * gu[:, EXPERT_DIM:2 * EXPERT_DIM]).astype(BF16)
        y = _dot(h, wd_ref[...])
        y_ref[...] = _pack_pair(y[:, 0:HALF], y[:, HALF:2 * HALF])

    @pl.when(pl.program_id(0) >= nused_ref[0])
    def _():
        y_ref[...] = jnp.zeros_like(y_ref)


def _experts(block_expert, nused, rows, wgu, wd):
    n_rows = rows.shape[0]
    nb = n_rows // MOE_BLOCK

    def row_map(i, be, nu):
        return (jnp.minimum(i, nu[0] - 1), 0)

    def w_map(i, be, nu):
        return (be[jnp.minimum(i, nu[0] - 1)], 0, 0)

    grid_spec = pltpu.PrefetchScalarGridSpec(
        num_scalar_prefetch=2,
        grid=(nb,),
        in_specs=[pl.BlockSpec(memory_space=pl.ANY),
                  pl.BlockSpec((None, D_MODEL, 2 * EXPERT_DIM), w_map),
                  pl.BlockSpec((None, EXPERT_DIM, D_MODEL), w_map)],
        out_specs=pl.BlockSpec((MOE_BLOCK, HALF), lambda i, be, nu: (i, 0)),
        scratch_shapes=[pltpu.VMEM((EXPERT_BUFS, MOE_BLOCK, HALF), I32), pltpu.SemaphoreType.DMA((EXPERT_BUFS,))],
    )
    return pl.pallas_call(
        _expert_kernel,
        grid_spec=grid_spec,
        out_shape=jax.ShapeDtypeStruct((n_rows, HALF), I32),
        compiler_params=_cparams(("arbitrary",)),
        name="moe_experts",
    )(block_expert, nused, rows, wgu, wd)


def _combine_kernel(tab_ref, tabn_ref, x1_ref, tok_ref, loc_ref, w_ref, gate_ref, wgu_ref, wd_ref, y_ref, o_ref,
                    buf_ref, sem, *, n_tiles):
    i = pl.program_id(0)
    slot = i % 2
    tt = tok_ref.shape[0]

    @pl.when(i == 0)
    def _():
        _run_copies(tab_ref, buf_ref.at[slot], y_ref, sem.at[slot], False)

    @pl.when(i + 1 < n_tiles)
    def _():
        _run_copies(tabn_ref, buf_ref.at[1 - slot], y_ref, sem.at[1 - slot], False)

    lo, hi = _unpack_pair(tok_ref[...])
    gu = _dot(lo.astype(BF16), wgu_ref[0:HALF, :]) + _dot(hi.astype(BF16), wgu_ref[HALF:2 * HALF, :])
    h = (_silu(gu[:, 0:EXPERT_DIM]) * gu[:, EXPERT_DIM:2 * EXPERT_DIM]).astype(BF16)
    shared = _dot(h, wd_ref[...])
    _run_wait(buf_ref.at[slot], y_ref, sem.at[slot])
    acc_lo = shared[:, 0:HALF]
    acc_hi = shared[:, HALF:2 * HALF]
    loc = loc_ref[...]
    w = w_ref[...].astype(BF16)
    blk = loc >> (tt.bit_length() - 1)
    low = (loc & (tt - 1)).astype(F32).astype(BF16)
    r = lax.broadcasted_iota(I32, (tt, tt), 1).astype(F32).astype(BF16)
    for rb in range(TILE_ROWS // tt):
        wb = jnp.zeros((tt, tt), BF16)
        for kk in range(TOP_K):
            lk = jnp.where(blk[:, kk:kk + 1] == rb, low[:, kk:kk + 1], jnp.asarray(-1.0, BF16))
            wb = jnp.where(lk == r, jnp.broadcast_to(w[:, kk:kk + 1], (tt, tt)), wb)
        ylo, yhi = _unpack_pair(buf_ref[slot, rb * tt:(rb + 1) * tt, :])
        acc_lo = acc_lo + _dot(wb, ylo.astype(BF16))
        acc_hi = acc_hi + _dot(wb, yhi.astype(BF16))
    gate = gate_ref[...]
    o_ref[:, 0:HALF] = x1_ref[:, 0:HALF] + gate[:, 0:HALF] * acc_lo
    o_ref[:, HALF:2 * HALF] = x1_ref[:, HALF:2 * HALF] + gate[:, HALF:2 * HALF] * acc_hi


def _combine(tab, x1, tok, loc_tok, w_tok, gate, wgu, wd, y_rows, tokens_per_gate):
    t, d = x1.shape
    tt = MOE_TILE
    nt = t // tt
    per = tokens_per_gate // tt
    return pl.pallas_call(
        functools.partial(_combine_kernel, n_tiles=nt),
        grid=(nt,),
        in_specs=[pl.BlockSpec((TAB_WORDS,), lambda i: (i,), memory_space=pltpu.SMEM),
                  pl.BlockSpec((TAB_WORDS,), lambda i: (jnp.minimum(i + 1, nt - 1),), memory_space=pltpu.SMEM),
                  pl.BlockSpec((tt, d), lambda i: (i, 0)),
                  pl.BlockSpec((tt, HALF), lambda i: (i, 0)),
                  pl.BlockSpec((tt, TOP_K), lambda i: (i, 0)),
                  pl.BlockSpec((tt, TOP_K), lambda i: (i, 0)),
                  pl.BlockSpec((None, 1, d), lambda i: (i // per, 0, 0)),
                  pl.BlockSpec(wgu.shape, lambda i: (0, 0)),
                  pl.BlockSpec(wd.shape, lambda i: (0, 0)),
                  pl.BlockSpec(memory_space=pl.ANY)],
        out_specs=pl.BlockSpec((tt, d), lambda i: (i, 0)),
        out_shape=jax.ShapeDtypeStruct((t, d), F32),
        scratch_shapes=[pltpu.VMEM((2, TILE_ROWS, HALF), I32), pltpu.SemaphoreType.DMA((2,))],
        compiler_params=_cparams(("arbitrary",)),
        name="moe_combine",
    )(tab, tab, x1, tok, loc_tok, w_tok, gate, wgu, wd, y_rows)


def _moe(parts, router_bias, w_gate, w_up, w_down, ws_gate, ws_up, ws_down, row_buf=None):
    logits = jnp.concatenate([p[2].reshape(-1, LANES) for p in parts], axis=0)
    t = logits.shape[0]
    w, loc, tile_cnt, tile_carry, tile_off, cnt = _route(logits, router_bias)
    counts = cnt[:, 0]
    padded = (counts + MOE_BLOCK - 1) // MOE_BLOCK * MOE_BLOCK
    pad_end = jnp.cumsum(padded)
    pad_start = pad_end - padded
    nt = t // MOE_TILE
    max_aligned = t * TOP_K + nt * N_EXPERTS * (RUN_ALIGN - 1)
    area_rows = -(-max_aligned // MOE_BLOCK) * MOE_BLOCK + N_EXPERTS * MOE_BLOCK
    n_rows = -(-(area_rows + FILLER_ROWS) // MOE_BLOCK) * MOE_BLOCK
    if row_buf is not None:
        assert row_buf.shape[0] >= n_rows
        n_rows = row_buf.shape[0]
    tile_rows_used = tile_off[:, N_EXPERTS - 1, 0] + tile_cnt[:, N_EXPERTS - 1, 0]

    def field(per_expert, filler):
        vals = jnp.concatenate([per_expert.astype(I32), filler.astype(I32)[:, None]], axis=1)
        return jnp.pad(vals, ((0, 0), (0, TAB_FIELD - N_RUNS)))

    tab = jnp.concatenate(
        [field(pad_start[None, :] + tile_carry[:, :, 0], jnp.full((nt,), area_rows, I32)),
         field(tile_cnt[:, :, 0], TILE_ROWS - tile_rows_used),
         field(tile_off[:, :, 0], tile_rows_used),
         jnp.zeros((nt, TAB_WORDS - 3 * TAB_FIELD), I32)], axis=1).reshape(-1)
    w_tok = w.T
    loc_tok = loc.T
    nb = n_rows // MOE_BLOCK
    block_start = jnp.arange(nb, dtype=I32) * MOE_BLOCK
    block_expert = jnp.minimum(jnp.sum(block_start[:, None] >= pad_end[None, :], axis=1), N_EXPERTS - 1).astype(I32)
    nused = (pad_end[-1] // MOE_BLOCK).astype(I32).reshape(1)
    rows = jnp.zeros((n_rows, HALF), I32) if row_buf is None else row_buf
    off = 0
    for x1, tok, _, _, _ in parts:
        cnt_tok = tok.shape[0] * tok.shape[1]
        part_tab = tab[off // MOE_TILE * TAB_WORDS:(off + cnt_tok) // MOE_TILE * TAB_WORDS]
        rows = _dispatch(part_tab, loc[:, off:off + cnt_tok], tok.reshape(cnt_tok, HALF), rows)
        off += cnt_tok
    wgu = jnp.concatenate([w_gate, w_up], axis=-1).astype(BF16)
    y_rows = _experts(block_expert, nused, rows, wgu, w_down.astype(BF16))
    wsgu = jnp.concatenate([ws_gate, ws_up], axis=-1).astype(BF16)
    wsd = ws_down.astype(BF16)
    outs = []
    off = 0
    for x1, tok, _, gate, per in parts:
        cnt_tok = tok.shape[0] * tok.shape[1]
        part_tab = tab[off // MOE_TILE * TAB_WORDS:(off + cnt_tok) // MOE_TILE * TAB_WORDS]
        o = _combine(part_tab, x1.reshape(cnt_tok, D_MODEL), tok.reshape(cnt_tok, HALF),
                     loc_tok[off:off + cnt_tok], w_tok[off:off + cnt_tok], gate, wsgu, wsd, y_rows, per)
        outs.append(o.reshape(x1.shape))
        off += cnt_tok
    return outs, rows


def kernel(x, c, ctx, c_ctx, ada_w, ada_b, norm_mix, norm_ffn, ev_w_in, ev_w_out, diff_qk_gain, diff_lambda,
           diff_out_gain, hgrn_lb, hgrn_out_gain, od_w_in, od_w_out, gqa_qk_gain, na_qk_gain, na_rpb, moe_router,
           moe_router_bias, moe_w_gate, moe_w_up, moe_w_down, shared_w_gate, shared_w_up, shared_w_down):
    bsz, n, d = x.shape
    m = ctx.shape[1]
    depth = ada_w.shape[0]
    cond_rows = -(-(bsz + 1) // 8) * 8
    cond = jnp.zeros((cond_rows, d), F32).at[0:bsz].set(c).at[bsz].set(c_ctx)
    mods = _ada_mod(cond, ada_w, ada_b)

    xc = ctx
    row_buf = None
    for layer in range(depth):
        need_ctx = layer < depth - 1
        j = layer // 2
        mod = mods[layer].reshape(cond_rows, 6, d)
        mx = [mod[0:bsz, i][:, None, :] for i in range(6)]
        mc = [jnp.broadcast_to(mod[bsz:bsz + 1, i][:, None, :], (bsz, 1, d)) for i in range(6)]
        if layer % 2 == 0:
            w_in = ev_w_in[j].astype(BF16)
            px = _even_proj(x, mx[0], mx[1], norm_mix[layer], w_in, diff_qk_gain[j], hgrn_lb[0], hgrn_lb[1], j, True)
            pc = _even_proj(xc, mc[0], mc[1], norm_mix[layer], w_in, diff_qk_gain[j], hgrn_lb[0], hgrn_lb[1], j, False)
            lam_init = 0.8 - 0.6 * math.exp(-0.3 * layer)
            a_x = _diff_attention(px[0], [(pc[1], pc[2]), (px[1], px[2])], diff_lambda[j], diff_out_gain[j], lam_init)
            a_c = _diff_attention(pc[0], [(pc[1], pc[2])], diff_lambda[j], diff_out_gain[j], lam_init)
            b_x, b_c = _hgrn(px[3:], pc[3:], hgrn_out_gain[j])
            w_out = ev_w_out[j].astype(BF16)
        else:
            w_in = od_w_in[j].astype(BF16)
            px = _odd_proj(x, mx[0], mx[1], norm_mix[layer], w_in, gqa_qk_gain[j], na_qk_gain[j], True)
            pc = _odd_proj(xc, mc[0], mc[1], norm_mix[layer], w_in, gqa_qk_gain[j], na_qk_gain[j], False)
            a_x = _gqa_attention(px[0], pc[1], px[1])
            b_x = _na_attention(px[2], px[3], px[4], pc[3], pc[4], na_rpb[j])
            a_c = b_c = None
            w_out = od_w_out[j].astype(BF16)
        router_pad = jnp.zeros((d, LANES), F32).at[:, 0:N_EXPERTS].set(moe_router[layer].astype(F32))
        x1, tok_x, logit_x = _out_proj(a_x, b_x, w_out, x, mx[2], norm_ffn[layer], mx[3], mx[4], router_pad)
        parts = [(x1, tok_x, logit_x, mx[5], n)]
        if need_ctx:
            xc1, tok_c, logit_c = _out_proj(a_c, b_c, w_out, xc, mc[2], norm_ffn[layer], mc[3], mc[4], router_pad)
            parts.append((xc1, tok_c, logit_c, mc[5][0:1], bsz * m))
        outs, row_buf = _moe(parts, moe_router_bias[layer], moe_w_gate[layer], moe_w_up[layer], moe_w_down[layer],
                             shared_w_gate[layer], shared_w_up[layer], shared_w_down[layer], row_buf)
        x = outs[0]
        if need_ctx:
            xc = outs[1]
    return x
```
